```python
import math
import jax, jax.numpy as jnp
from jax import lax
import numpy as np

D_MODEL = 2048
BATCH = 32
SEQ = 256
DEPTH = 4
DEC_BATCH = 8
DEC_SEQ = 2048
PAST_LEN = 256

GRID_W = 64
BLOCK = 128
EPS = 1e-6
NEG_INF = -1e30
RET_HEADS = 4
RET_DK = 128
RET_DV = 128
RET_WIDTH = RET_HEADS * RET_DV
LRU_WIDTH = 512
LRU_BLOCKS = 4
LRU_BLOCK_DIM = LRU_WIDTH // LRU_BLOCKS
CONV_W = 4
CONV_LEFT = 2
LRU_C = 8.0
ATT_Q_HEADS = 4
ATT_KV_HEADS = 2
ATT_GROUP = ATT_Q_HEADS // ATT_KV_HEADS
HEAD_DIM = 128
WINDOW = 128
ROPE_BASE = 10000.0
SSM_WIDTH = 512
SSM_GROUP = 16
SSM_GROUPS = SSM_WIDTH // SSM_GROUP
SSM_STATE = 64
N_BRANCH = 4
BRANCH_WIDTH = 512
D_FF = 4 * D_MODEL
IN_SIZES = (RET_WIDTH, RET_WIDTH, RET_WIDTH, RET_WIDTH, LRU_WIDTH, LRU_WIDTH,
            ATT_Q_HEADS * HEAD_DIM, ATT_KV_HEADS * HEAD_DIM, ATT_KV_HEADS * HEAD_DIM,
            SSM_WIDTH, N_BRANCH * D_MODEL)
IN_TOTAL = 4 * RET_WIDTH + 2 * LRU_WIDTH + (ATT_Q_HEADS + 2 * ATT_KV_HEADS) * HEAD_DIM + SSM_WIDTH + N_BRANCH * D_MODEL

kernel_name = "hybrid_diffusion_prefix_trunk_step"


def rms_norm(x, g):
    xf = x.astype(jnp.float32)
    y = xf * lax.rsqrt(jnp.mean(xf * xf, axis=-1, keepdims=True) + EPS)
    return (y * g.astype(jnp.float32)).astype(x.dtype)


def linear_scan(a, b, h0):
    b = b.at[:, 0].add(a[:, 0] * h0)
    def comb(l, r):
        return (l[0] * r[0], r[0] * l[1] + r[1])
    _, h = lax.associative_scan(comb, (a, b), axis=1)
    return h


def retention_scan(q, k, v, log_gamma, s0):
    B_, L = q.shape[:2]
    n = L // BLOCK
    qc = q.reshape(B_, n, BLOCK, RET_HEADS, RET_DK)
    kc = k.reshape(B_, n, BLOCK, RET_HEADS, RET_DK)
    vc = v.reshape(B_, n, BLOCK, RET_HEADS, RET_DV)
    idx = jnp.arange(BLOCK, dtype=jnp.float32)
    rel = idx[:, None] - idx[None, :]
    decay = jnp.where(rel >= 0, jnp.exp(jnp.maximum(rel, 0.0)[None] * log_gamma[:, None, None]), 0.0)
    scores = jnp.einsum('bnihd,bnjhd->bnhij', qc, kc) * decay
    intra = jnp.einsum('bnhij,bnjhe->bnihe', scores, vc)
    w_end = jnp.exp((BLOCK - 1 - idx)[:, None] * log_gamma[None, :])
    kv = jnp.einsum('bnjhd,bnjhe->bnhde', kc * w_end[..., None], vc)
    chunk_decay = jnp.exp(BLOCK * log_gamma)[:, None, None]
    def step(s, kv_n):
        return chunk_decay * s + kv_n, s
    s_last, s_in = lax.scan(step, s0, jnp.moveaxis(kv, 1, 0))
    w_in = jnp.exp((idx + 1)[:, None] * log_gamma[None, :])
    cross = jnp.einsum('bnihd,bnhde->bnihe', qc * w_in[..., None], jnp.moveaxis(s_in, 0, 1))
    return (intra + cross).reshape(B_, L, RET_HEADS, RET_DV), s_last


def retention(q, k, v, g, decay_logit, gn_gain, s0):
    B_, L, _ = q.shape
    f32 = jnp.float32
    q = q.astype(f32).reshape(B_, L, RET_HEADS, RET_DK) * RET_DK ** -0.5
    k = k.astype(f32).reshape(B_, L, RET_HEADS, RET_DK)
    v = v.astype(f32).reshape(B_, L, RET_HEADS, RET_DV)
    log_gamma = -jax.nn.softplus(-decay_logit.astype(f32))
    o_f, s_f = retention_scan(q, k, v, log_gamma[0], s0[:, 0])
    o_b, s_b = retention_scan(jnp.flip(q, 1), jnp.flip(k, 1), jnp.flip(v, 1), log_gamma[1], s0[:, 1])
    o = o_f + jnp.flip(o_b, 1)
    mu = jnp.mean(o, axis=-1, keepdims=True)
    var = jnp.mean(jnp.square(o - mu), axis=-1, keepdims=True)
    o = ((o - mu) * lax.rsqrt(var + EPS)).reshape(B_, L, RET_WIDTH) * gn_gain.astype(f32)
    return jax.nn.silu(g.astype(f32)) * o, jnp.stack([s_f, s_b], axis=1)


def centred_conv(x, w, b):
    L = x.shape[1]
    xp = jnp.pad(x, ((0, 0), (CONV_LEFT, CONV_W - 1 - CONV_LEFT), (0, 0)))
    return sum(xp[:, j:j + L] * w[j] for j in range(CONV_W)) + b


def rglru_scan(x, w_a, b_a, w_x, b_x, lam, h0):
    B_, L, _ = x.shape
    xb = x.reshape(B_, L, LRU_BLOCKS, LRU_BLOCK_DIM)
    r = jax.nn.sigmoid(jnp.einsum('blnd,nde->blne', xb, w_a).reshape(B_, L, LRU_WIDTH) + b_a)
    i = jax.nn.sigmoid(jnp.einsum('blnd,nde->blne', xb, w_x).reshape(B_, L, LRU_WIDTH) + b_x)
    log_a = -LRU_C * r * jax.nn.softplus(-lam.astype(jnp.float32))
    a = jnp.exp(log_a)
    b = jnp.sqrt(-jnp.expm1(2.0 * log_a)) * (i * x)
    return linear_scan(a, b, h0)


def rglru_mixer(x, gate, p, s0):
    xc = centred_conv(x, p['lru_conv_w'], p['lru_conv_b']).astype(jnp.float32)
    outs, finals = [], []
    for d in range(2):
        xd = xc if d == 0 else jnp.flip(xc, 1)
        h = rglru_scan(xd, p['lru_wa'][d], p['lru_ba'][d], p['lru_wx'][d], p['lru_bx'][d], p['lru_lambda'][d], s0[:, d])
        finals.append(h[:, -1])
        outs.append(h if d == 0 else jnp.flip(h, 1))
    y = jax.nn.gelu(gate.astype(jnp.float32)) * (outs[0] + outs[1])
    return y, jnp.stack(finals, axis=1)


def axial_rope(x):
    L = x.shape[1]
    t = jnp.arange(L)
    row = (t // GRID_W).astype(jnp.float32)
    col = (t % GRID_W).astype(jnp.float32)
    n_pairs = HEAD_DIM // 4
    freqs = ROPE_BASE ** (-jnp.arange(n_pairs, dtype=jnp.float32) / n_pairs)
    ang = jnp.concatenate([row[:, None] * freqs, col[:, None] * freqs], axis=-1)
    cos = jnp.cos(ang)[None, :, None, :].astype(x.dtype)
    sin = jnp.sin(ang)[None, :, None, :].astype(x.dtype)
    x1, x2 = x[..., :HEAD_DIM // 2], x[..., HEAD_DIM // 2:]
    return jnp.concatenate([x1 * cos - x2 * sin, x2 * cos + x1 * sin], axis=-1)


def sink_attend(q, k, v, sink, mask):
    s = jnp.einsum('bqkgd,bskd->bkgqs', q, k).astype(jnp.float32) * HEAD_DIM ** -0.5
    if mask is not None:
        s = jnp.where(mask, s, NEG_INF)
    snk = sink.astype(jnp.float32)[None, :, :, None, None]
    m = jnp.maximum(jnp.max(s, axis=-1, keepdims=True), snk)
    pr = jnp.exp(s - m)
    denom = jnp.sum(pr, axis=-1, keepdims=True) + jnp.exp(snk - m)
    return jnp.einsum('bkgqs,bskd->bqkgd', (pr / denom).astype(v.dtype), v)


def context_attention(q, k, v, sink):
    B_, L = q.shape[:2]
    n = L // BLOCK
    qb = jnp.moveaxis(q.reshape(B_, n, BLOCK, ATT_KV_HEADS, ATT_GROUP, HEAD_DIM), 1, 0)
    snk = sink.reshape(ATT_KV_HEADS, ATT_GROUP)
    out = lax.map(lambda qi: sink_attend(qi, k, v, snk, None), qb)
    return jnp.moveaxis(out, 0, 1).reshape(B_, L, ATT_Q_HEADS * HEAD_DIM)


def latent_attention(q, k, v, ck, cv, sink):
    B_, L = q.shape[:2]
    n = L // BLOCK
    P = ck.shape[1]
    qg = q.reshape(B_, L, ATT_KV_HEADS, ATT_GROUP, HEAD_DIM)
    pad = ((0, 0), (BLOCK, BLOCK), (0, 0), (0, 0))
    kp, vp = jnp.pad(k, pad), jnp.pad(v, pad)
    snk = sink.reshape(ATT_KV_HEADS, ATT_GROUP)
    off = jnp.arange(BLOCK)[:, None] - jnp.arange(3 * BLOCK)[None, :] + BLOCK
    ctx_mask = jnp.ones((BLOCK, P), dtype=bool)
    def blk(i):
        start = i * BLOCK
        qi = lax.dynamic_slice_in_dim(qg, start, BLOCK, axis=1)
        ki = lax.dynamic_slice_in_dim(kp, start, 3 * BLOCK, axis=1)
        vi = lax.dynamic_slice_in_dim(vp, start, 3 * BLOCK, axis=1)
        kpos = start - BLOCK + jnp.arange(3 * BLOCK)
        band = (jnp.abs(off) <= WINDOW) & ((kpos >= 0) & (kpos < L))[None, :]
        mask = jnp.concatenate([band, ctx_mask], axis=1)
        return sink_attend(qi, jnp.concatenate([ki, ck], axis=1), jnp.concatenate([vi, cv], axis=1), snk, mask)
    out = lax.map(blk, jnp.arange(n))
    return jnp.moveaxis(out, 0, 1).reshape(B_, L, ATT_Q_HEADS * HEAD_DIM)


def s5_scan(u, a_re, a_im, log_dt, b_re, b_im, c_re, c_im, h0):
    f32 = jnp.float32
    B_, L, _ = u.shape
    lam = lax.complex(a_re.astype(f32), a_im.astype(f32))
    dt = jnp.exp(log_dt.astype(f32))[:, None]
    a_bar = jnp.exp(lam * dt)
    b_bar = ((a_bar - 1.0) / lam)[..., None] * lax.complex(b_re.astype(f32), b_im.astype(f32))
    ug = u.reshape(B_, L, SSM_GROUPS, SSM_GROUP).astype(jnp.complex64)
    bu = jnp.einsum('blgc,gpc->blgp', ug, b_bar)
    a_seq = jnp.broadcast_to(a_bar, (1, L, SSM_GROUPS, SSM_STATE))
    h = linear_scan(a_seq, bu, h0)
    y = jnp.einsum('blgp,gcp->blgc', h, lax.complex(c_re.astype(f32), c_im.astype(f32))).real
    return y.reshape(B_, L, SSM_WIDTH), h[:, -1]


def s5_mixer(u, p, s0):
    f32 = jnp.float32
    uf = u.astype(f32)
    outs, finals = [], []
    for d in range(2):
        ud = uf if d == 0 else jnp.flip(uf, 1)
        y, hl = s5_scan(ud, p['ssm_a_re'][d], p['ssm_a_im'][d], p['ssm_log_dt'][d], p['ssm_b_re'][d],
                        p['ssm_b_im'][d], p['ssm_c_re'][d], p['ssm_c_im'][d], s0[:, d])
        finals.append(hl)
        outs.append(y if d == 0 else jnp.flip(y, 1))
    y = jax.nn.gelu(outs[0] + outs[1] + p['ssm_d'].astype(f32) * uf)
    out = y * jax.nn.sigmoid(y @ p['ssm_w_glu'].astype(f32) + p['ssm_b_glu'].astype(f32))
    return out, jnp.stack(finals, axis=1)


def token_mixers(h, p, ctx_cache):
    f32 = jnp.float32
    B_, L, _ = h.shape
    is_ctx = ctx_cache is None
    offs = [int(o) for o in np.cumsum(IN_SIZES)[:-1]]
    rq, rk, rv, rg, lx, lgate, aq, ak, av, su, gates = jnp.split(h @ p['w_in'], offs, axis=-1)
    if is_ctx:
        s_ret0 = jnp.zeros((B_, 2, RET_HEADS, RET_DK, RET_DV), f32)
        s_lru0 = jnp.zeros((B_, 2, LRU_WIDTH), f32)
        s_ssm0 = jnp.zeros((B_, 2, SSM_GROUPS, SSM_STATE), jnp.complex64)
    else:
        ck, cv, s_ret0, s_lru0, s_re0, s_im0 = ctx_cache
        s_ret0 = s_ret0.astype(f32)
        s_lru0 = s_lru0.astype(f32)
        s_ssm0 = lax.complex(s_re0.astype(f32), s_im0.astype(f32))
    o_ret, s_ret = retention(rq, rk, rv, rg, p['ret_decay_logit'], p['ret_gn'], s_ret0)
    o_lru, s_lru = rglru_mixer(lx, lgate, p, s_lru0)
    q = rms_norm(aq.reshape(B_, L, ATT_Q_HEADS, HEAD_DIM), p['att_q_norm'])
    k = rms_norm(ak.reshape(B_, L, ATT_KV_HEADS, HEAD_DIM), p['att_k_norm'])
    v = av.reshape(B_, L, ATT_KV_HEADS, HEAD_DIM)
    if is_ctx:
        o_att = context_attention(q, k, v, p['att_sink'])
    else:
        o_att = latent_attention(axial_rope(q), axial_rope(k), v, ck.astype(k.dtype), cv.astype(v.dtype), p['att_sink'])
    o_ssm, s_ssm = s5_mixer(su, p, s_ssm0)
    g = gates.reshape(B_, L, N_BRANCH, D_MODEL)
    merged = 0.0
    for n_b, o_b in enumerate((o_ret, o_lru, o_att, o_ssm)):
        merged = merged + jax.nn.sigmoid(g[:, :, n_b]) * (o_b.astype(h.dtype) @ p['w_br'][n_b])
    out = merged @ p['w_out']
    if is_ctx:
        return out, (k, v, s_ret, s_lru, s_ssm.real, s_ssm.imag)
    return out, None


def trunk_layer(x, cond, p, ctx_cache):
    mod = jax.nn.silu(cond) @ p['w_mod'] + p['b_mod']
    sh1, sc1, g1, sh2, sc2, g2 = jnp.split(mod[:, None, :], 6, axis=-1)
    h = rms_norm(x, p['norm1']) * (1.0 + sc1) + sh1
    out, st = token_mixers(h, p, ctx_cache)
    x = x + g1 * out
    h = rms_norm(x, p['norm2']) * (1.0 + sc2) + sh2
    f = jnp.square(jax.nn.relu(h @ p['w_ff1'])) @ p['w_ff2']
    return x + g2 * f, st


def setup_inputs(seed: int = 0) -> dict:
    key = jax.random.key(seed)
    keys = iter(jax.random.split(key, 64))
    f32 = jnp.float32
    def nrm(shape, scale):
        return jax.random.normal(next(keys), shape, f32) * scale
    def unif(shape, lo, hi):
        return jax.random.uniform(next(keys), shape, f32, lo, hi)
    gam = 1.0 - 2.0 ** (-5.0 - jnp.arange(RET_HEADS, dtype=f32))
    a0 = unif((DEPTH, 2, LRU_WIDTH), 0.9, 0.999)
    return {
        'x_prompt': nrm((BATCH, SEQ, D_MODEL), 1.0),
        'x_sample': nrm((DEC_BATCH, DEC_SEQ, D_MODEL), 1.0),
        'cache_attn_k': nrm((DEC_BATCH, DEPTH, PAST_LEN, ATT_KV_HEADS, HEAD_DIM), 1.0),
        'cache_attn_v': nrm((DEC_BATCH, DEPTH, PAST_LEN, ATT_KV_HEADS, HEAD_DIM), 1.0),
        'state_ret': nrm((DEC_BATCH, DEPTH, 2, RET_HEADS, RET_DK, RET_DV), 1.0),
        'state_lru': nrm((DEC_BATCH, DEPTH, 2, LRU_WIDTH), 0.5),
        'state_ssm_re': nrm((DEC_BATCH, DEPTH, 2, SSM_GROUPS, SSM_STATE), 0.1),
        'state_ssm_im': nrm((DEC_BATCH, DEPTH, 2, SSM_GROUPS, SSM_STATE), 0.1),
        'c': nrm((DEC_BATCH, D_MODEL), 1.0),
        'c_ctx': nrm((D_MODEL,), 1.0),
        'w_mod': nrm((DEPTH, D_MODEL, 6 * D_MODEL), 0.5 * D_MODEL ** -0.5),
        'b_mod': nrm((DEPTH, 6 * D_MODEL), 0.02),
        'norm1': 1.0 + nrm((DEPTH, D_MODEL), 0.02),
        'w_in': nrm((DEPTH, D_MODEL, IN_TOTAL), D_MODEL ** -0.5),
        'ret_decay_logit': jnp.log(gam / (1.0 - gam)) + nrm((DEPTH, 2, RET_HEADS), 0.1),
        'ret_gn': 1.0 + nrm((DEPTH, RET_WIDTH), 0.02),
        'lru_conv_w': nrm((DEPTH, CONV_W, LRU_WIDTH), CONV_W ** -0.5),
        'lru_conv_b': nrm((DEPTH, LRU_WIDTH), 0.02),
        'lru_wa': nrm((DEPTH, 2, LRU_BLOCKS, LRU_BLOCK_DIM, LRU_BLOCK_DIM), LRU_BLOCK_DIM ** -0.5),
        'lru_ba': nrm((DEPTH, 2, LRU_WIDTH), 0.02),
        'lru_wx': nrm((DEPTH, 2, LRU_BLOCKS, LRU_BLOCK_DIM, LRU_BLOCK_DIM), LRU_BLOCK_DIM ** -0.5),
        'lru_bx': nrm((DEPTH, 2, LRU_WIDTH), 0.02),
        'lru_lambda': jnp.log(a0 / (1.0 - a0)),
        'att_q_norm': 1.0 + nrm((DEPTH, HEAD_DIM), 0.02),
        'att_k_norm': 1.0 + nrm((DEPTH, HEAD_DIM), 0.02),
        'att_sink': nrm((DEPTH, ATT_Q_HEADS), 0.5),
        'ssm_a_re': -0.5 + nrm((DEPTH, 2, SSM_GROUPS, SSM_STATE), 0.01),
        'ssm_a_im': math.pi * jnp.arange(SSM_STATE, dtype=f32) + nrm((DEPTH, 2, SSM_GROUPS, SSM_STATE), 0.01),
        'ssm_log_dt': unif((DEPTH, 2, SSM_GROUPS), math.log(1e-3), math.log(1e-1)),
        'ssm_b_re': nrm((DEPTH, 2, SSM_GROUPS, SSM_STATE, SSM_GROUP), (2 * SSM_GROUP) ** -0.5),
        'ssm_b_im': nrm((DEPTH, 2, SSM_GROUPS, SSM_STATE, SSM_GROUP), (2 * SSM_GROUP) ** -0.5),
        'ssm_c_re': nrm((DEPTH, 2, SSM_GROUPS, SSM_GROUP, SSM_STATE), SSM_STATE ** -0.5),
        'ssm_c_im': nrm((DEPTH, 2, SSM_GROUPS, SSM_GROUP, SSM_STATE), SSM_STATE ** -0.5),
        'ssm_d': nrm((DEPTH, SSM_WIDTH), 1.0),
        'ssm_w_glu': nrm((DEPTH, SSM_WIDTH, SSM_WIDTH), SSM_WIDTH ** -0.5),
        'ssm_b_glu': nrm((DEPTH, SSM_WIDTH), 0.02),
        'w_br': nrm((DEPTH, N_BRANCH, BRANCH_WIDTH, D_MODEL), BRANCH_WIDTH ** -0.5),
        'w_out': nrm((DEPTH, D_MODEL, D_MODEL), D_MODEL ** -0.5),
        'norm2': 1.0 + nrm((DEPTH, D_MODEL), 0.02),
        'w_ff1': nrm((DEPTH, D_MODEL, D_FF), D_MODEL ** -0.5),
        'w_ff2': nrm((DEPTH, D_FF, D_MODEL), D_FF ** -0.5),
    }


def reference(x_prompt, x_sample, cache_attn_k, cache_attn_v, state_ret, state_lru, state_ssm_re, state_ssm_im,
              c, c_ctx, w_mod, b_mod, norm1, w_in, ret_decay_logit, ret_gn, lru_conv_w, lru_conv_b,
              lru_wa, lru_ba, lru_wx, lru_bx, lru_lambda, att_q_norm, att_k_norm, att_sink,
              ssm_a_re, ssm_a_im, ssm_log_dt, ssm_b_re, ssm_b_im, ssm_c_re, ssm_c_im, ssm_d,
              ssm_w_glu, ssm_b_glu, w_br, w_out, norm2, w_ff1, w_ff2):
    y_prompt, y_sample = x_prompt, x_sample
    cond_ctx = c_ctx[None, :]
    ks, vs, rets, lrus, sres, sims = [], [], [], [], [], []
    for l in range(DEPTH):
        p = {
            'w_mod': w_mod[l], 'b_mod': b_mod[l], 'norm1': norm1[l], 'w_in': w_in[l],
            'ret_decay_logit': ret_decay_logit[l], 'ret_gn': ret_gn[l],
            'lru_conv_w': lru_conv_w[l], 'lru_conv_b': lru_conv_b[l], 'lru_wa': lru_wa[l], 'lru_ba': lru_ba[l],
            'lru_wx': lru_wx[l], 'lru_bx': lru_bx[l], 'lru_lambda': lru_lambda[l],
            'att_q_norm': att_q_norm[l], 'att_k_norm': att_k_norm[l], 'att_sink': att_sink[l],
            'ssm_a_re': ssm_a_re[l], 'ssm_a_im': ssm_a_im[l], 'ssm_log_dt': ssm_log_dt[l],
            'ssm_b_re': ssm_b_re[l], 'ssm_b_im': ssm_b_im[l], 'ssm_c_re': ssm_c_re[l], 'ssm_c_im': ssm_c_im[l],
            'ssm_d': ssm_d[l], 'ssm_w_glu': ssm_w_glu[l], 'ssm_b_glu': ssm_b_glu[l],
            'w_br': w_br[l], 'w_out': w_out[l], 'norm2': norm2[l], 'w_ff1': w_ff1[l], 'w_ff2': w_ff2[l],
        }
        y_prompt, st = trunk_layer(y_prompt, cond_ctx, p, None)
        ks.append(st[0]); vs.append(st[1]); rets.append(st[2]); lrus.append(st[3]); sres.append(st[4]); sims.append(st[5])
        cache_l = (cache_attn_k[:, l], cache_attn_v[:, l], state_ret[:, l], state_lru[:, l],
                   state_ssm_re[:, l], state_ssm_im[:, l])
        y_sample, _ = trunk_layer(y_sample, c, p, cache_l)
    new_attn_k = jnp.stack(ks, axis=1)
    new_attn_v = jnp.stack(vs, axis=1)
    new_ret = jnp.stack(rets, axis=1)
    new_lru = jnp.stack(lrus, axis=1)
    new_ssm_re = jnp.stack(sres, axis=1)
    new_ssm_im = jnp.stack(sims, axis=1)
    return (y_prompt, y_sample, new_attn_k, new_attn_v, new_ret, new_lru, new_ssm_re, new_ssm_im)
```

```python
import functools
import math

import jax
import jax.numpy as jnp
from jax import lax
from jax.experimental import pallas as pl
from jax.experimental.pallas import tpu as pltpu

f32 = jnp.float32
bf16 = jnp.bfloat16

D_MODEL = 2048
DEPTH = 4
NB_P, L_P = 32, 256
NB_S, L_S = 8, 2048
PAST_LEN = 256
N_P = NB_P * L_P
N_S = NB_S * L_S
N_TOK = N_P + N_S
GRID_W = 64
CHUNK = 128
EPS = 1e-6
NEG_INF = -1e30
RET_HEADS = 4
RET_DK = 128
LRU_WIDTH = 512
LRU_BLOCKS = 4
LRU_C = 8.0
CONV_W = 4
ATT_Q_HEADS = 4
ATT_KV_HEADS = 2
HEAD_DIM = 128
WINDOW = 128
ROPE_BASE = 10000.0
SSM_WIDTH = 512
SSM_GROUP = 16
SSM_GROUPS = 32
SSM_STATE = 64
SSM_N = SSM_GROUPS * SSM_STATE
N_BRANCH = 4
BRANCH_WIDTH = 512
D_FF = 4 * D_MODEL
RQ, RK, RV, RG, LX, LG, AQ, AK, AV, SU, ZW = 0, 512, 1024, 1536, 2048, 2560, 3072, 3584, 3840, 4096, 4608
IN_TOTAL = ZW + N_BRANCH * D_MODEL
SUB = 8
COND_ROWS = 16
CTX_ROW = NB_S
VMEM_LIMIT = 56 * 1024 * 1024

TM = 1024
LRU_TC = 256
SSM_TC = 64


def _cp(n_axes, vmem=VMEM_LIMIT):
    return pltpu.CompilerParams(dimension_semantics=("arbitrary",) * n_axes, vmem_limit_bytes=vmem)


def _mod_row(i):
    n_ctx = N_P // TM
    return jnp.where(i < n_ctx, CTX_ROW, (i - n_ctx) // (L_S // TM))


def _modnorm(x, nw, shift, scale):
    y = x * lax.rsqrt(jnp.mean(x * x, axis=-1, keepdims=True) + EPS) * nw
    return y * (1.0 + scale) + shift


def _mod_kernel(c_ref, w_ref, b_ref, o_ref):
    c = c_ref[...]
    s = (c * jax.nn.sigmoid(c)).astype(bf16)
    o_ref[...] = jnp.dot(s, w_ref[...].astype(bf16), preferred_element_type=f32) + b_ref[...]


def _modulation(cond, w_mod, b_mod):
    tn = 1024
    n6 = 6 * D_MODEL
    return pl.pallas_call(
        _mod_kernel,
        grid=(DEPTH, n6 // tn),
        in_specs=[
            pl.BlockSpec((COND_ROWS, D_MODEL), lambda l, j: (0, 0)),
            pl.BlockSpec((None, D_MODEL, tn), lambda l, j: (l, 0, j)),
            pl.BlockSpec((None, 1, tn), lambda l, j: (l, 0, j)),
        ],
        out_specs=pl.BlockSpec((None, COND_ROWS, tn), lambda l, j: (l, 0, j)),
        out_shape=jax.ShapeDtypeStruct((DEPTH, COND_ROWS, n6), f32),
        compiler_params=_cp(2),
        name="modulation",
    )(cond, w_mod, b_mod.reshape(DEPTH, 1, n6))


def _proj_kernel(x_ref, mod_ref, nw_ref, w_ref, o_ref, h_scr):
    @pl.when(pl.program_id(1) == 0)
    def _():
        h_scr[...] = _modnorm(x_ref[...], nw_ref[...], mod_ref[0:1, :], mod_ref[1:2, :]).astype(bf16)

    o_ref[...] = jnp.dot(h_scr[...], w_ref[...], preferred_element_type=f32)


def _proj_in(x, mod, nw, w_in, layer):
    tn = 512
    return pl.pallas_call(
        _proj_kernel,
        grid=(N_TOK // TM, ZW // tn),
        in_specs=[
            pl.BlockSpec((TM, D_MODEL), lambda i, j: (i, 0)),
            pl.BlockSpec((None, None, 6, D_MODEL), lambda i, j: (layer, _mod_row(i), 0, 0)),
            pl.BlockSpec((None, 1, D_MODEL), lambda i, j: (layer, 0, 0)),
            pl.BlockSpec((None, D_MODEL, tn), lambda i, j: (layer, 0, j)),
        ],
        out_specs=pl.BlockSpec((TM, tn), lambda i, j: (i, j)),
        out_shape=jax.ShapeDtypeStruct((N_TOK, ZW), f32),
        scratch_shapes=[pltpu.VMEM((TM, D_MODEL), bf16)],
        compiler_params=_cp(2),
        name="proj_in",
    )(x, mod, nw, w_in)


def _ret_kernel(*refs, nc, has_s0, emit_state):
    q_ref, k_ref, v_ref, g_ref, lg_ref, gn_ref = refs[:6]
    pos = 6
    s0_ref = None
    if has_s0:
        s0_ref = refs[pos]
        pos += 1
    o_ref = refs[pos]
    pos += 1
    sout_ref = None
    if emit_state:
        sout_ref = refs[pos]
        pos += 1
    sf_scr = refs[pos]

    C = CHUNK
    ii = lax.broadcasted_iota(jnp.int32, (C, C), 0).astype(f32)
    jj = lax.broadcasted_iota(jnp.int32, (C, C), 1).astype(f32)
    lgf = lg_ref[0:1, :]
    lgb = lg_ref[1:2, :]
    rel = ii - jj
    dec = (jnp.where(rel >= 0, jnp.exp(jnp.maximum(rel, 0.0) * lgf), 0.0)
           + jnp.where(rel <= 0, jnp.exp(jnp.maximum(-rel, 0.0) * lgb), 0.0))
    w_in_f = jnp.exp((ii + 1.0) * lgf)
    w_end_f = jnp.exp((C - 1.0 - ii) * lgf)
    w_in_b = jnp.exp((C - ii) * lgb)
    w_end_b = jnp.exp(ii * lgb)
    cd_f = jnp.exp(C * lgf)
    cd_b = jnp.exp(C * lgb)
    gn = gn_ref[...]
    tn_dims = (((0,), (0,)), ((), ()))
    nt_dims = (((1,), (1,)), ((), ()))

    if has_s0:
        s_f0 = s0_ref[0]
        s_b0 = s0_ref[1]
    else:
        s_f0 = jnp.zeros((C, C), f32)
        s_b0 = jnp.zeros((C, C), f32)

    def fwd_body(n, s):
        r = pl.multiple_of(n * C, C)
        sf_scr[n] = s
        kk = (k_ref[pl.ds(r, C), :] * w_end_f).astype(bf16)
        vv = v_ref[pl.ds(r, C), :].astype(bf16)
        return cd_f * s + lax.dot_general(kk, vv, tn_dims, preferred_element_type=f32)

    s_f = lax.fori_loop(0, nc, fwd_body, s_f0)

    def bwd_body(idx, s):
        n = nc - 1 - idx
        r = pl.multiple_of(n * C, C)
        q = q_ref[pl.ds(r, C), :] * (RET_DK ** -0.5)
        k = k_ref[pl.ds(r, C), :]
        vb = v_ref[pl.ds(r, C), :].astype(bf16)
        sc = lax.dot_general(q.astype(bf16), k.astype(bf16), nt_dims, preferred_element_type=f32) * dec
        o = jnp.dot(sc.astype(bf16), vb, preferred_element_type=f32)
        o += jnp.dot((q * w_in_f).astype(bf16), sf_scr[n].astype(bf16), preferred_element_type=f32)
        o += jnp.dot((q * w_in_b).astype(bf16), s.astype(bf16), preferred_element_type=f32)
        mu = jnp.mean(o, axis=-1, keepdims=True)
        d = o - mu
        var = jnp.mean(d * d, axis=-1, keepdims=True)
        on = d * lax.rsqrt(var + EPS) * gn
        g = g_ref[pl.ds(r, C), :]
        o_ref[pl.ds(r, C), :] = (g * jax.nn.sigmoid(g) * on).astype(bf16)
        kk = (k * w_end_b).astype(bf16)
        return cd_b * s + lax.dot_general(kk, vb, tn_dims, preferred_element_type=f32)

    s_b = lax.fori_loop(0, nc, bwd_body, s_b0)
    if emit_state:
        sout_ref[0] = s_f
        sout_ref[1] = s_b


def _retention(z, lg, gn, s0, *, nb, seq, row0, emit_state):
    rb0 = row0 // seq
    cq, ck, cv, cg = RQ // 128, RK // 128, RV // 128, RG // 128
    in_specs = [
        pl.BlockSpec((seq, 128), lambda b, h: (rb0 + b, cq + h)),
        pl.BlockSpec((seq, 128), lambda b, h: (rb0 + b, ck + h)),
        pl.BlockSpec((seq, 128), lambda b, h: (rb0 + b, cv + h)),
        pl.BlockSpec((seq, 128), lambda b, h: (rb0 + b, cg + h)),
        pl.BlockSpec((None, 2, 128), lambda b, h: (h, 0, 0)),
        pl.BlockSpec((None, 1, 128), lambda b, h: (h, 0, 0)),
    ]
    args = [z, z, z, z, lg, gn]
    st_spec = pl.BlockSpec((None, 2, None, 128, 128), lambda b, h: (b, 0, h, 0, 0))
    if s0 is not None:
        in_specs.append(st_spec)
        args.append(s0)
    out_specs = [pl.BlockSpec((seq, 128), lambda b, h: (b, h))]
    out_shape = [jax.ShapeDtypeStruct((nb * seq, RET_HEADS * 128), bf16)]
    if emit_state:
        out_specs.append(st_spec)
        out_shape.append(jax.ShapeDtypeStruct((nb, 2, RET_HEADS, 128, 128), f32))
    nc = seq // CHUNK
    res = pl.pallas_call(
        functools.partial(_ret_kernel, nc=nc, has_s0=s0 is not None, emit_state=emit_state),
        grid=(nb, RET_HEADS),
        in_specs=in_specs,
        out_specs=out_specs,
        out_shape=out_shape,
        scratch_shapes=[pltpu.VMEM((nc, 128, 128), f32)],
        compiler_params=_cp(2),
        name="retention",
    )(*args)
    return res if emit_state else (res[0], None)


def _lru_kernel(*refs, tc, nch, direction, has_s0):
    xprev_ref, xcur_ref, xnext_ref, cw_ref, cb_ref, wa_ref, ba_ref, wx_ref, bx_ref, sp_ref = refs[:10]
    pos = 10
    s0_ref = None
    if has_s0:
        s0_ref = refs[pos]
        pos += 1
    if direction == 1:
        gate_ref, hf_ref = refs[pos], refs[pos + 1]
        pos += 2
    out_ref, fin_ref = refs[pos], refs[pos + 1]
    xe_scr, a_scr, b_scr, h_scr = refs[pos + 2: pos + 6]

    R = tc * SUB
    j = pl.program_id(1)
    c = j if direction == 0 else nch - 1 - j

    @pl.when(j == 0)
    def _():
        h_scr[...] = s0_ref[...] if has_s0 else jnp.zeros((SUB, LRU_WIDTH), f32)

    xe_scr[0:2 * SUB, :] = jnp.where(c > 0, xprev_ref[...], 0.0)
    xe_scr[2 * SUB:2 * SUB + R, :] = xcur_ref[...]
    xe_scr[2 * SUB + R:3 * SUB + R, :] = jnp.where(c < nch - 1, xnext_ref[...], 0.0)
    xc = cb_ref[...] + xe_scr[0:R, :] * cw_ref[0:1, :]
    for t in range(1, CONV_W):
        xc = xc + xe_scr[t * SUB:t * SUB + R, :] * cw_ref[t:t + 1, :]

    bd = LRU_WIDTH // LRU_BLOCKS
    for n in range(LRU_BLOCKS):
        sl = slice(n * bd, (n + 1) * bd)
        xs = xc[:, sl]
        xb = xs.astype(bf16)
        r = jax.nn.sigmoid(jnp.dot(xb, wa_ref[n], preferred_element_type=f32) + ba_ref[:, sl])
        i = jax.nn.sigmoid(jnp.dot(xb, wx_ref[n], preferred_element_type=f32) + bx_ref[:, sl])
        log_a = -LRU_C * r * sp_ref[:, sl]
        a = jnp.exp(log_a)
        a_scr[:, sl] = a
        b_scr[:, sl] = jnp.sqrt(-jnp.tanh(log_a) * (a * a + 1.0)) * (i * xs)

    def step(s, h):
        t = s if direction == 0 else tc - 1 - s
        r0 = pl.multiple_of(t * SUB, SUB)
        h = a_scr[pl.ds(r0, SUB), :] * h + b_scr[pl.ds(r0, SUB), :]
        b_scr[pl.ds(r0, SUB), :] = h
        return h

    h = lax.fori_loop(0, tc, step, h_scr[...], unroll=8)
    h_scr[...] = h
    fin_ref[...] = h
    if direction == 0:
        out_ref[...] = b_scr[...]
    else:
        out_ref[...] = (jax.nn.gelu(gate_ref[...]) * (hf_ref[...] + b_scr[...])).astype(bf16)


def _lru_dir(x_tm, gate_tm, hf_tm, s0_tm, prm, *, seq, direction):
    nbg = x_tm.shape[0]
    tc = min(LRU_TC, seq)
    nch = seq // tc
    R = tc * SUB

    def cidx(j):
        return j if direction == 0 else nch - 1 - j

    W = LRU_WIDTH
    in_specs = [
        pl.BlockSpec((None, 2 * SUB, W), lambda g, j: (g, jnp.maximum(cidx(j) * (tc // 2) - 1, 0), 0)),
        pl.BlockSpec((None, R, W), lambda g, j: (g, cidx(j), 0)),
        pl.BlockSpec((None, SUB, W), lambda g, j: (g, jnp.minimum((cidx(j) + 1) * tc, seq - 1), 0)),
        pl.BlockSpec((CONV_W, W), lambda g, j: (0, 0)),
        pl.BlockSpec((1, W), lambda g, j: (0, 0)),
        pl.BlockSpec((LRU_BLOCKS, 128, 128), lambda g, j: (0, 0, 0)),
        pl.BlockSpec((1, W), lambda g, j: (0, 0)),
        pl.BlockSpec((LRU_BLOCKS, 128, 128), lambda g, j: (0, 0, 0)),
        pl.BlockSpec((1, W), lambda g, j: (0, 0)),
        pl.BlockSpec((1, W), lambda g, j: (0, 0)),
    ]
    args = [x_tm, x_tm, x_tm, prm["cw"], prm["cb"], prm["wa"][direction], prm["ba"][direction],
            prm["wx"][direction], prm["bx"][direction], prm["sp"][direction]]
    if s0_tm is not None:
        in_specs.append(pl.BlockSpec((None, SUB, W), lambda g, j: (g, 0, 0)))
        args.append(s0_tm)
    chunk_spec = pl.BlockSpec((None, R, W), lambda g, j: (g, cidx(j), 0))
    if direction == 1:
        in_specs += [chunk_spec, chunk_spec]
        args += [gate_tm, hf_tm]
    out_dtype = f32 if direction == 0 else bf16
    return pl.pallas_call(
        functools.partial(_lru_kernel, tc=tc, nch=nch, direction=direction, has_s0=s0_tm is not None),
        grid=(nbg, nch),
        in_specs=in_specs,
        out_specs=[chunk_spec, pl.BlockSpec((None, SUB, W), lambda g, j: (g, 0, 0))],
        out_shape=[jax.ShapeDtypeStruct((nbg, seq * SUB, W), out_dtype),
                   jax.ShapeDtypeStruct((nbg, SUB, W), f32)],
        scratch_shapes=[pltpu.VMEM((R + 3 * SUB, W), f32), pltpu.VMEM((R, W), f32),
                        pltpu.VMEM((R, W), f32), pltpu.VMEM((SUB, W), f32)],
        compiler_params=_cp(2),
        name="rglru_dir%d" % direction,
    )(*args)


def _s5_kernel(*refs, tc, nch, direction, has_s0):
    u_ref, bd_ref, a_ref, cd_ref = refs[:4]
    pos = 4
    s0_ref = None
    if has_s0:
        s0_ref = refs[pos]
        pos += 1
    if direction == 1:
        yf_ref, dv_ref, wg_ref, bg_ref = refs[pos:pos + 4]
        pos += 4
    out_ref, fin_ref = refs[pos], refs[pos + 1]
    hs_scr, h_scr = refs[pos + 2], refs[pos + 3]

    j = pl.program_id(1)

    @pl.when(j == 0)
    def _():
        h_scr[...] = s0_ref[...] if has_s0 else jnp.zeros((SUB, 2 * SSM_N), f32)

    u = u_ref[...]
    hs_scr[...] = jnp.dot(u.astype(bf16), bd_ref[...], preferred_element_type=f32)

    cbw = 512
    for cb in range(SSM_N // cbw):
        cre = slice(cb * cbw, (cb + 1) * cbw)
        cim = slice(SSM_N + cb * cbw, SSM_N + (cb + 1) * cbw)
        a_re = jnp.broadcast_to(a_ref[0:1, cre], (SUB, cbw))
        a_im = jnp.broadcast_to(a_ref[1:2, cre], (SUB, cbw))

        def step(s, carry):
            hr, hi = carry
            t = s if direction == 0 else tc - 1 - s
            r0 = pl.multiple_of(t * SUB, SUB)
            nr = a_re * hr - a_im * hi + hs_scr[pl.ds(r0, SUB), cre]
            ni = a_re * hi + a_im * hr + hs_scr[pl.ds(r0, SUB), cim]
            hs_scr[pl.ds(r0, SUB), cre] = nr
            hs_scr[pl.ds(r0, SUB), cim] = ni
            return nr, ni

        hr, hi = lax.fori_loop(0, tc, step, (h_scr[:, cre], h_scr[:, cim]), unroll=4)
        h_scr[:, cre] = hr
        h_scr[:, cim] = hi

    fin_ref[...] = h_scr[...]
    y = jnp.dot(hs_scr[...].astype(bf16), cd_ref[...], preferred_element_type=f32)
    if direction == 0:
        out_ref[...] = y
    else:
        yy = jax.nn.gelu(yf_ref[...] + y + dv_ref[...] * u)
        gl = jnp.dot(yy.astype(bf16), wg_ref[...], preferred_element_type=f32) + bg_ref[...]
        out_ref[...] = (yy * jax.nn.sigmoid(gl)).astype(bf16)


def _s5_dir(u_tm, yf_tm, s0_tm, prm, *, seq, direction):
    nbg = u_tm.shape[0]
    tc = min(SSM_TC, seq)
    nch = seq // tc
    R = tc * SUB
    W = SSM_WIDTH

    def cidx(j):
        return j if direction == 0 else nch - 1 - j

    chunk_spec = pl.BlockSpec((None, R, W), lambda g, j: (g, cidx(j), 0))
    st_spec = pl.BlockSpec((None, SUB, 2 * SSM_N), lambda g, j: (g, 0, 0))
    in_specs = [
        chunk_spec,
        pl.BlockSpec((W, 2 * SSM_N), lambda g, j: (0, 0)),
        pl.BlockSpec((2, SSM_N), lambda g, j: (0, 0)),
        pl.BlockSpec((2 * SSM_N, W), lambda g, j: (0, 0)),
    ]
    args = [u_tm, prm["bd"][direction], prm["a"][direction], prm["cd"][direction]]
    if s0_tm is not None:
        in_specs.append(st_spec)
        args.append(s0_tm)
    if direction == 1:
        in_specs += [chunk_spec, pl.BlockSpec((1, W), lambda g, j: (0, 0)),
                     pl.BlockSpec((W, W), lambda g, j: (0, 0)), pl.BlockSpec((1, W), lambda g, j: (0, 0))]
        args += [yf_tm, prm["d"], prm["wglu"], prm["bglu"]]
    out_dtype = f32 if direction == 0 else bf16
    return pl.pallas_call(
        functools.partial(_s5_kernel, tc=tc, nch=nch, direction=direction, has_s0=s0_tm is not None),
        grid=(nbg, nch),
        in_specs=in_specs,
        out_specs=[chunk_spec, st_spec],
        out_shape=[jax.ShapeDtypeStruct((nbg, seq * SUB, W), out_dtype),
                   jax.ShapeDtypeStruct((nbg, SUB, 2 * SSM_N), f32)],
        scratch_shapes=[pltpu.VMEM((R, 2 * SSM_N), f32), pltpu.VMEM((SUB, 2 * SSM_N), f32)],
        compiler_params=_cp(2),
        name="s5_dir%d" % direction,
    )(*args)


def _rope(x, cos, sin_signed):
    return x * cos + pltpu.roll(x, HEAD_DIM // 2, 1) * sin_signed


def _unit_rms(x, w):
    return x * lax.rsqrt(jnp.mean(x * x, axis=-1, keepdims=True) + EPS) * w


def _attn_kernel(*refs, seq, latent):
    q_ref, k_ref, v_ref, qw_ref, kw_ref, sink_ref = refs[:6]
    pos = 6
    if latent:
        cos_ref, sin_ref, ck_ref, cv_ref = refs[pos:pos + 4]
        pos += 4
    o_ref = refs[pos]
    pos += 1
    if not latent:
        kout_ref, vout_ref = refs[pos], refs[pos + 1]
        pos += 2
    kb_scr, vb_scr = refs[pos], refs[pos + 1]
    pos += 2
    if latent:
        ckb_scr, cvb_scr = refs[pos], refs[pos + 1]

    HD = HEAD_DIM
    qb = pl.program_id(1)

    @pl.when(qb == 0)
    def _():
        for hk in range(ATT_KV_HEADS):
            sl = slice(hk * HD, (hk + 1) * HD)
            kn = _unit_rms(k_ref[:, sl], kw_ref[...])
            if latent:
                kn = _rope(kn, cos_ref[...], sin_ref[...])
            else:
                kout_ref[:, sl] = kn
            kb_scr[:, sl] = kn.astype(bf16)
        v = v_ref[...]
        vb_scr[...] = v.astype(bf16)
        if latent:
            ckb_scr[...] = ck_ref[...].astype(bf16)
            cvb_scr[...] = cv_ref[...].astype(bf16)
        else:
            vout_ref[...] = v

    q0 = pl.multiple_of(qb * CHUNK, CHUNK)
    if latent:
        win = 3 * CHUNK
        start = pl.multiple_of(jnp.clip(q0 - CHUNK, 0, seq - win), CHUNK)
        qpos = q0 + (lax.broadcasted_iota(jnp.int32, (2 * CHUNK, win), 0) & (CHUNK - 1))
        kpos = start + lax.broadcasted_iota(jnp.int32, (2 * CHUNK, win), 1)
        band = jnp.abs(qpos - kpos) <= WINDOW
        cos_q = cos_ref[pl.ds(q0, CHUNK), :]
        sin_q = sin_ref[pl.ds(q0, CHUNK), :]
    else:
        win = seq
        start = 0
    scale = HD ** -0.5
    nt_dims = (((1,), (1,)), ((), ()))
    row = lax.broadcasted_iota(jnp.int32, (2 * CHUNK, 1), 0)
    for hk in range(ATT_KV_HEADS):
        sl = slice(hk * HD, (hk + 1) * HD)
        qs = []
        for g in range(2):
            hq = hk * 2 + g
            qn = _unit_rms(q_ref[:, hq * HD:(hq + 1) * HD], qw_ref[...])
            if latent:
                qn = _rope(qn, cos_q, sin_q)
            qs.append(qn.astype(bf16))
        qg = jnp.concatenate(qs, axis=0)
        snk = jnp.where(row < CHUNK, sink_ref[hk * 2], sink_ref[hk * 2 + 1])
        s1 = lax.dot_general(qg, kb_scr[pl.ds(start, win), sl], nt_dims, preferred_element_type=f32) * scale
        if latent:
            s1 = jnp.where(band, s1, NEG_INF)
        m = jnp.maximum(jnp.max(s1, axis=-1, keepdims=True), snk)
        if latent:
            s2 = lax.dot_general(qg, ckb_scr[:, sl], nt_dims, preferred_element_type=f32) * scale
            m = jnp.maximum(m, jnp.max(s2, axis=-1, keepdims=True))
            p2 = jnp.exp(s2 - m)
        p1 = jnp.exp(s1 - m)
        den = jnp.sum(p1, axis=-1, keepdims=True) + jnp.exp(snk - m)
        if latent:
            den = den + jnp.sum(p2, axis=-1, keepdims=True)
        o = jnp.dot((p1 / den).astype(bf16), vb_scr[pl.ds(start, win), sl], preferred_element_type=f32)
        if latent:
            o += jnp.dot((p2 / den).astype(bf16), cvb_scr[:, sl], preferred_element_type=f32)
        for g in range(2):
            hq = hk * 2 + g
            o_ref[:, hq * HD:(hq + 1) * HD] = o[g * CHUNK:(g + 1) * CHUNK].astype(bf16)


def _attention(z, qw, kw, sink, rope, cache, *, nb, seq, row0, latent):
    nqb = seq // CHUNK
    rq0 = row0 // CHUNK
    rb0 = row0 // seq
    kvw = ATT_KV_HEADS * HEAD_DIM
    in_specs = [
        pl.BlockSpec((CHUNK, 512), lambda b, i: (rq0 + b * nqb + i, AQ // 512)),
        pl.BlockSpec((seq, kvw), lambda b, i: (rb0 + b, AK // kvw)),
        pl.BlockSpec((seq, kvw), lambda b, i: (rb0 + b, AV // kvw)),
        pl.BlockSpec((1, HEAD_DIM), lambda b, i: (0, 0)),
        pl.BlockSpec((1, HEAD_DIM), lambda b, i: (0, 0)),
        pl.BlockSpec(memory_space=pltpu.SMEM),
    ]
    args = [z, z, z, qw, kw, sink]
    scratch = [pltpu.VMEM((seq, kvw), bf16), pltpu.VMEM((seq, kvw), bf16)]
    out_specs = [pl.BlockSpec((CHUNK, 512), lambda b, i: (b * nqb + i, 0))]
    out_shape = [jax.ShapeDtypeStruct((nb * seq, ATT_Q_HEADS * HEAD_DIM), bf16)]
    if latent:
        in_specs += [pl.BlockSpec((seq, HEAD_DIM), lambda b, i: (0, 0)),
                     pl.BlockSpec((seq, HEAD_DIM), lambda b, i: (0, 0)),
                     pl.BlockSpec((None, PAST_LEN, kvw), lambda b, i: (b, 0, 0)),
                     pl.BlockSpec((None, PAST_LEN, kvw), lambda b, i: (b, 0, 0))]
        args += [rope[0], rope[1], cache[0], cache[1]]
        scratch += [pltpu.VMEM((PAST_LEN, kvw), bf16), pltpu.VMEM((PAST_LEN, kvw), bf16)]
    else:
        kv_out = pl.BlockSpec((seq, kvw), lambda b, i: (b, 0))
        out_specs += [kv_out, kv_out]
        out_shape += [jax.ShapeDtypeStruct((nb * seq, kvw), f32)] * 2
    return pl.pallas_call(
        functools.partial(_attn_kernel, seq=seq, latent=latent),
        grid=(nb, nqb),
        in_specs=in_specs,
        out_specs=out_specs,
        out_shape=out_shape,
        scratch_shapes=scratch,
        compiler_params=_cp(2),
        name="attention_lat" if latent else "attention_ctx",
    )(*args)


def _merge_kernel(x_ref, mod_ref, nw_ref, o0, o1, o2, o3, g0, g1, g2, g3, wbr_ref, out_ref, h_scr):
    @pl.when(pl.program_id(1) == 0)
    def _():
        h_scr[...] = _modnorm(x_ref[...], nw_ref[...], mod_ref[0:1, :], mod_ref[1:2, :]).astype(bf16)

    h = h_scr[...]
    acc = None
    for b, (o_ref, g_ref) in enumerate(((o0, g0), (o1, g1), (o2, g2), (o3, g3))):
        gate = jax.nn.sigmoid(jnp.dot(h, g_ref[...], preferred_element_type=f32))
        t = gate * jnp.dot(o_ref[...], wbr_ref[b], preferred_element_type=f32)
        acc = t if acc is None else acc + t
    out_ref[...] = acc.astype(bf16)


def _merge(x, mod, nw, branches, w_in, w_br, layer):
    tn = 256
    g0 = ZW // tn
    gb = D_MODEL // tn
    o_spec = pl.BlockSpec((TM, BRANCH_WIDTH), lambda i, j: (i, 0))

    def gate_spec(b):
        return pl.BlockSpec((None, D_MODEL, tn), lambda i, j: (layer, 0, g0 + b * gb + j))

    return pl.pallas_call(
        _merge_kernel,
        grid=(N_TOK // TM, D_MODEL // tn),
        in_specs=[
            pl.BlockSpec((TM, D_MODEL), lambda i, j: (i, 0)),
            pl.BlockSpec((None, None, 6, D_MODEL), lambda i, j: (layer, _mod_row(i), 0, 0)),
            pl.BlockSpec((None, 1, D_MODEL), lambda i, j: (layer, 0, 0)),
            o_spec, o_spec, o_spec, o_spec,
            gate_spec(0), gate_spec(1), gate_spec(2), gate_spec(3),
            pl.BlockSpec((None, N_BRANCH, BRANCH_WIDTH, tn), lambda i, j: (layer, 0, 0, j)),
        ],
        out_specs=pl.BlockSpec((TM, tn), lambda i, j: (i, j)),
        out_shape=jax.ShapeDtypeStruct((N_TOK, D_MODEL), bf16),
        scratch_shapes=[pltpu.VMEM((TM, D_MODEL), bf16)],
        compiler_params=_cp(2),
        name="merge",
    )(x, mod, nw, *branches, w_in, w_in, w_in, w_in, w_br)


def _outproj_kernel(x_ref, mod_ref, m_ref, w_ref, o_ref):
    o_ref[...] = x_ref[...] + mod_ref[2:3, :] * jnp.dot(m_ref[...], w_ref[...], preferred_element_type=f32)


def _out_proj(x, mod, merged, w_out, layer):
    tn = 512
    return pl.pallas_call(
        _outproj_kernel,
        grid=(N_TOK // TM, D_MODEL // tn),
        in_specs=[
            pl.BlockSpec((TM, tn), lambda i, j: (i, j)),
            pl.BlockSpec((None, None, 6, tn), lambda i, j: (layer, _mod_row(i), 0, j)),
            pl.BlockSpec((TM, D_MODEL), lambda i, j: (i, 0)),
            pl.BlockSpec((None, D_MODEL, tn), lambda i, j: (layer, 0, j)),
        ],
        out_specs=pl.BlockSpec((TM, tn), lambda i, j: (i, j)),
        out_shape=jax.ShapeDtypeStruct((N_TOK, D_MODEL), f32),
        compiler_params=_cp(2),
        name="out_proj",
    )(x, mod, merged, w_out)


def _ffn_kernel(x_ref, mod_ref, nw_ref, w1_ref, w2_ref, o_ref, h_scr, *, nj):
    j = pl.program_id(1)

    @pl.when(j == 0)
    def _():
        h_scr[...] = _modnorm(x_ref[...], nw_ref[...], mod_ref[3:4, :], mod_ref[4:5, :]).astype(bf16)

    @pl.when(j == 0)
    def _():
        o_ref[...] = jnp.zeros_like(o_ref)

    a = jnp.maximum(jnp.dot(h_scr[...], w1_ref[...], preferred_element_type=f32), 0.0)
    a2 = (a * a).astype(bf16)
    cw = 512
    for cidx in range(D_MODEL // cw):
        cs = slice(cidx * cw, (cidx + 1) * cw)
        o_ref[:, cs] += jnp.dot(a2, w2_ref[:, cs], preferred_element_type=f32)

    @pl.when(j == nj - 1)
    def _():
        o_ref[...] = x_ref[...] + mod_ref[5:6, :] * o_ref[...]


def _ffn(x, mod, nw, w1, w2, layer):
    tf = 512
    nj = D_FF // tf
    return pl.pallas_call(
        functools.partial(_ffn_kernel, nj=nj),
        grid=(N_TOK // TM, nj),
        in_specs=[
            pl.BlockSpec((TM, D_MODEL), lambda i, j: (i, 0)),
            pl.BlockSpec((None, None, 6, D_MODEL), lambda i, j: (layer, _mod_row(i), 0, 0)),
            pl.BlockSpec((None, 1, D_MODEL), lambda i, j: (layer, 0, 0)),
            pl.BlockSpec((None, D_MODEL, tf), lambda i, j: (layer, 0, j)),
            pl.BlockSpec((None, tf, D_MODEL), lambda i, j: (layer, j, 0)),
        ],
        out_specs=pl.BlockSpec((TM, D_MODEL), lambda i, j: (i, 0)),
        out_shape=jax.ShapeDtypeStruct((N_TOK, D_MODEL), f32),
        scratch_shapes=[pltpu.VMEM((TM, D_MODEL), bf16)],
        compiler_params=_cp(2),
        name="ffn",
    )(x, mod, nw, w1, w2)


def _to_tm(a, nb, seq):
    c = a.shape[-1]
    return a.reshape(nb // SUB, SUB, seq, c).transpose(0, 2, 1, 3).reshape(nb // SUB, seq * SUB, c)


def _from_tm(a, nb, seq):
    c = a.shape[-1]
    return a.reshape(nb // SUB, seq, SUB, c).transpose(0, 2, 1, 3).reshape(nb * seq, c)


def _state_to_tm(s, nb):
    return s.reshape(nb // SUB, SUB, s.shape[-1])


def _rope_tables():
    t = jnp.arange(L_S)
    row = (t // GRID_W).astype(f32)
    col = (t % GRID_W).astype(f32)
    n_pairs = HEAD_DIM // 4
    freqs = ROPE_BASE ** (-jnp.arange(n_pairs, dtype=f32) / n_pairs)
    ang = jnp.concatenate([row[:, None] * freqs, col[:, None] * freqs], axis=-1)
    cos = jnp.cos(ang)
    sin = jnp.sin(ang)
    return jnp.concatenate([cos, cos], axis=-1), jnp.concatenate([-sin, sin], axis=-1)


def _block_diag_in(w):
    eye = jnp.eye(SSM_GROUPS, dtype=w.dtype)
    return jnp.einsum("gpc,gh->gchp", w, eye).reshape(SSM_WIDTH, SSM_N)


def _block_diag_out(w):
    eye = jnp.eye(SSM_GROUPS, dtype=w.dtype)
    return jnp.einsum("gcp,gh->gphc", w, eye).reshape(SSM_N, SSM_WIDTH)


def _ssm_params(a_re, a_im, log_dt, b_re, b_im, c_re, c_im):
    lam = lax.complex(a_re, a_im)
    dt = jnp.exp(log_dt)[..., None]
    a_bar = jnp.exp(lam * dt)
    b_bar = ((a_bar - 1.0) / lam)[..., None] * lax.complex(b_re, b_im)
    a = jnp.stack([a_bar.real.reshape(2, SSM_N), a_bar.imag.reshape(2, SSM_N)], axis=1)
    bd = jnp.stack([jnp.concatenate([_block_diag_in(b_bar[d].real), _block_diag_in(b_bar[d].imag)], axis=1)
                    for d in range(2)]).astype(bf16)
    cd = jnp.stack([jnp.concatenate([_block_diag_out(c_re[d]), -_block_diag_out(c_im[d])], axis=0)
                    for d in range(2)]).astype(bf16)
    return a, bd, cd


def kernel(x_prompt, x_sample, cache_attn_k, cache_attn_v, state_ret, state_lru, state_ssm_re, state_ssm_im, c, c_ctx, w_mod, b_mod, norm1, w_in, ret_decay_logit, ret_gn, lru_conv_w, lru_conv_b, lru_wa, lru_ba, lru_wx, lru_bx, lru_lambda, att_q_norm, att_k_norm, att_sink, ssm_a_re, ssm_a_im, ssm_log_dt, ssm_b_re, ssm_b_im, ssm_c_re, ssm_c_im, ssm_d, ssm_w_glu, ssm_b_glu, w_br, w_out, norm2, w_ff1, w_ff2):
    x = jnp.concatenate([x_prompt.reshape(N_P, D_MODEL), x_sample.reshape(N_S, D_MODEL)], axis=0)
    cond = jnp.concatenate([c, c_ctx[None, :], jnp.zeros((COND_ROWS - NB_S - 1, D_MODEL), f32)], axis=0)
    mod = _modulation(cond, w_mod, b_mod).reshape(DEPTH, COND_ROWS, 6, D_MODEL)

    w_in_b = w_in.astype(bf16)
    w_br_b = w_br.astype(bf16)
    w_out_b = w_out.astype(bf16)
    w_ff1_b = w_ff1.astype(bf16)
    w_ff2_b = w_ff2.astype(bf16)
    norm1_r = norm1.reshape(DEPTH, 1, D_MODEL)
    norm2_r = norm2.reshape(DEPTH, 1, D_MODEL)
    rope = _rope_tables()
    log_gamma = -jax.nn.softplus(-ret_decay_logit)
    lru_sp = jax.nn.softplus(-lru_lambda)

    ks, vs, rets, lrus, sres, sims = [], [], [], [], [], []
    for l in range(DEPTH):
        z = _proj_in(x, mod, norm1_r, w_in_b, l)

        lg = jnp.broadcast_to(log_gamma[l].T[:, :, None], (RET_HEADS, 2, 128))
        gn = ret_gn[l].reshape(RET_HEADS, 1, 128)
        o_ret_p, s_ret = _retention(z, lg, gn, None, nb=NB_P, seq=L_P, row0=0, emit_state=True)
        o_ret_s, _ = _retention(z, lg, gn, state_ret[:, l], nb=NB_S, seq=L_S, row0=N_P, emit_state=False)
        rets.append(s_ret)

        lru_prm = {
            "cw": lru_conv_w[l], "cb": lru_conv_b[l].reshape(1, LRU_WIDTH),
            "wa": lru_wa[l].astype(bf16), "ba": lru_ba[l].reshape(2, 1, LRU_WIDTH),
            "wx": lru_wx[l].astype(bf16), "bx": lru_bx[l].reshape(2, 1, LRU_WIDTH),
            "sp": lru_sp[l].reshape(2, 1, LRU_WIDTH),
        }
        o_lru, lru_fin = [], []
        for (nb, seq, row0, s0) in ((NB_P, L_P, 0, None), (NB_S, L_S, N_P, state_lru[:, l])):
            x_tm = _to_tm(z[row0:row0 + nb * seq, LX:LX + LRU_WIDTH], nb, seq)
            g_tm = _to_tm(z[row0:row0 + nb * seq, LG:LG + LRU_WIDTH], nb, seq)
            s0f = None if s0 is None else _state_to_tm(s0[:, 0], nb)
            s0b = None if s0 is None else _state_to_tm(s0[:, 1], nb)
            hf, fin_f = _lru_dir(x_tm, None, None, s0f, lru_prm, seq=seq, direction=0)
            y, fin_b = _lru_dir(x_tm, g_tm, hf, s0b, lru_prm, seq=seq, direction=1)
            o_lru.append(_from_tm(y, nb, seq))
            lru_fin.append((fin_f, fin_b))
        lrus.append(jnp.stack([lru_fin[0][0].reshape(NB_P, LRU_WIDTH),
                               lru_fin[0][1].reshape(NB_P, LRU_WIDTH)], axis=1))

        qw = att_q_norm[l].reshape(1, HEAD_DIM)
        kw = att_k_norm[l].reshape(1, HEAD_DIM)
        kvw = ATT_KV_HEADS * HEAD_DIM
        o_att_p, k_new, v_new = _attention(z, qw, kw, att_sink[l], None, None,
                                           nb=NB_P, seq=L_P, row0=0, latent=False)
        cache = (cache_attn_k[:, l].reshape(NB_S, PAST_LEN, kvw), cache_attn_v[:, l].reshape(NB_S, PAST_LEN, kvw))
        (o_att_s,) = _attention(z, qw, kw, att_sink[l], rope, cache, nb=NB_S, seq=L_S, row0=N_P, latent=True)
        ks.append(k_new.reshape(NB_P, L_P, ATT_KV_HEADS, HEAD_DIM))
        vs.append(v_new.reshape(NB_P, L_P, ATT_KV_HEADS, HEAD_DIM))

        ssm_a, ssm_bd, ssm_cd = _ssm_params(ssm_a_re[l], ssm_a_im[l], ssm_log_dt[l], ssm_b_re[l], ssm_b_im[l],
                                            ssm_c_re[l], ssm_c_im[l])
        ssm_prm = {"a": ssm_a, "bd": ssm_bd, "cd": ssm_cd, "d": ssm_d[l].reshape(1, SSM_WIDTH),
                   "wglu": ssm_w_glu[l].astype(bf16), "bglu": ssm_b_glu[l].reshape(1, SSM_WIDTH)}
        o_ssm, ssm_fin = [], []
        for (nb, seq, row0, has_s0) in ((NB_P, L_P, 0, False), (NB_S, L_S, N_P, True)):
            u_tm = _to_tm(z[row0:row0 + nb * seq, SU:SU + SSM_WIDTH], nb, seq)
            s0d = [None, None]
            if has_s0:
                for d in range(2):
                    s0d[d] = _state_to_tm(jnp.concatenate([state_ssm_re[:, l, d].reshape(nb, SSM_N),
                                                           state_ssm_im[:, l, d].reshape(nb, SSM_N)], axis=-1), nb)
            yf, fin_f = _s5_dir(u_tm, None, s0d[0], ssm_prm, seq=seq, direction=0)
            y, fin_b = _s5_dir(u_tm, yf, s0d[1], ssm_prm, seq=seq, direction=1)
            o_ssm.append(_from_tm(y, nb, seq))
            ssm_fin.append((fin_f, fin_b))
        fin = jnp.stack([ssm_fin[0][0].reshape(NB_P, 2 * SSM_N), ssm_fin[0][1].reshape(NB_P, 2 * SSM_N)], axis=1)
        sres.append(fin[:, :, :SSM_N].reshape(NB_P, 2, SSM_GROUPS, SSM_STATE))
        sims.append(fin[:, :, SSM_N:].reshape(NB_P, 2, SSM_GROUPS, SSM_STATE))

        branches = [jnp.concatenate(p, axis=0) for p in ((o_ret_p, o_ret_s), o_lru, (o_att_p, o_att_s), o_ssm)]
        merged = _merge(x, mod, norm1_r, branches, w_in_b, w_br_b, l)
        x = _out_proj(x, mod, merged, w_out_b, l)
        x = _ffn(x, mod, norm2_r, w_ff1_b, w_ff2_b, l)

    y_prompt = x[:N_P].reshape(NB_P, L_P, D_MODEL)
    y_sample = x[N_P:].reshape(NB_S, L_S, D_MODEL)
    return (y_prompt, y_sample, jnp.stack(ks, axis=1), jnp.stack(vs, axis=1), jnp.stack(rets, axis=1),
            jnp.stack(lrus, axis=1), jnp.stack(sres, axis=1), jnp.stack(sims, axis=1))
```

```python
import functools
from typing import NamedTuple

import jax
import jax.numpy as jnp
from jax import lax
from jax.experimental import pallas as pl
from jax.experimental.pallas import tpu as pltpu

f32 = jnp.float32
bf16 = jnp.bfloat16

D_MODEL = 2048
DEPTH = 4
PAST_LEN = 256
GRID_W = 64
CHUNK = 128
EPS = 1e-6
NEG_INF = -1e30
RET_HEADS = 4
RET_DK = 128
LRU_WIDTH = 512
LRU_BLOCKS = 4
LRU_C = 8.0
CONV_W = 4
ATT_Q_HEADS = 4
ATT_KV_HEADS = 2
HEAD_DIM = 128
WINDOW = 128
ROPE_BASE = 10000.0
SSM_WIDTH = 512
SSM_GROUP = 16
SSM_GROUPS = 32
SSM_STATE = 64
SSM_N = SSM_GROUPS * SSM_STATE
SSM_CB = 512
SSM_CI = SSM_CB // SSM_STATE * SSM_GROUP
N_BRANCH = 4
BRANCH_WIDTH = 512
D_FF = 4 * D_MODEL
RQ, RK, RV, RG, LX, LG, AQ, AK, AV, SU, ZW = 0, 512, 1024, 1536, 2048, 2560, 3072, 3584, 3840, 4096, 4608
SUB = 8
COND_ROWS = 16
VMEM_LIMIT = 56 * 1024 * 1024

TM = 1024
LRU_TC = 128
SSM_TC = 64
ROW_BLK = 16


class Group(NamedTuple):
    nb: int
    seq: int
    ctx: bool

    @property
    def n(self):
        return self.nb * self.seq


PROMPT = Group(32, 256, True)
SAMPLE = Group(8, 2048, False)
CTX_ROW = SAMPLE.nb


def _cp(n_axes, vmem=VMEM_LIMIT):
    return pltpu.CompilerParams(dimension_semantics=("arbitrary",) * n_axes, vmem_limit_bytes=vmem)


def _mod_row(grp, i):
    return CTX_ROW if grp.ctx else i // (grp.seq // TM)


def _modnorm_rows(x_ref, nw, shift, scale, h_ref, r_scr, gs_scr):
    tm, d = x_ref.shape
    lane_tiles = d // 128

    def ssq_body(r, carry):
        rows = pl.ds(pl.multiple_of(r * SUB, SUB), SUB)
        acc = None
        for t in range(lane_tiles):
            v = x_ref[rows, t * 128:(t + 1) * 128]
            acc = v * v if acc is None else acc + v * v
        r_scr[rows, :] = acc
        return carry

    lax.fori_loop(0, tm // SUB, ssq_body, 0, unroll=4)
    ssq = jnp.sum(r_scr[...], axis=-1, keepdims=True)
    r_scr[...] = jnp.broadcast_to(lax.rsqrt(ssq * (1.0 / d) + EPS), (tm, 128))
    gs_scr[0] = jnp.broadcast_to(nw * (1.0 + scale), (SUB, d))
    gs_scr[1] = jnp.broadcast_to(shift, (SUB, d))

    def out_body(r, carry):
        r0 = pl.multiple_of(r * ROW_BLK, ROW_BLK)
        halves = [pl.ds(pl.multiple_of(r0 + k * SUB, SUB), SUB) for k in range(ROW_BLK // SUB)]
        invs = [r_scr[rows, :] for rows in halves]
        for t in range(lane_tiles):
            cols = slice(t * 128, (t + 1) * 128)
            gain = gs_scr[0, :, cols]
            shf = gs_scr[1, :, cols]
            parts = [x_ref[rows, cols] * inv * gain + shf for rows, inv in zip(halves, invs)]
            h_ref[pl.ds(r0, ROW_BLK), cols] = jnp.concatenate(parts, axis=0).astype(bf16)
        return carry

    lax.fori_loop(0, tm // ROW_BLK, out_body, 0, unroll=2)


def _mod_kernel(c_ref, w_ref, b_ref, o_ref):
    c = c_ref[...]
    s = (c * jax.nn.sigmoid(c)).astype(bf16)
    o_ref[...] = jnp.dot(s, w_ref[...].astype(bf16), preferred_element_type=f32) + b_ref[...]


def _modulation(cond, w_mod, b_mod):
    tn = 1024
    n6 = 6 * D_MODEL
    return pl.pallas_call(
        _mod_kernel,
        grid=(DEPTH, n6 // tn),
        in_specs=[
            pl.BlockSpec((COND_ROWS, D_MODEL), lambda l, j: (0, 0)),
            pl.BlockSpec((None, D_MODEL, tn), lambda l, j: (l, 0, j)),
            pl.BlockSpec((None, 1, tn), lambda l, j: (l, 0, j)),
        ],
        out_specs=pl.BlockSpec((None, COND_ROWS, tn), lambda l, j: (l, 0, j)),
        out_shape=jax.ShapeDtypeStruct((DEPTH, COND_ROWS, n6), f32),
        compiler_params=_cp(2),
        name="modulation",
    )(cond, w_mod, b_mod.reshape(DEPTH, 1, n6))


def _proj_kernel(x_ref, mod_ref, nw_ref, w_ref, o_ref, h_ref, r_scr, gs_scr):
    @pl.when(pl.program_id(1) == 0)
    def _():
        _modnorm_rows(x_ref, nw_ref[...], mod_ref[0:1, :], mod_ref[1:2, :], h_ref, r_scr, gs_scr)

    o_ref[...] = jnp.dot(h_ref[...], w_ref[...], preferred_element_type=f32)


def _proj_in(x, mod, nw, w_in, layer, grp):
    tn = 512
    return pl.pallas_call(
        _proj_kernel,
        grid=(grp.n // TM, ZW // tn),
        in_specs=[
            pl.BlockSpec((TM, D_MODEL), lambda i, j: (i, 0)),
            pl.BlockSpec((None, None, 6, D_MODEL), lambda i, j: (layer, _mod_row(grp, i), 0, 0)),
            pl.BlockSpec((None, 1, D_MODEL), lambda i, j: (layer, 0, 0)),
            pl.BlockSpec((None, D_MODEL, tn), lambda i, j: (layer, 0, j)),
        ],
        out_specs=[pl.BlockSpec((TM, tn), lambda i, j: (i, j)),
                   pl.BlockSpec((TM, D_MODEL), lambda i, j: (i, 0))],
        out_shape=[jax.ShapeDtypeStruct((grp.n, ZW), f32), jax.ShapeDtypeStruct((grp.n, D_MODEL), bf16)],
        scratch_shapes=[pltpu.VMEM((TM, 128), f32), pltpu.VMEM((2, SUB, D_MODEL), f32)],
        compiler_params=_cp(2),
        name="proj_in",
    )(x, mod, nw, w_in)


def _ret_kernel(*refs, nc, has_s0, emit_state):
    q_ref, k_ref, v_ref, g_ref, lg_ref, gn_ref = refs[:6]
    pos = 6
    s0_ref = None
    if has_s0:
        s0_ref = refs[pos]
        pos += 1
    o_ref = refs[pos]
    pos += 1
    sout_ref = None
    if emit_state:
        sout_ref = refs[pos]
        pos += 1
    w_scr, kv_scr = refs[pos], refs[pos + 1]

    C = CHUNK
    H = RET_HEADS
    DEC, W_IN_F, W_IN_B, W_END_F, W_END_B = range(5)
    tn_dims = (((0,), (0,)), ((), ()))
    nt_dims = (((1,), (1,)), ((), ()))

    @pl.when(pl.program_id(0) == 0)
    def _():
        ii = lax.broadcasted_iota(jnp.int32, (C, C), 0).astype(f32)
        jj = lax.broadcasted_iota(jnp.int32, (C, C), 1).astype(f32)
        rel = ii - jj
        for h in range(H):
            lgf = lg_ref[h, 0:1, :]
            lgb = lg_ref[h, 1:2, :]
            w_scr[h, DEC] = (jnp.where(rel >= 0, jnp.exp(jnp.maximum(rel, 0.0) * lgf), 0.0)
                             + jnp.where(rel <= 0, jnp.exp(jnp.maximum(-rel, 0.0) * lgb), 0.0))
            w_scr[h, W_IN_F] = jnp.exp((ii + 1.0) * lgf)
            w_scr[h, W_IN_B] = jnp.exp((C - ii) * lgb)
            w_scr[h, W_END_F] = jnp.exp((C - 1.0 - ii) * lgf)
            w_scr[h, W_END_B] = jnp.exp(ii * lgb)

    def kv_body(n, carry):
        r = pl.multiple_of(n * C, C)
        for h in range(H):
            cols = slice(h * C, (h + 1) * C)
            k = k_ref[pl.ds(r, C), cols]
            vb = v_ref[pl.ds(r, C), cols].astype(bf16)
            kw = jnp.concatenate([k * w_scr[h, W_END_F], k * w_scr[h, W_END_B]], axis=1).astype(bf16)
            kv = lax.dot_general(kw, vb, tn_dims, preferred_element_type=f32)
            kv_scr[0, n * H + h] = kv[:C]
            kv_scr[1, n * H + h] = kv[C:]
        return carry

    lax.fori_loop(0, nc, kv_body, 0)

    for h in range(H):
        cd_f = jnp.exp(C * lg_ref[h, 0:1, :])
        cd_b = jnp.exp(C * lg_ref[h, 1:2, :])

        def fwd_body(n, s, h=h, cd_f=cd_f):
            kv = kv_scr[0, n * H + h]
            kv_scr[0, n * H + h] = s
            return cd_f * s + kv

        def bwd_body(i, s, h=h, cd_b=cd_b):
            n = nc - 1 - i
            kv = kv_scr[1, n * H + h]
            kv_scr[1, n * H + h] = s
            return cd_b * s + kv

        s_f = lax.fori_loop(0, nc, fwd_body, s0_ref[0, h] if has_s0 else jnp.zeros((C, C), f32))
        s_b = lax.fori_loop(0, nc, bwd_body, s0_ref[1, h] if has_s0 else jnp.zeros((C, C), f32))
        if emit_state:
            sout_ref[0, h] = s_f
            sout_ref[1, h] = s_b

    def out_body(n, carry):
        r = pl.multiple_of(n * C, C)
        for h in range(H):
            cols = slice(h * C, (h + 1) * C)
            q = q_ref[pl.ds(r, C), cols] * (RET_DK ** -0.5)
            kb = k_ref[pl.ds(r, C), cols].astype(bf16)
            vb = v_ref[pl.ds(r, C), cols].astype(bf16)
            sc = lax.dot_general(q.astype(bf16), kb, nt_dims, preferred_element_type=f32) * w_scr[h, DEC]
            lhs = jnp.concatenate([sc, q * w_scr[h, W_IN_F], q * w_scr[h, W_IN_B]], axis=1).astype(bf16)
            rhs = jnp.concatenate([vb, kv_scr[0, n * H + h].astype(bf16), kv_scr[1, n * H + h].astype(bf16)], axis=0)
            o = jnp.dot(lhs, rhs, preferred_element_type=f32)
            mu = jnp.mean(o, axis=-1, keepdims=True)
            d = o - mu
            var = jnp.mean(d * d, axis=-1, keepdims=True)
            on = d * lax.rsqrt(var + EPS) * gn_ref[:, cols]
            g = g_ref[pl.ds(r, C), cols]
            o_ref[pl.ds(r, C), cols] = (g * jax.nn.sigmoid(g) * on).astype(bf16)
        return carry

    lax.fori_loop(0, nc, out_body, 0)


def _retention(z, lg, gn, s0, grp):
    seq = grp.seq
    W = RET_HEADS * 128
    in_specs = [
        pl.BlockSpec((seq, W), lambda b: (b, RQ // W)),
        pl.BlockSpec((seq, W), lambda b: (b, RK // W)),
        pl.BlockSpec((seq, W), lambda b: (b, RV // W)),
        pl.BlockSpec((seq, W), lambda b: (b, RG // W)),
        pl.BlockSpec((RET_HEADS, 2, 128), lambda b: (0, 0, 0)),
        pl.BlockSpec((1, W), lambda b: (0, 0)),
    ]
    args = [z, z, z, z, lg, gn]
    st_spec = pl.BlockSpec((None, 2, RET_HEADS, 128, 128), lambda b: (b, 0, 0, 0, 0))
    if s0 is not None:
        in_specs.append(st_spec)
        args.append(s0)
    out_specs = [pl.BlockSpec((seq, W), lambda b: (b, 0))]
    out_shape = [jax.ShapeDtypeStruct((grp.n, W), bf16)]
    if grp.ctx:
        out_specs.append(st_spec)
        out_shape.append(jax.ShapeDtypeStruct((grp.nb, 2, RET_HEADS, 128, 128), f32))
    nc = seq // CHUNK
    res = pl.pallas_call(
        functools.partial(_ret_kernel, nc=nc, has_s0=s0 is not None, emit_state=grp.ctx),
        grid=(grp.nb,),
        in_specs=in_specs,
        out_specs=out_specs,
        out_shape=out_shape,
        scratch_shapes=[pltpu.VMEM((RET_HEADS, 5, 128, 128), f32),
                        pltpu.VMEM((2, nc * RET_HEADS, 128, 128), f32)],
        compiler_params=_cp(1),
        name="retention",
    )(*args)
    return res if grp.ctx else (res[0], None)


def _gather_tm(src_ref, t0, nt, dst_ref, row0):
    w = src_ref.shape[-1]

    def body(t, carry):
        v = src_ref[:, pl.ds(t0 + t, 1), :]
        dst_ref[pl.ds(pl.multiple_of(row0 + t * SUB, SUB), SUB), :] = v.reshape(SUB, w)
        return carry

    lax.fori_loop(0, nt, body, 0, unroll=4)


def _scatter_bm(src_ref, nt, dst_ref):
    w = src_ref.shape[-1]

    def body(t, carry):
        v = src_ref[pl.ds(pl.multiple_of(t * SUB, SUB), SUB), :]
        dst_ref[:, pl.ds(t, 1), :] = v.reshape(SUB, 1, w)
        return carry

    lax.fori_loop(0, nt, body, 0, unroll=4)


def _lru_kernel(*refs, tc, nch, direction, has_s0):
    xprev_ref, xcur_ref, xnext_ref, cw_ref, cb_ref, wa_ref, ba_ref, wx_ref, bx_ref, sp_ref = refs[:10]
    pos = 10
    s0_ref = None
    if has_s0:
        s0_ref = refs[pos]
        pos += 1
    if direction == 1:
        gate_ref, hf_ref = refs[pos], refs[pos + 1]
        pos += 2
    out_ref, fin_ref = refs[pos], refs[pos + 1]
    xe_scr, a_scr, b_scr, h_scr = refs[pos + 2: pos + 6]
    if direction == 1:
        g_scr, y_scr = refs[pos + 6], refs[pos + 7]

    R = tc * SUB
    W = LRU_WIDTH
    j = pl.program_id(1)
    c = j if direction == 0 else nch - 1 - j

    @pl.when(j == 0)
    def _():
        h_scr[...] = s0_ref[...] if has_s0 else jnp.zeros((SUB, W), f32)

    for t in range(2):
        v = xprev_ref[:, SUB - 2 + t:SUB - 1 + t, :].reshape(SUB, W)
        xe_scr[t * SUB:(t + 1) * SUB, :] = jnp.where(c > 0, v, 0.0)
    _gather_tm(xcur_ref, 0, tc, xe_scr, 2 * SUB)
    xe_scr[2 * SUB + R:3 * SUB + R, :] = jnp.where(c < nch - 1, xnext_ref[:, 0:1, :].reshape(SUB, W), 0.0)
    if direction == 1:
        _gather_tm(gate_ref, 0, tc, g_scr, 0)

    xc = cb_ref[...] + xe_scr[0:R, :] * cw_ref[0:1, :]
    for t in range(1, CONV_W):
        xc = xc + xe_scr[t * SUB:t * SUB + R, :] * cw_ref[t:t + 1, :]

    bd = W // LRU_BLOCKS
    for n in range(LRU_BLOCKS):
        sl = slice(n * bd, (n + 1) * bd)
        xs = xc[:, sl]
        xb = xs.astype(bf16)
        r = jax.nn.sigmoid(jnp.dot(xb, wa_ref[n], preferred_element_type=f32) + ba_ref[:, sl])
        i = jax.nn.sigmoid(jnp.dot(xb, wx_ref[n], preferred_element_type=f32) + bx_ref[:, sl])
        log_a = -LRU_C * r * sp_ref[:, sl]
        a = jnp.exp(log_a)
        a_scr[:, sl] = a
        b_scr[:, sl] = jnp.sqrt(-jnp.tanh(log_a) * (a * a + 1.0)) * (i * xs)

    def step(s, h):
        t = s if direction == 0 else tc - 1 - s
        r0 = pl.multiple_of(t * SUB, SUB)
        h = a_scr[pl.ds(r0, SUB), :] * h + b_scr[pl.ds(r0, SUB), :]
        b_scr[pl.ds(r0, SUB), :] = h
        return h

    h = lax.fori_loop(0, tc, step, h_scr[...], unroll=8)
    h_scr[...] = h
    fin_ref[...] = h
    if direction == 0:
        out_ref[...] = b_scr[...]
    else:
        b_scr[...] = jax.nn.gelu(g_scr[...]) * (hf_ref[...] + b_scr[...])
        _scatter_bm(b_scr, tc, y_scr)
        out_ref[...] = y_scr[...].astype(bf16)


def _lru_dir(z3, hf_tm, s0, prm, grp, direction):
    nb, seq = grp.nb, grp.seq
    nbg = nb // SUB
    tc = min(LRU_TC, seq)
    nch = seq // tc
    R = tc * SUB
    W = LRU_WIDTH

    def cidx(j):
        return j if direction == 0 else nch - 1 - j

    in_specs = [
        pl.BlockSpec((SUB, SUB, W), lambda g, j: (g, jnp.maximum(cidx(j) * (tc // SUB) - 1, 0), LX // W)),
        pl.BlockSpec((SUB, tc, W), lambda g, j: (g, cidx(j), LX // W)),
        pl.BlockSpec((SUB, SUB, W), lambda g, j: (g, jnp.minimum((cidx(j) + 1) * (tc // SUB), seq // SUB - 1), LX // W)),
        pl.BlockSpec((CONV_W, W), lambda g, j: (0, 0)),
        pl.BlockSpec((1, W), lambda g, j: (0, 0)),
        pl.BlockSpec((LRU_BLOCKS, 128, 128), lambda g, j: (0, 0, 0)),
        pl.BlockSpec((1, W), lambda g, j: (0, 0)),
        pl.BlockSpec((LRU_BLOCKS, 128, 128), lambda g, j: (0, 0, 0)),
        pl.BlockSpec((1, W), lambda g, j: (0, 0)),
        pl.BlockSpec((1, W), lambda g, j: (0, 0)),
    ]
    args = [z3, z3, z3, prm["cw"], prm["cb"], prm["wa"][direction], prm["ba"][direction],
            prm["wx"][direction], prm["bx"][direction], prm["sp"][direction]]
    st_spec = pl.BlockSpec((SUB, W), lambda g, j: (g, 0))
    if s0 is not None:
        in_specs.append(st_spec)
        args.append(s0)
    tm_spec = pl.BlockSpec((None, R, W), lambda g, j: (g, cidx(j), 0))
    scratch = [pltpu.VMEM((R + 3 * SUB, W), f32), pltpu.VMEM((R, W), f32),
               pltpu.VMEM((R, W), f32), pltpu.VMEM((SUB, W), f32)]
    if direction == 0:
        out_spec = tm_spec
        out_sds = jax.ShapeDtypeStruct((nbg, seq * SUB, W), f32)
    else:
        in_specs += [pl.BlockSpec((SUB, tc, W), lambda g, j: (g, cidx(j), LG // W)), tm_spec]
        args += [z3, hf_tm]
        out_spec = pl.BlockSpec((SUB, tc, W), lambda g, j: (g, cidx(j), 0))
        out_sds = jax.ShapeDtypeStruct((nb, seq, W), bf16)
        scratch += [pltpu.VMEM((R, W), f32), pltpu.VMEM((SUB, tc, W), f32)]
    return pl.pallas_call(
        functools.partial(_lru_kernel, tc=tc, nch=nch, direction=direction, has_s0=s0 is not None),
        grid=(nbg, nch),
        in_specs=in_specs,
        out_specs=[out_spec, st_spec],
        out_shape=[out_sds, jax.ShapeDtypeStruct((nb, W), f32)],
        scratch_shapes=scratch,
        compiler_params=_cp(2),
        name="rglru_dir%d" % direction,
    )(*args)


def _s5_kernel(*refs, tc, nch, direction, has_s0):
    u_ref, bd_ref, a_ref, cd_ref = refs[:4]
    pos = 4
    s0_ref = None
    if has_s0:
        s0_ref = refs[pos]
        pos += 1
    if direction == 1:
        yf_ref, dv_ref, wg_ref, bg_ref = refs[pos:pos + 4]
        pos += 4
    out_ref, fin_ref = refs[pos], refs[pos + 1]
    u_scr, hs_scr, h_scr = refs[pos + 2: pos + 5]
    if direction == 1:
        y_scr = refs[pos + 5]

    j = pl.program_id(1)

    @pl.when(j == 0)
    def _():
        h_scr[...] = s0_ref[...] if has_s0 else jnp.zeros((SUB, 2 * SSM_N), f32)

    _gather_tm(u_ref, 0, tc, u_scr, 0)
    u = u_scr[...]
    ub = u.astype(bf16)

    cbw = SSM_CB
    for cb in range(SSM_N // cbw):
        cre = slice(cb * cbw, (cb + 1) * cbw)
        cim = slice(SSM_N + cb * cbw, SSM_N + (cb + 1) * cbw)
        cin = slice(cb * SSM_CI, (cb + 1) * SSM_CI)
        hs_scr[:, cre] = jnp.dot(ub[:, cin], bd_ref[0, cb], preferred_element_type=f32)
        hs_scr[:, cim] = jnp.dot(ub[:, cin], bd_ref[1, cb], preferred_element_type=f32)
        a_re = jnp.broadcast_to(a_ref[0:1, cre], (SUB, cbw))
        a_im = jnp.broadcast_to(a_ref[1:2, cre], (SUB, cbw))

        def step(s, carry):
            hr, hi = carry
            t = s if direction == 0 else tc - 1 - s
            r0 = pl.multiple_of(t * SUB, SUB)
            nr = a_re * hr - a_im * hi + hs_scr[pl.ds(r0, SUB), cre]
            ni = a_re * hi + a_im * hr + hs_scr[pl.ds(r0, SUB), cim]
            hs_scr[pl.ds(r0, SUB), cre] = nr
            hs_scr[pl.ds(r0, SUB), cim] = ni
            return nr, ni

        hr, hi = lax.fori_loop(0, tc, step, (h_scr[:, cre], h_scr[:, cim]), unroll=4)
        h_scr[:, cre] = hr
        h_scr[:, cim] = hi

    fin_ref[...] = h_scr[...]
    y_parts = []
    for cb in range(SSM_N // cbw):
        cre = slice(cb * cbw, (cb + 1) * cbw)
        cim = slice(SSM_N + cb * cbw, SSM_N + (cb + 1) * cbw)
        y_parts.append(jnp.dot(hs_scr[:, cre].astype(bf16), cd_ref[0, cb], preferred_element_type=f32)
                       + jnp.dot(hs_scr[:, cim].astype(bf16), cd_ref[1, cb], preferred_element_type=f32))
    y = jnp.concatenate(y_parts, axis=1)
    if direction == 0:
        out_ref[...] = y
    else:
        yy = jax.nn.gelu(yf_ref[...] + y + dv_ref[...] * u)
        gl = jnp.dot(yy.astype(bf16), wg_ref[...], preferred_element_type=f32) + bg_ref[...]
        u_scr[...] = yy * jax.nn.sigmoid(gl)
        _scatter_bm(u_scr, tc, y_scr)
        out_ref[...] = y_scr[...].astype(bf16)


def _s5_dir(z3, yf_tm, s0, prm, grp, direction):
    nb, seq = grp.nb, grp.seq
    nbg = nb // SUB
    tc = min(SSM_TC, seq)
    nch = seq // tc
    R = tc * SUB
    W = SSM_WIDTH
    ncb = SSM_N // SSM_CB

    def cidx(j):
        return j if direction == 0 else nch - 1 - j

    tm_spec = pl.BlockSpec((None, R, W), lambda g, j: (g, cidx(j), 0))
    st_spec = pl.BlockSpec((SUB, 2 * SSM_N), lambda g, j: (g, 0))
    in_specs = [
        pl.BlockSpec((SUB, tc, W), lambda g, j: (g, cidx(j), SU // W)),
        pl.BlockSpec((2, ncb, SSM_CI, SSM_CB), lambda g, j: (0, 0, 0, 0)),
        pl.BlockSpec((2, SSM_N), lambda g, j: (0, 0)),
        pl.BlockSpec((2, ncb, SSM_CB, SSM_CI), lambda g, j: (0, 0, 0, 0)),
    ]
    args = [z3, prm["bd"][direction], prm["a"][direction], prm["cd"][direction]]
    if s0 is not None:
        in_specs.append(st_spec)
        args.append(s0)
    scratch = [pltpu.VMEM((R, W), f32), pltpu.VMEM((R, 2 * SSM_N), f32), pltpu.VMEM((SUB, 2 * SSM_N), f32)]
    if direction == 0:
        out_spec = tm_spec
        out_sds = jax.ShapeDtypeStruct((nbg, seq * SUB, W), f32)
    else:
        in_specs += [tm_spec, pl.BlockSpec((1, W), lambda g, j: (0, 0)),
                     pl.BlockSpec((W, W), lambda g, j: (0, 0)), pl.BlockSpec((1, W), lambda g, j: (0, 0))]
        args += [yf_tm, prm["d"], prm["wglu"], prm["bglu"]]
        out_spec = pl.BlockSpec((SUB, tc, W), lambda g, j: (g, cidx(j), 0))
        out_sds = jax.ShapeDtypeStruct((nb, seq, W), bf16)
        scratch.append(pltpu.VMEM((SUB, tc, W), f32))
    return pl.pallas_call(
        functools.partial(_s5_kernel, tc=tc, nch=nch, direction=direction, has_s0=s0 is not None),
        grid=(nbg, nch),
        in_specs=in_specs,
        out_specs=[out_spec, st_spec],
        out_shape=[out_sds, jax.ShapeDtypeStruct((nb, 2 * SSM_N), f32)],
        scratch_shapes=scratch,
        compiler_params=_cp(2),
        name="s5_dir%d" % direction,
    )(*args)


def _rope(x, cos, sin_signed):
    return x * cos + pltpu.roll(x, HEAD_DIM // 2, 1) * sin_signed


def _unit_rms(x, w):
    return x * lax.rsqrt(jnp.mean(x * x, axis=-1, keepdims=True) + EPS) * w


def _attn_kernel(*refs, seq, latent):
    q_ref, k_ref, v_ref, qw_ref, kw_ref, sink_ref = refs[:6]
    pos = 6
    if latent:
        cos_ref, sin_ref, ck_ref, cv_ref = refs[pos:pos + 4]
        pos += 4
    o_ref = refs[pos]
    pos += 1
    if not latent:
        kout_ref, vout_ref = refs[pos], refs[pos + 1]
        pos += 2
    kb_scr, vb_scr = refs[pos], refs[pos + 1]
    pos += 2
    if latent:
        ckb_scr, cvb_scr = refs[pos], refs[pos + 1]

    HD = HEAD_DIM
    qb = pl.program_id(1)

    @pl.when(qb == 0)
    def _():
        for hk in range(ATT_KV_HEADS):
            sl = slice(hk * HD, (hk + 1) * HD)
            kn = _unit_rms(k_ref[:, sl], kw_ref[...])
            if latent:
                kn = _rope(kn, cos_ref[...], sin_ref[...])
            else:
                kout_ref[:, sl] = kn
            kb_scr[:, sl] = kn.astype(bf16)
        v = v_ref[...]
        vb_scr[...] = v.astype(bf16)
        if latent:
            ckb_scr[...] = ck_ref[...].astype(bf16)
            cvb_scr[...] = cv_ref[...].astype(bf16)
        else:
            vout_ref[...] = v

    q0 = pl.multiple_of(qb * CHUNK, CHUNK)
    if latent:
        win = 3 * CHUNK
        start = pl.multiple_of(jnp.clip(q0 - CHUNK, 0, seq - win), CHUNK)
        qpos = q0 + (lax.broadcasted_iota(jnp.int32, (2 * CHUNK, win), 0) & (CHUNK - 1))
        kpos = start + lax.broadcasted_iota(jnp.int32, (2 * CHUNK, win), 1)
        band = jnp.abs(qpos - kpos) <= WINDOW
        cos_q = cos_ref[pl.ds(q0, CHUNK), :]
        sin_q = sin_ref[pl.ds(q0, CHUNK), :]
    else:
        win = seq
        start = 0
    scale = HD ** -0.5
    nt_dims = (((1,), (1,)), ((), ()))
    row = lax.broadcasted_iota(jnp.int32, (2 * CHUNK, 1), 0)
    for hk in range(ATT_KV_HEADS):
        sl = slice(hk * HD, (hk + 1) * HD)
        qs = []
        for g in range(2):
            hq = hk * 2 + g
            qn = _unit_rms(q_ref[:, hq * HD:(hq + 1) * HD], qw_ref[...])
            if latent:
                qn = _rope(qn, cos_q, sin_q)
            qs.append(qn.astype(bf16))
        qg = jnp.concatenate(qs, axis=0)
        snk = jnp.where(row < CHUNK, sink_ref[hk * 2], sink_ref[hk * 2 + 1])
        s1 = lax.dot_general(qg, kb_scr[pl.ds(start, win), sl], nt_dims, preferred_element_type=f32) * scale
        if latent:
            s1 = jnp.where(band, s1, NEG_INF)
        m = jnp.maximum(jnp.max(s1, axis=-1, keepdims=True), snk)
        if latent:
            s2 = lax.dot_general(qg, ckb_scr[:, sl], nt_dims, preferred_element_type=f32) * scale
            m = jnp.maximum(m, jnp.max(s2, axis=-1, keepdims=True))
            p2 = jnp.exp(s2 - m)
        p1 = jnp.exp(s1 - m)
        den = jnp.sum(p1, axis=-1, keepdims=True) + jnp.exp(snk - m)
        if latent:
            den = den + jnp.sum(p2, axis=-1, keepdims=True)
        o = jnp.dot((p1 / den).astype(bf16), vb_scr[pl.ds(start, win), sl], preferred_element_type=f32)
        if latent:
            o += jnp.dot((p2 / den).astype(bf16), cvb_scr[:, sl], preferred_element_type=f32)
        for g in range(2):
            hq = hk * 2 + g
            o_ref[:, hq * HD:(hq + 1) * HD] = o[g * CHUNK:(g + 1) * CHUNK].astype(bf16)


def _attention(z, qw, kw, sink, rope, cache, grp):
    nb, seq = grp.nb, grp.seq
    latent = not grp.ctx
    nqb = seq // CHUNK
    kvw = ATT_KV_HEADS * HEAD_DIM
    in_specs = [
        pl.BlockSpec((CHUNK, 512), lambda b, i: (b * nqb + i, AQ // 512)),
        pl.BlockSpec((seq, kvw), lambda b, i: (b, AK // kvw)),
        pl.BlockSpec((seq, kvw), lambda b, i: (b, AV // kvw)),
        pl.BlockSpec((1, HEAD_DIM), lambda b, i: (0, 0)),
        pl.BlockSpec((1, HEAD_DIM), lambda b, i: (0, 0)),
        pl.BlockSpec(memory_space=pltpu.SMEM),
    ]
    args = [z, z, z, qw, kw, sink]
    scratch = [pltpu.VMEM((seq, kvw), bf16), pltpu.VMEM((seq, kvw), bf16)]
    out_specs = [pl.BlockSpec((CHUNK, 512), lambda b, i: (b * nqb + i, 0))]
    out_shape = [jax.ShapeDtypeStruct((nb * seq, ATT_Q_HEADS * HEAD_DIM), bf16)]
    if latent:
        in_specs += [pl.BlockSpec((seq, HEAD_DIM), lambda b, i: (0, 0)),
                     pl.BlockSpec((seq, HEAD_DIM), lambda b, i: (0, 0)),
                     pl.BlockSpec((None, PAST_LEN, kvw), lambda b, i: (b, 0, 0)),
                     pl.BlockSpec((None, PAST_LEN, kvw), lambda b, i: (b, 0, 0))]
        args += [rope[0], rope[1], cache[0], cache[1]]
        scratch += [pltpu.VMEM((PAST_LEN, kvw), bf16), pltpu.VMEM((PAST_LEN, kvw), bf16)]
    else:
        kv_out = pl.BlockSpec((seq, kvw), lambda b, i: (b, 0))
        out_specs += [kv_out, kv_out]
        out_shape += [jax.ShapeDtypeStruct((nb * seq, kvw), f32)] * 2
    return pl.pallas_call(
        functools.partial(_attn_kernel, seq=seq, latent=latent),
        grid=(nb, nqb),
        in_specs=in_specs,
        out_specs=out_specs,
        out_shape=out_shape,
        scratch_shapes=scratch,
        compiler_params=_cp(2),
        name="attention_lat" if latent else "attention_ctx",
    )(*args)


def _merge_kernel(h_ref, o0, o1, o2, o3, g0, g1, g2, g3, wbr_ref, out_ref):
    h = h_ref[...]
    acc = None
    for b, (o_ref, g_ref) in enumerate(((o0, g0), (o1, g1), (o2, g2), (o3, g3))):
        gate = jax.nn.sigmoid(jnp.dot(h, g_ref[...], preferred_element_type=f32))
        t = gate * jnp.dot(o_ref[...], wbr_ref[b], preferred_element_type=f32)
        acc = t if acc is None else acc + t
    out_ref[...] = acc.astype(bf16)


def _merge(h, branches, w_in, w_br, layer, grp):
    tn = 256
    g0 = ZW // tn
    gb = D_MODEL // tn
    o_spec = pl.BlockSpec((TM, BRANCH_WIDTH), lambda i, j: (i, 0))

    def gate_spec(b):
        return pl.BlockSpec((None, D_MODEL, tn), lambda i, j: (layer, 0, g0 + b * gb + j))

    return pl.pallas_call(
        _merge_kernel,
        grid=(grp.n // TM, D_MODEL // tn),
        in_specs=[
            pl.BlockSpec((TM, D_MODEL), lambda i, j: (i, 0)),
            o_spec, o_spec, o_spec, o_spec,
            gate_spec(0), gate_spec(1), gate_spec(2), gate_spec(3),
            pl.BlockSpec((None, N_BRANCH, BRANCH_WIDTH, tn), lambda i, j: (layer, 0, 0, j)),
        ],
        out_specs=pl.BlockSpec((TM, tn), lambda i, j: (i, j)),
        out_shape=jax.ShapeDtypeStruct((grp.n, D_MODEL), bf16),
        compiler_params=_cp(2),
        name="merge",
    )(h, *branches, w_in, w_in, w_in, w_in, w_br)


def _outproj_kernel(x_ref, mod_ref, m_ref, w_ref, o_ref):
    o_ref[...] = x_ref[...] + mod_ref[2:3, :] * jnp.dot(m_ref[...], w_ref[...], preferred_element_type=f32)


def _out_proj(x, mod, merged, w_out, layer, grp):
    tn = 512
    return pl.pallas_call(
        _outproj_kernel,
        grid=(grp.n // TM, D_MODEL // tn),
        in_specs=[
            pl.BlockSpec((TM, tn), lambda i, j: (i, j)),
            pl.BlockSpec((None, None, 6, tn), lambda i, j: (layer, _mod_row(grp, i), 0, j)),
            pl.BlockSpec((TM, D_MODEL), lambda i, j: (i, 0)),
            pl.BlockSpec((None, D_MODEL, tn), lambda i, j: (layer, 0, j)),
        ],
        out_specs=pl.BlockSpec((TM, tn), lambda i, j: (i, j)),
        out_shape=jax.ShapeDtypeStruct((grp.n, D_MODEL), f32),
        compiler_params=_cp(2),
        name="out_proj",
    )(x, mod, merged, w_out)


def _ffn_kernel(x_ref, mod_ref, nw_ref, w1_ref, w2_ref, o_ref, h_scr, r_scr, gs_scr, *, nj):
    j = pl.program_id(1)

    @pl.when(j == 0)
    def _():
        _modnorm_rows(x_ref, nw_ref[...], mod_ref[3:4, :], mod_ref[4:5, :], h_scr, r_scr, gs_scr)
        o_ref[...] = jnp.zeros_like(o_ref)

    a = jnp.maximum(jnp.dot(h_scr[...], w1_ref[...], preferred_element_type=f32), 0.0)
    a2 = (a * a).astype(bf16)
    cw = 512
    for cidx in range(D_MODEL // cw):
        cs = slice(cidx * cw, (cidx + 1) * cw)
        o_ref[:, cs] += jnp.dot(a2, w2_ref[:, cs], preferred_element_type=f32)

    @pl.when(j == nj - 1)
    def _():
        o_ref[...] = x_ref[...] + mod_ref[5:6, :] * o_ref[...]


def _ffn(x, mod, nw, w1, w2, layer, grp):
    tf = 512
    nj = D_FF // tf
    return pl.pallas_call(
        functools.partial(_ffn_kernel, nj=nj),
        grid=(grp.n // TM, nj),
        in_specs=[
            pl.BlockSpec((TM, D_MODEL), lambda i, j: (i, 0)),
            pl.BlockSpec((None, None, 6, D_MODEL), lambda i, j: (layer, _mod_row(grp, i), 0, 0)),
            pl.BlockSpec((None, 1, D_MODEL), lambda i, j: (layer, 0, 0)),
            pl.BlockSpec((None, D_MODEL, tf), lambda i, j: (layer, 0, j)),
            pl.BlockSpec((None, tf, D_MODEL), lambda i, j: (layer, j, 0)),
        ],
        out_specs=pl.BlockSpec((TM, D_MODEL), lambda i, j: (i, 0)),
        out_shape=jax.ShapeDtypeStruct((grp.n, D_MODEL), f32),
        scratch_shapes=[pltpu.VMEM((TM, D_MODEL), bf16), pltpu.VMEM((TM, 128), f32),
                        pltpu.VMEM((2, SUB, D_MODEL), f32)],
        compiler_params=_cp(2),
        name="ffn",
    )(x, mod, nw, w1, w2)


def _rope_tables(seq):
    t = jnp.arange(seq)
    row = (t // GRID_W).astype(f32)
    col = (t % GRID_W).astype(f32)
    n_pairs = HEAD_DIM // 4
    freqs = ROPE_BASE ** (-jnp.arange(n_pairs, dtype=f32) / n_pairs)
    ang = jnp.concatenate([row[:, None] * freqs, col[:, None] * freqs], axis=-1)
    cos = jnp.cos(ang)
    sin = jnp.sin(ang)
    return jnp.concatenate([cos, cos], axis=-1), jnp.concatenate([-sin, sin], axis=-1)


def _block_diag_in(w):
    eye = jnp.eye(SSM_GROUPS, dtype=w.dtype)
    return jnp.einsum("gpc,gh->gchp", w, eye).reshape(SSM_WIDTH, SSM_N)


def _block_diag_out(w):
    eye = jnp.eye(SSM_GROUPS, dtype=w.dtype)
    return jnp.einsum("gcp,gh->gphc", w, eye).reshape(SSM_N, SSM_WIDTH)


def _ssm_params(a_re, a_im, log_dt, b_re, b_im, c_re, c_im):
    dt = jnp.exp(log_dt)[..., None]
    mag = jnp.exp(a_re * dt)
    ab_re = mag * jnp.cos(a_im * dt)
    ab_im = mag * jnp.sin(a_im * dt)
    den = a_re * a_re + a_im * a_im
    q_re = (((ab_re - 1.0) * a_re + ab_im * a_im) / den)[..., None]
    q_im = ((ab_im * a_re - (ab_re - 1.0) * a_im) / den)[..., None]
    bb_re = q_re * b_re - q_im * b_im
    bb_im = q_re * b_im + q_im * b_re
    a = jnp.stack([ab_re.reshape(2, SSM_N), ab_im.reshape(2, SSM_N)], axis=1)

    def diag_blocks(full, rows, cols):
        return jnp.stack([full[i * rows:(i + 1) * rows, i * cols:(i + 1) * cols] for i in range(SSM_N // SSM_CB)])

    bd = jnp.stack([jnp.stack([diag_blocks(_block_diag_in(w[d]), SSM_CI, SSM_CB) for w in (bb_re, bb_im)])
                    for d in range(2)]).astype(bf16)
    cd = jnp.stack([jnp.stack([diag_blocks(_block_diag_out(w[d]), SSM_CB, SSM_CI) for w in (c_re, -c_im)])
                    for d in range(2)]).astype(bf16)
    return a, bd, cd


def _mixers(z, grp, layer, prm, states):
    nb, seq = grp.nb, grp.seq
    z3 = z.reshape(nb, seq, ZW)
    s_ret0, s_lru0, s_ssm0, cache = states

    o_ret, s_ret = _retention(z, prm["ret_lg"], prm["ret_gn"], s_ret0, grp)

    hf, lru_f = _lru_dir(z3, None, None if s_lru0 is None else s_lru0[:, 0], prm["lru"], grp, 0)
    o_lru, lru_b = _lru_dir(z3, hf, None if s_lru0 is None else s_lru0[:, 1], prm["lru"], grp, 1)

    att = _attention(z, prm["att_qw"], prm["att_kw"], prm["att_sink"], prm["rope"], cache, grp)

    yf, ssm_f = _s5_dir(z3, None, None if s_ssm0 is None else s_ssm0[:, 0], prm["ssm"], grp, 0)
    o_ssm, ssm_b = _s5_dir(z3, yf, None if s_ssm0 is None else s_ssm0[:, 1], prm["ssm"], grp, 1)

    branches = [o_ret, o_lru.reshape(grp.n, LRU_WIDTH), att[0], o_ssm.reshape(grp.n, SSM_WIDTH)]
    new_state = None
    if grp.ctx:
        ssm_fin = jnp.stack([ssm_f, ssm_b], axis=1)
        new_state = (att[1].reshape(nb, seq, ATT_KV_HEADS, HEAD_DIM), att[2].reshape(nb, seq, ATT_KV_HEADS, HEAD_DIM),
                     s_ret, jnp.stack([lru_f, lru_b], axis=1),
                     ssm_fin[:, :, :SSM_N].reshape(nb, 2, SSM_GROUPS, SSM_STATE),
                     ssm_fin[:, :, SSM_N:].reshape(nb, 2, SSM_GROUPS, SSM_STATE))
    return branches, new_state


def kernel(x_prompt, x_sample, cache_attn_k, cache_attn_v, state_ret, state_lru, state_ssm_re, state_ssm_im, c, c_ctx, w_mod, b_mod, norm1, w_in, ret_decay_logit, ret_gn, lru_conv_w, lru_conv_b, lru_wa, lru_ba, lru_wx, lru_bx, lru_lambda, att_q_norm, att_k_norm, att_sink, ssm_a_re, ssm_a_im, ssm_log_dt, ssm_b_re, ssm_b_im, ssm_c_re, ssm_c_im, ssm_d, ssm_w_glu, ssm_b_glu, w_br, w_out, norm2, w_ff1, w_ff2):
    xs = {PROMPT: x_prompt.reshape(PROMPT.n, D_MODEL), SAMPLE: x_sample.reshape(SAMPLE.n, D_MODEL)}
    cond = jnp.concatenate([c, c_ctx[None, :], jnp.zeros((COND_ROWS - SAMPLE.nb - 1, D_MODEL), f32)], axis=0)
    mod = _modulation(cond, w_mod, b_mod).reshape(DEPTH, COND_ROWS, 6, D_MODEL)

    w_in_b = w_in.astype(bf16)
    w_br_b = w_br.astype(bf16)
    w_out_b = w_out.astype(bf16)
    w_ff1_b = w_ff1.astype(bf16)
    w_ff2_b = w_ff2.astype(bf16)
    norm1_r = norm1.reshape(DEPTH, 1, D_MODEL)
    norm2_r = norm2.reshape(DEPTH, 1, D_MODEL)
    rope = _rope_tables(SAMPLE.seq)
    log_gamma = -jax.nn.softplus(-ret_decay_logit)
    lru_sp = jax.nn.softplus(-lru_lambda)
    kvw = ATT_KV_HEADS * HEAD_DIM

    new_states = []
    for l in range(DEPTH):
        ssm_a, ssm_bd, ssm_cd = _ssm_params(ssm_a_re[l], ssm_a_im[l], ssm_log_dt[l], ssm_b_re[l], ssm_b_im[l],
                                            ssm_c_re[l], ssm_c_im[l])
        prm = {
            "ret_lg": jnp.broadcast_to(log_gamma[l].T[:, :, None], (RET_HEADS, 2, 128)),
            "ret_gn": ret_gn[l].reshape(1, RET_HEADS * 128),
            "lru": {"cw": lru_conv_w[l], "cb": lru_conv_b[l].reshape(1, LRU_WIDTH),
                    "wa": lru_wa[l].astype(bf16), "ba": lru_ba[l].reshape(2, 1, LRU_WIDTH),
                    "wx": lru_wx[l].astype(bf16), "bx": lru_bx[l].reshape(2, 1, LRU_WIDTH),
                    "sp": lru_sp[l].reshape(2, 1, LRU_WIDTH)},
            "att_qw": att_q_norm[l].reshape(1, HEAD_DIM), "att_kw": att_k_norm[l].reshape(1, HEAD_DIM),
            "att_sink": att_sink[l], "rope": rope,
            "ssm": {"a": ssm_a, "bd": ssm_bd, "cd": ssm_cd, "d": ssm_d[l].reshape(1, SSM_WIDTH),
                    "wglu": ssm_w_glu[l].astype(bf16), "bglu": ssm_b_glu[l].reshape(1, SSM_WIDTH)},
        }
        states = {
            PROMPT: (None, None, None, None),
            SAMPLE: (state_ret[:, l], state_lru[:, l],
                     jnp.concatenate([state_ssm_re[:, l].reshape(SAMPLE.nb, 2, SSM_N),
                                      state_ssm_im[:, l].reshape(SAMPLE.nb, 2, SSM_N)], axis=-1),
                     (cache_attn_k[:, l].reshape(SAMPLE.nb, PAST_LEN, kvw),
                      cache_attn_v[:, l].reshape(SAMPLE.nb, PAST_LEN, kvw))),
        }
        for grp in (PROMPT, SAMPLE):
            x = xs[grp]
            z, h = _proj_in(x, mod, norm1_r, w_in_b, l, grp)
            branches, st = _mixers(z, grp, l, prm, states[grp])
            if grp.ctx:
                new_states.append(st)
            merged = _merge(h, branches, w_in_b, w_br_b, l, grp)
            x = _out_proj(x, mod, merged, w_out_b, l, grp)
            xs[grp] = _ffn(x, mod, norm2_r, w_ff1_b, w_ff2_b, l, grp)

    y_prompt = xs[PROMPT].reshape(PROMPT.nb, PROMPT.seq, D_MODEL)
    y_sample = xs[SAMPLE].reshape(SAMPLE.nb, SAMPLE.seq, D_MODEL)
    return (y_prompt, y_sample) + tuple(jnp.stack([st[i] for st in new_states], axis=1) for i in range(6))
```

```python
import functools
from typing import NamedTuple

import jax
import jax.numpy as jnp
from jax import lax
from jax.experimental import pallas as pl
from jax.experimental.pallas import tpu as pltpu

f32 = jnp.float32
bf16 = jnp.bfloat16

D_MODEL = 2048
DEPTH = 4
PAST_LEN = 256
GRID_W = 64
CHUNK = 128
EPS = 1e-6
NEG_INF = -1e30
RET_HEADS = 4
RET_DK = 128
LRU_WIDTH = 512
LRU_BLOCKS = 4
LRU_C = 8.0
CONV_W = 4
ATT_Q_HEADS = 4
ATT_KV_HEADS = 2
HEAD_DIM = 128
WINDOW = 128
ROPE_BASE = 10000.0
SSM_WIDTH = 512
SSM_GROUP = 16
SSM_GROUPS = 32
SSM_STATE = 64
SSM_N = SSM_GROUPS * SSM_STATE
SSM_CB = 512
SSM_CI = SSM_CB // SSM_STATE * SSM_GROUP
N_BRANCH = 4
BRANCH_WIDTH = 512
D_FF = 4 * D_MODEL
RQ, RK, RV, RG, LX, LG, AQ, AK, AV, SU, ZW = 0, 512, 1024, 1536, 2048, 2560, 3072, 3584, 3840, 4096, 4608
SUB = 8
COND_ROWS = 16
VMEM_LIMIT = 56 * 1024 * 1024

TM = 1024
LRU_TC = 128
SSM_TC = 128
ROW_BLK = 16


class Group(NamedTuple):
    nb: int
    seq: int
    ctx: bool

    @property
    def n(self):
        return self.nb * self.seq


PROMPT = Group(32, 256, True)
SAMPLE = Group(8, 2048, False)
CTX_ROW = SAMPLE.nb


def _cp(n_axes, vmem=VMEM_LIMIT):
    return pltpu.CompilerParams(dimension_semantics=("arbitrary",) * n_axes, vmem_limit_bytes=vmem)


def _sigmoid(x):
    return 0.5 * jnp.tanh(0.5 * x) + 0.5


def _mod_row(grp, i):
    return CTX_ROW if grp.ctx else i // (grp.seq // TM)


def _modnorm_rows(x_ref, nw, shift, scale, h_ref, r_scr, gs_scr):
    tm, d = x_ref.shape
    lane_tiles = d // 128

    def ssq_body(r, carry):
        rows = pl.ds(pl.multiple_of(r * SUB, SUB), SUB)
        acc = None
        for t in range(lane_tiles):
            v = x_ref[rows, t * 128:(t + 1) * 128]
            acc = v * v if acc is None else acc + v * v
        r_scr[rows, :] = acc
        return carry

    lax.fori_loop(0, tm // SUB, ssq_body, 0, unroll=4)
    ssq = jnp.sum(r_scr[...], axis=-1, keepdims=True)
    r_scr[...] = jnp.broadcast_to(lax.rsqrt(ssq * (1.0 / d) + EPS), (tm, 128))
    gs_scr[0] = jnp.broadcast_to(nw * (1.0 + scale), (SUB, d))
    gs_scr[1] = jnp.broadcast_to(shift, (SUB, d))

    def out_body(r, carry):
        r0 = pl.multiple_of(r * ROW_BLK, ROW_BLK)
        halves = [pl.ds(pl.multiple_of(r0 + k * SUB, SUB), SUB) for k in range(ROW_BLK // SUB)]
        invs = [r_scr[rows, :] for rows in halves]
        for t in range(lane_tiles):
            cols = slice(t * 128, (t + 1) * 128)
            gain = gs_scr[0, :, cols]
            shf = gs_scr[1, :, cols]
            parts = [x_ref[rows, cols] * inv * gain + shf for rows, inv in zip(halves, invs)]
            h_ref[pl.ds(r0, ROW_BLK), cols] = jnp.concatenate(parts, axis=0).astype(bf16)
        return carry

    lax.fori_loop(0, tm // ROW_BLK, out_body, 0, unroll=2)


def _mod_kernel(c_ref, w_ref, b_ref, o_ref):
    c = c_ref[...]
    s = (c * _sigmoid(c)).astype(bf16)
    o_ref[...] = jnp.dot(s, w_ref[...].astype(bf16), preferred_element_type=f32) + b_ref[...]


def _modulation(cond, w_mod, b_mod):
    tn = 1024
    n6 = 6 * D_MODEL
    return pl.pallas_call(
        _mod_kernel,
        grid=(DEPTH, n6 // tn),
        in_specs=[
            pl.BlockSpec((COND_ROWS, D_MODEL), lambda l, j: (0, 0)),
            pl.BlockSpec((None, D_MODEL, tn), lambda l, j: (l, 0, j)),
            pl.BlockSpec((None, 1, tn), lambda l, j: (l, 0, j)),
        ],
        out_specs=pl.BlockSpec((None, COND_ROWS, tn), lambda l, j: (l, 0, j)),
        out_shape=jax.ShapeDtypeStruct((DEPTH, COND_ROWS, n6), f32),
        compiler_params=_cp(2),
        name="modulation",
    )(cond, w_mod, b_mod.reshape(DEPTH, 1, n6))


def _proj_kernel(x_ref, mod_ref, nw_ref, w_ref, o_ref, h_ref, r_scr, gs_scr):
    @pl.when(pl.program_id(1) == 0)
    def _():
        _modnorm_rows(x_ref, nw_ref[...], mod_ref[0:1, :], mod_ref[1:2, :], h_ref, r_scr, gs_scr)

    o_ref[...] = jnp.dot(h_ref[...], w_ref[...], preferred_element_type=f32)


def _proj_in(x, mod, nw, w_in, layer, grp):
    tn = 1536
    return pl.pallas_call(
        _proj_kernel,
        grid=(grp.n // TM, ZW // tn),
        in_specs=[
            pl.BlockSpec((TM, D_MODEL), lambda i, j: (i, 0)),
            pl.BlockSpec((None, None, 6, D_MODEL), lambda i, j: (layer, _mod_row(grp, i), 0, 0)),
            pl.BlockSpec((None, 1, D_MODEL), lambda i, j: (layer, 0, 0)),
            pl.BlockSpec((None, D_MODEL, tn), lambda i, j: (layer, 0, j)),
        ],
        out_specs=[pl.BlockSpec((TM, tn), lambda i, j: (i, j)),
                   pl.BlockSpec((TM, D_MODEL), lambda i, j: (i, 0))],
        out_shape=[jax.ShapeDtypeStruct((grp.n, ZW), f32), jax.ShapeDtypeStruct((grp.n, D_MODEL), bf16)],
        scratch_shapes=[pltpu.VMEM((TM, 128), f32), pltpu.VMEM((2, SUB, D_MODEL), f32)],
        compiler_params=_cp(2),
        name="proj_in",
    )(x, mod, nw, w_in)


def _ret_kernel(*refs, nc, has_s0, emit_state):
    q_ref, k_ref, v_ref, g_ref, lg_ref, gn_ref = refs[:6]
    pos = 6
    s0_ref = None
    if has_s0:
        s0_ref = refs[pos]
        pos += 1
    o_ref = refs[pos]
    pos += 1
    sout_ref = None
    if emit_state:
        sout_ref = refs[pos]
        pos += 1
    w_scr, kv_scr = refs[pos], refs[pos + 1]

    C = CHUNK
    H = RET_HEADS
    DEC, W_IN_F, W_IN_B, W_END_F, W_END_B = range(5)
    tn_dims = (((0,), (0,)), ((), ()))
    nt_dims = (((1,), (1,)), ((), ()))

    @pl.when(pl.program_id(0) == 0)
    def _():
        ii = lax.broadcasted_iota(jnp.int32, (C, C), 0).astype(f32)
        jj = lax.broadcasted_iota(jnp.int32, (C, C), 1).astype(f32)
        rel = ii - jj
        for h in range(H):
            lgf = lg_ref[h, 0:1, :]
            lgb = lg_ref[h, 1:2, :]
            w_scr[h, DEC] = (jnp.where(rel >= 0, jnp.exp(jnp.maximum(rel, 0.0) * lgf), 0.0)
                             + jnp.where(rel <= 0, jnp.exp(jnp.maximum(-rel, 0.0) * lgb), 0.0))
            w_scr[h, W_IN_F] = jnp.exp((ii + 1.0) * lgf)
            w_scr[h, W_IN_B] = jnp.exp((C - ii) * lgb)
            w_scr[h, W_END_F] = jnp.exp((C - 1.0 - ii) * lgf)
            w_scr[h, W_END_B] = jnp.exp(ii * lgb)

    def kv_body(n, carry):
        r = pl.multiple_of(n * C, C)
        for h in range(H):
            cols = slice(h * C, (h + 1) * C)
            k = k_ref[pl.ds(r, C), cols]
            vb = v_ref[pl.ds(r, C), cols].astype(bf16)
            kw = jnp.concatenate([k * w_scr[h, W_END_F], k * w_scr[h, W_END_B]], axis=1).astype(bf16)
            kv = lax.dot_general(kw, vb, tn_dims, preferred_element_type=f32)
            kv_scr[0, n * H + h] = kv[:C]
            kv_scr[1, n * H + h] = kv[C:]
        return carry

    lax.fori_loop(0, nc, kv_body, 0)

    for h in range(H):
        cd_f = jnp.exp(C * lg_ref[h, 0:1, :])
        cd_b = jnp.exp(C * lg_ref[h, 1:2, :])

        def fwd_body(n, s, h=h, cd_f=cd_f):
            kv = kv_scr[0, n * H + h]
            kv_scr[0, n * H + h] = s
            return cd_f * s + kv

        def bwd_body(i, s, h=h, cd_b=cd_b):
            n = nc - 1 - i
            kv = kv_scr[1, n * H + h]
            kv_scr[1, n * H + h] = s
            return cd_b * s + kv

        s_f = lax.fori_loop(0, nc, fwd_body, s0_ref[0, h] if has_s0 else jnp.zeros((C, C), f32))
        s_b = lax.fori_loop(0, nc, bwd_body, s0_ref[1, h] if has_s0 else jnp.zeros((C, C), f32))
        if emit_state:
            sout_ref[0, h] = s_f
            sout_ref[1, h] = s_b

    def out_body(n, carry):
        r = pl.multiple_of(n * C, C)
        for h in range(H):
            cols = slice(h * C, (h + 1) * C)
            q = q_ref[pl.ds(r, C), cols] * (RET_DK ** -0.5)
            kb = k_ref[pl.ds(r, C), cols].astype(bf16)
            vb = v_ref[pl.ds(r, C), cols].astype(bf16)
            sc = lax.dot_general(q.astype(bf16), kb, nt_dims, preferred_element_type=f32) * w_scr[h, DEC]
            lhs = jnp.concatenate([sc, q * w_scr[h, W_IN_F], q * w_scr[h, W_IN_B]], axis=1).astype(bf16)
            rhs = jnp.concatenate([vb, kv_scr[0, n * H + h].astype(bf16), kv_scr[1, n * H + h].astype(bf16)], axis=0)
            o = jnp.dot(lhs, rhs, preferred_element_type=f32)
            mu = jnp.mean(o, axis=-1, keepdims=True)
            d = o - mu
            var = jnp.mean(d * d, axis=-1, keepdims=True)
            on = d * lax.rsqrt(var + EPS) * gn_ref[:, cols]
            g = g_ref[pl.ds(r, C), cols]
            o_ref[pl.ds(r, C), cols] = (g * _sigmoid(g) * on).astype(bf16)
        return carry

    lax.fori_loop(0, nc, out_body, 0, unroll=2)


def _retention(z, lg, gn, s0, grp):
    seq = grp.seq
    W = RET_HEADS * 128
    in_specs = [
        pl.BlockSpec((seq, W), lambda b: (b, RQ // W)),
        pl.BlockSpec((seq, W), lambda b: (b, RK // W)),
        pl.BlockSpec((seq, W), lambda b: (b, RV // W)),
        pl.BlockSpec((seq, W), lambda b: (b, RG // W)),
        pl.BlockSpec((RET_HEADS, 2, 128), lambda b: (0, 0, 0)),
        pl.BlockSpec((1, W), lambda b: (0, 0)),
    ]
    args = [z, z, z, z, lg, gn]
    st_spec = pl.BlockSpec((None, 2, RET_HEADS, 128, 128), lambda b: (b, 0, 0, 0, 0))
    if s0 is not None:
        in_specs.append(st_spec)
        args.append(s0)
    out_specs = [pl.BlockSpec((seq, W), lambda b: (b, 0))]
    out_shape = [jax.ShapeDtypeStruct((grp.n, W), bf16)]
    if grp.ctx:
        out_specs.append(st_spec)
        out_shape.append(jax.ShapeDtypeStruct((grp.nb, 2, RET_HEADS, 128, 128), f32))
    nc = seq // CHUNK
    res = pl.pallas_call(
        functools.partial(_ret_kernel, nc=nc, has_s0=s0 is not None, emit_state=grp.ctx),
        grid=(grp.nb,),
        in_specs=in_specs,
        out_specs=out_specs,
        out_shape=out_shape,
        scratch_shapes=[pltpu.VMEM((RET_HEADS, 5, 128, 128), f32),
                        pltpu.VMEM((2, nc * RET_HEADS, 128, 128), f32)],
        compiler_params=_cp(1),
        name="retention",
    )(*args)
    return res if grp.ctx else (res[0], None)


def _gather_tm(src_ref, t0, nt, dst_ref, row0):
    w = src_ref.shape[-1]

    def body(t, carry):
        v = src_ref[:, pl.ds(t0 + t, 1), :]
        dst_ref[pl.ds(pl.multiple_of(row0 + t * SUB, SUB), SUB), :] = v.reshape(SUB, w)
        return carry

    lax.fori_loop(0, nt, body, 0, unroll=4)


def _scatter_bm(src_ref, nt, dst_ref):
    w = src_ref.shape[-1]

    def body(t, carry):
        v = src_ref[pl.ds(pl.multiple_of(t * SUB, SUB), SUB), :]
        dst_ref[:, pl.ds(t, 1), :] = v.reshape(SUB, 1, w)
        return carry

    lax.fori_loop(0, nt, body, 0, unroll=4)


def _lru_kernel(*refs, tc, nch, direction, has_s0):
    xprev_ref, xcur_ref, xnext_ref, cw_ref, cb_ref, wa_ref, ba_ref, wx_ref, bx_ref, sp_ref = refs[:10]
    pos = 10
    s0_ref = None
    if has_s0:
        s0_ref = refs[pos]
        pos += 1
    if direction == 1:
        gate_ref, hf_ref = refs[pos], refs[pos + 1]
        pos += 2
    out_ref, fin_ref = refs[pos], refs[pos + 1]
    xe_scr, a_scr, b_scr, h_scr = refs[pos + 2: pos + 6]
    if direction == 1:
        g_scr, y_scr = refs[pos + 6], refs[pos + 7]

    R = tc * SUB
    W = LRU_WIDTH
    j = pl.program_id(1)
    c = j if direction == 0 else nch - 1 - j

    @pl.when(j == 0)
    def _():
        h_scr[...] = s0_ref[...] if has_s0 else jnp.zeros((SUB, W), f32)

    for t in range(2):
        v = xprev_ref[:, SUB - 2 + t:SUB - 1 + t, :].reshape(SUB, W)
        xe_scr[t * SUB:(t + 1) * SUB, :] = jnp.where(c > 0, v, 0.0)
    _gather_tm(xcur_ref, 0, tc, xe_scr, 2 * SUB)
    xe_scr[2 * SUB + R:3 * SUB + R, :] = jnp.where(c < nch - 1, xnext_ref[:, 0:1, :].reshape(SUB, W), 0.0)
    if direction == 1:
        _gather_tm(gate_ref, 0, tc, g_scr, 0)

    xc = cb_ref[...] + xe_scr[0:R, :] * cw_ref[0:1, :]
    for t in range(1, CONV_W):
        xc = xc + xe_scr[t * SUB:t * SUB + R, :] * cw_ref[t:t + 1, :]

    bd = W // LRU_BLOCKS
    for n in range(LRU_BLOCKS):
        sl = slice(n * bd, (n + 1) * bd)
        xs = xc[:, sl]
        xb = xs.astype(bf16)
        r = _sigmoid(jnp.dot(xb, wa_ref[n], preferred_element_type=f32) + ba_ref[:, sl])
        i = _sigmoid(jnp.dot(xb, wx_ref[n], preferred_element_type=f32) + bx_ref[:, sl])
        log_a = -LRU_C * r * sp_ref[:, sl]
        a = jnp.exp(log_a)
        a_scr[:, sl] = a
        b_scr[:, sl] = jnp.sqrt(-jnp.tanh(log_a) * (a * a + 1.0)) * (i * xs)

    def step(s, h):
        t = s if direction == 0 else tc - 1 - s
        r0 = pl.multiple_of(t * SUB, SUB)
        h = a_scr[pl.ds(r0, SUB), :] * h + b_scr[pl.ds(r0, SUB), :]
        b_scr[pl.ds(r0, SUB), :] = h
        return h

    h = lax.fori_loop(0, tc, step, h_scr[...], unroll=8)
    h_scr[...] = h
    fin_ref[...] = h
    if direction == 0:
        out_ref[...] = b_scr[...]
    else:
        b_scr[...] = jax.nn.gelu(g_scr[...]) * (hf_ref[...] + b_scr[...])
        _scatter_bm(b_scr, tc, y_scr)
        out_ref[...] = y_scr[...].astype(bf16)


def _lru_dir(z3, hf_tm, s0, prm, grp, direction):
    nb, seq = grp.nb, grp.seq
    nbg = nb // SUB
    tc = min(LRU_TC, seq)
    nch = seq // tc
    R = tc * SUB
    W = LRU_WIDTH

    def cidx(j):
        return j if direction == 0 else nch - 1 - j

    in_specs = [
        pl.BlockSpec((SUB, SUB, W), lambda g, j: (g, jnp.maximum(cidx(j) * (tc // SUB) - 1, 0), LX // W)),
        pl.BlockSpec((SUB, tc, W), lambda g, j: (g, cidx(j), LX // W)),
        pl.BlockSpec((SUB, SUB, W), lambda g, j: (g, jnp.minimum((cidx(j) + 1) * (tc // SUB), seq // SUB - 1), LX // W)),
        pl.BlockSpec((CONV_W, W), lambda g, j: (0, 0)),
        pl.BlockSpec((1, W), lambda g, j: (0, 0)),
        pl.BlockSpec((LRU_BLOCKS, 128, 128), lambda g, j: (0, 0, 0)),
        pl.BlockSpec((1, W), lambda g, j: (0, 0)),
        pl.BlockSpec((LRU_BLOCKS, 128, 128), lambda g, j: (0, 0, 0)),
        pl.BlockSpec((1, W), lambda g, j: (0, 0)),
        pl.BlockSpec((1, W), lambda g, j: (0, 0)),
    ]
    args = [z3, z3, z3, prm["cw"], prm["cb"], prm["wa"][direction], prm["ba"][direction],
            prm["wx"][direction], prm["bx"][direction], prm["sp"][direction]]
    st_spec = pl.BlockSpec((SUB, W), lambda g, j: (g, 0))
    if s0 is not None:
        in_specs.append(st_spec)
        args.append(s0)
    tm_spec = pl.BlockSpec((None, R, W), lambda g, j: (g, cidx(j), 0))
    scratch = [pltpu.VMEM((R + 3 * SUB, W), f32), pltpu.VMEM((R, W), f32),
               pltpu.VMEM((R, W), f32), pltpu.VMEM((SUB, W), f32)]
    if direction == 0:
        out_spec = tm_spec
        out_sds = jax.ShapeDtypeStruct((nbg, seq * SUB, W), f32)
    else:
        in_specs += [pl.BlockSpec((SUB, tc, W), lambda g, j: (g, cidx(j), LG // W)), tm_spec]
        args += [z3, hf_tm]
        out_spec = pl.BlockSpec((SUB, tc, W), lambda g, j: (g, cidx(j), 0))
        out_sds = jax.ShapeDtypeStruct((nb, seq, W), bf16)
        scratch += [pltpu.VMEM((R, W), f32), pltpu.VMEM((SUB, tc, W), f32)]
    return pl.pallas_call(
        functools.partial(_lru_kernel, tc=tc, nch=nch, direction=direction, has_s0=s0 is not None),
        grid=(nbg, nch),
        in_specs=in_specs,
        out_specs=[out_spec, st_spec],
        out_shape=[out_sds, jax.ShapeDtypeStruct((nb, W), f32)],
        scratch_shapes=scratch,
        compiler_params=_cp(2),
        name="rglru_dir%d" % direction,
    )(*args)


def _s5_kernel(*refs, tc, nch, direction, has_s0):
    u_ref, bd_ref, a_ref, cd_ref = refs[:4]
    pos = 4
    s0_ref = None
    if has_s0:
        s0_ref = refs[pos]
        pos += 1
    if direction == 1:
        yf_ref, dv_ref, wg_ref, bg_ref = refs[pos:pos + 4]
        pos += 4
    out_ref, fin_ref = refs[pos], refs[pos + 1]
    u_scr, ub_scr, hs_scr, h_scr = refs[pos + 2: pos + 6]
    if direction == 1:
        y_scr = refs[pos + 6]

    j = pl.program_id(1)

    @pl.when(j == 0)
    def _():
        h_scr[...] = s0_ref[...] if has_s0 else jnp.zeros((SUB, 2 * SSM_N), f32)

    _gather_tm(u_ref, 0, tc, u_scr, 0)
    ub_scr[...] = u_scr[...].astype(bf16)

    cbw = SSM_CB
    for cb in range(SSM_N // cbw):
        cre = slice(cb * cbw, (cb + 1) * cbw)
        cim = slice(SSM_N + cb * cbw, SSM_N + (cb + 1) * cbw)
        cin = slice(cb * SSM_CI, (cb + 1) * SSM_CI)
        hs_scr[:, cre] = jnp.dot(ub_scr[:, cin], bd_ref[0, cb], preferred_element_type=f32)
        hs_scr[:, cim] = jnp.dot(ub_scr[:, cin], bd_ref[1, cb], preferred_element_type=f32)
        a_re = jnp.broadcast_to(a_ref[0:1, cre], (SUB, cbw))
        a_im = jnp.broadcast_to(a_ref[1:2, cre], (SUB, cbw))

        def step(s, carry):
            hr, hi = carry
            t = s if direction == 0 else tc - 1 - s
            r0 = pl.multiple_of(t * SUB, SUB)
            nr = a_re * hr - a_im * hi + hs_scr[pl.ds(r0, SUB), cre]
            ni = a_re * hi + a_im * hr + hs_scr[pl.ds(r0, SUB), cim]
            hs_scr[pl.ds(r0, SUB), cre] = nr
            hs_scr[pl.ds(r0, SUB), cim] = ni
            return nr, ni

        hr, hi = lax.fori_loop(0, tc, step, (h_scr[:, cre], h_scr[:, cim]), unroll=True)
        h_scr[:, cre] = hr
        h_scr[:, cim] = hi

    fin_ref[...] = h_scr[...]
    y_parts = []
    for cb in range(SSM_N // cbw):
        cre = slice(cb * cbw, (cb + 1) * cbw)
        cim = slice(SSM_N + cb * cbw, SSM_N + (cb + 1) * cbw)
        y_parts.append(jnp.dot(hs_scr[:, cre].astype(bf16), cd_ref[0, cb], preferred_element_type=f32)
                       + jnp.dot(hs_scr[:, cim].astype(bf16), cd_ref[1, cb], preferred_element_type=f32))
    y = jnp.concatenate(y_parts, axis=1)
    if direction == 0:
        out_ref[...] = y
    else:
        yy = jax.nn.gelu(yf_ref[...] + y + dv_ref[...] * u_scr[...])
        gl = jnp.dot(yy.astype(bf16), wg_ref[...], preferred_element_type=f32) + bg_ref[...]
        u_scr[...] = yy * _sigmoid(gl)
        _scatter_bm(u_scr, tc, y_scr)
        out_ref[...] = y_scr[...].astype(bf16)


def _s5_dir(z3, yf_tm, s0, prm, grp, direction):
    nb, seq = grp.nb, grp.seq
    nbg = nb // SUB
    tc = min(SSM_TC, seq)
    nch = seq // tc
    R = tc * SUB
    W = SSM_WIDTH
    ncb = SSM_N // SSM_CB

    def cidx(j):
        return j if direction == 0 else nch - 1 - j

    tm_spec = pl.BlockSpec((None, R, W), lambda g, j: (g, cidx(j), 0))
    st_spec = pl.BlockSpec((SUB, 2 * SSM_N), lambda g, j: (g, 0))
    in_specs = [
        pl.BlockSpec((SUB, tc, W), lambda g, j: (g, cidx(j), SU // W)),
        pl.BlockSpec((2, ncb, SSM_CI, SSM_CB), lambda g, j: (0, 0, 0, 0)),
        pl.BlockSpec((2, SSM_N), lambda g, j: (0, 0)),
        pl.BlockSpec((2, ncb, SSM_CB, SSM_CI), lambda g, j: (0, 0, 0, 0)),
    ]
    args = [z3, prm["bd"][direction], prm["a"][direction], prm["cd"][direction]]
    if s0 is not None:
        in_specs.append(st_spec)
        args.append(s0)
    scratch = [pltpu.VMEM((R, W), f32), pltpu.VMEM((R, W), bf16), pltpu.VMEM((R, 2 * SSM_N), f32),
               pltpu.VMEM((SUB, 2 * SSM_N), f32)]
    if direction == 0:
        out_spec = tm_spec
        out_sds = jax.ShapeDtypeStruct((nbg, seq * SUB, W), f32)
    else:
        in_specs += [tm_spec, pl.BlockSpec((1, W), lambda g, j: (0, 0)),
                     pl.BlockSpec((W, W), lambda g, j: (0, 0)), pl.BlockSpec((1, W), lambda g, j: (0, 0))]
        args += [yf_tm, prm["d"], prm["wglu"], prm["bglu"]]
        out_spec = pl.BlockSpec((SUB, tc, W), lambda g, j: (g, cidx(j), 0))
        out_sds = jax.ShapeDtypeStruct((nb, seq, W), bf16)
        scratch.append(pltpu.VMEM((SUB, tc, W), f32))
    return pl.pallas_call(
        functools.partial(_s5_kernel, tc=tc, nch=nch, direction=direction, has_s0=s0 is not None),
        grid=(nbg, nch),
        in_specs=in_specs,
        out_specs=[out_spec, st_spec],
        out_shape=[out_sds, jax.ShapeDtypeStruct((nb, 2 * SSM_N), f32)],
        scratch_shapes=scratch,
        compiler_params=_cp(2),
        name="s5_dir%d" % direction,
    )(*args)


def _rope(x, cos, sin_signed):
    return x * cos + pltpu.roll(x, HEAD_DIM // 2, 1) * sin_signed


def _unit_rms(x, w):
    return x * lax.rsqrt(jnp.mean(x * x, axis=-1, keepdims=True) + EPS) * w


def _attn_kernel(*refs, seq, latent):
    q_ref, k_ref, v_ref, qw_ref, kw_ref, sink_ref = refs[:6]
    pos = 6
    if latent:
        cos_ref, sin_ref, ck_ref, cv_ref = refs[pos:pos + 4]
        pos += 4
    o_ref = refs[pos]
    pos += 1
    if not latent:
        kout_ref, vout_ref = refs[pos], refs[pos + 1]
        pos += 2
    kb_scr, vb_scr = refs[pos], refs[pos + 1]
    pos += 2
    if latent:
        ckb_scr, cvb_scr = refs[pos], refs[pos + 1]

    HD = HEAD_DIM
    qb = pl.program_id(1)

    @pl.when(qb == 0)
    def _():
        for hk in range(ATT_KV_HEADS):
            sl = slice(hk * HD, (hk + 1) * HD)
            kn = _unit_rms(k_ref[:, sl], kw_ref[...])
            if latent:
                kn = _rope(kn, cos_ref[...], sin_ref[...])
            else:
                kout_ref[:, sl] = kn
            kb_scr[:, sl] = kn.astype(bf16)
        v = v_ref[...]
        vb_scr[...] = v.astype(bf16)
        if latent:
            ckb_scr[...] = ck_ref[...].astype(bf16)
            cvb_scr[...] = cv_ref[...].astype(bf16)
        else:
            vout_ref[...] = v

    q0 = pl.multiple_of(qb * CHUNK, CHUNK)
    if latent:
        win = 3 * CHUNK
        start = pl.multiple_of(jnp.clip(q0 - CHUNK, 0, seq - win), CHUNK)
        qpos = q0 + (lax.broadcasted_iota(jnp.int32, (2 * CHUNK, win), 0) & (CHUNK - 1))
        kpos = start + lax.broadcasted_iota(jnp.int32, (2 * CHUNK, win), 1)
        band = jnp.abs(qpos - kpos) <= WINDOW
        cos_q = cos_ref[pl.ds(q0, CHUNK), :]
        sin_q = sin_ref[pl.ds(q0, CHUNK), :]
    else:
        win = seq
        start = 0
    scale = HD ** -0.5
    nt_dims = (((1,), (1,)), ((), ()))
    row = lax.broadcasted_iota(jnp.int32, (2 * CHUNK, 1), 0)
    for hk in range(ATT_KV_HEADS):
        sl = slice(hk * HD, (hk + 1) * HD)
        qs = []
        for g in range(2):
            hq = hk * 2 + g
            qn = _unit_rms(q_ref[:, hq * HD:(hq + 1) * HD], qw_ref[...])
            if latent:
                qn = _rope(qn, cos_q, sin_q)
            qs.append(qn.astype(bf16))
        qg = jnp.concatenate(qs, axis=0)
        snk = jnp.where(row < CHUNK, sink_ref[hk * 2], sink_ref[hk * 2 + 1])
        s1 = lax.dot_general(qg, kb_scr[pl.ds(start, win), sl], nt_dims, preferred_element_type=f32) * scale
        if latent:
            s1 = jnp.where(band, s1, NEG_INF)
        m = jnp.maximum(jnp.max(s1, axis=-1, keepdims=True), snk)
        if latent:
            s2 = lax.dot_general(qg, ckb_scr[:, sl], nt_dims, preferred_element_type=f32) * scale
            m = jnp.maximum(m, jnp.max(s2, axis=-1, keepdims=True))
            p2 = jnp.exp(s2 - m)
        p1 = jnp.exp(s1 - m)
        den = jnp.sum(p1, axis=-1, keepdims=True) + jnp.exp(snk - m)
        if latent:
            den = den + jnp.sum(p2, axis=-1, keepdims=True)
        o = jnp.dot((p1 / den).astype(bf16), vb_scr[pl.ds(start, win), sl], preferred_element_type=f32)
        if latent:
            o += jnp.dot((p2 / den).astype(bf16), cvb_scr[:, sl], preferred_element_type=f32)
        for g in range(2):
            hq = hk * 2 + g
            o_ref[:, hq * HD:(hq + 1) * HD] = o[g * CHUNK:(g + 1) * CHUNK].astype(bf16)


def _attention(z, qw, kw, sink, rope, cache, grp):
    nb, seq = grp.nb, grp.seq
    latent = not grp.ctx
    nqb = seq // CHUNK
    kvw = ATT_KV_HEADS * HEAD_DIM
    in_specs = [
        pl.BlockSpec((CHUNK, 512), lambda b, i: (b * nqb + i, AQ // 512)),
        pl.BlockSpec((seq, kvw), lambda b, i: (b, AK // kvw)),
        pl.BlockSpec((seq, kvw), lambda b, i: (b, AV // kvw)),
        pl.BlockSpec((1, HEAD_DIM), lambda b, i: (0, 0)),
        pl.BlockSpec((1, HEAD_DIM), lambda b, i: (0, 0)),
        pl.BlockSpec(memory_space=pltpu.SMEM),
    ]
    args = [z, z, z, qw, kw, sink]
    scratch = [pltpu.VMEM((seq, kvw), bf16), pltpu.VMEM((seq, kvw), bf16)]
    out_specs = [pl.BlockSpec((CHUNK, 512), lambda b, i: (b * nqb + i, 0))]
    out_shape = [jax.ShapeDtypeStruct((nb * seq, ATT_Q_HEADS * HEAD_DIM), bf16)]
    if latent:
        in_specs += [pl.BlockSpec((seq, HEAD_DIM), lambda b, i: (0, 0)),
                     pl.BlockSpec((seq, HEAD_DIM), lambda b, i: (0, 0)),
                     pl.BlockSpec((None, PAST_LEN, kvw), lambda b, i: (b, 0, 0)),
                     pl.BlockSpec((None, PAST_LEN, kvw), lambda b, i: (b, 0, 0))]
        args += [rope[0], rope[1], cache[0], cache[1]]
        scratch += [pltpu.VMEM((PAST_LEN, kvw), bf16), pltpu.VMEM((PAST_LEN, kvw), bf16)]
    else:
        kv_out = pl.BlockSpec((seq, kvw), lambda b, i: (b, 0))
        out_specs += [kv_out, kv_out]
        out_shape += [jax.ShapeDtypeStruct((nb * seq, kvw), f32)] * 2
    return pl.pallas_call(
        functools.partial(_attn_kernel, seq=seq, latent=latent),
        grid=(nb, nqb),
        in_specs=in_specs,
        out_specs=out_specs,
        out_shape=out_shape,
        scratch_shapes=scratch,
        compiler_params=_cp(2),
        name="attention_lat" if latent else "attention_ctx",
    )(*args)


def _merge_kernel(h_ref, o0, o1, o2, o3, g0, g1, g2, g3, wbr_ref, out_ref):
    h = h_ref[...]
    acc = None
    for b, (o_ref, g_ref) in enumerate(((o0, g0), (o1, g1), (o2, g2), (o3, g3))):
        gate = _sigmoid(jnp.dot(h, g_ref[...], preferred_element_type=f32))
        t = gate * jnp.dot(o_ref[...], wbr_ref[b], preferred_element_type=f32)
        acc = t if acc is None else acc + t
    out_ref[...] = acc.astype(bf16)


def _merge(h, branches, w_in, w_br, layer, grp):
    tn = 256
    g0 = ZW // tn
    gb = D_MODEL // tn
    o_spec = pl.BlockSpec((TM, BRANCH_WIDTH), lambda i, j: (i, 0))

    def gate_spec(b):
        return pl.BlockSpec((None, D_MODEL, tn), lambda i, j: (layer, 0, g0 + b * gb + j))

    return pl.pallas_call(
        _merge_kernel,
        grid=(grp.n // TM, D_MODEL // tn),
        in_specs=[
            pl.BlockSpec((TM, D_MODEL), lambda i, j: (i, 0)),
            o_spec, o_spec, o_spec, o_spec,
            gate_spec(0), gate_spec(1), gate_spec(2), gate_spec(3),
            pl.BlockSpec((None, N_BRANCH, BRANCH_WIDTH, tn), lambda i, j: (layer, 0, 0, j)),
        ],
        out_specs=pl.BlockSpec((TM, tn), lambda i, j: (i, j)),
        out_shape=jax.ShapeDtypeStruct((grp.n, D_MODEL), bf16),
        compiler_params=_cp(2),
        name="merge",
    )(h, *branches, w_in, w_in, w_in, w_in, w_br)


def _outproj_kernel(x_ref, mod_ref, m_ref, w_ref, o_ref):
    o_ref[...] = x_ref[...] + mod_ref[2:3, :] * jnp.dot(m_ref[...], w_ref[...], preferred_element_type=f32)


def _out_proj(x, mod, merged, w_out, layer, grp):
    tn = 1024
    return pl.pallas_call(
        _outproj_kernel,
        grid=(grp.n // TM, D_MODEL // tn),
        in_specs=[
            pl.BlockSpec((TM, tn), lambda i, j: (i, j)),
            pl.BlockSpec((None, None, 6, tn), lambda i, j: (layer, _mod_row(grp, i), 0, j)),
            pl.BlockSpec((TM, D_MODEL), lambda i, j: (i, 0)),
            pl.BlockSpec((None, D_MODEL, tn), lambda i, j: (layer, 0, j)),
        ],
        out_specs=pl.BlockSpec((TM, tn), lambda i, j: (i, j)),
        out_shape=jax.ShapeDtypeStruct((grp.n, D_MODEL), f32),
        compiler_params=_cp(2),
        name="out_proj",
    )(x, mod, merged, w_out)


def _ffn_kernel(x_ref, mod_ref, nw_ref, w1_ref, w2_ref, o_ref, h_scr, r_scr, gs_scr, *, nj):
    j = pl.program_id(1)

    @pl.when(j == 0)
    def _():
        _modnorm_rows(x_ref, nw_ref[...], mod_ref[3:4, :], mod_ref[4:5, :], h_scr, r_scr, gs_scr)
        o_ref[...] = jnp.zeros_like(o_ref)

    a = jnp.maximum(jnp.dot(h_scr[...], w1_ref[...], preferred_element_type=f32), 0.0)
    a2 = (a * a).astype(bf16)
    cw = 512
    for cidx in range(D_MODEL // cw):
        cs = slice(cidx * cw, (cidx + 1) * cw)
        o_ref[:, cs] += jnp.dot(a2, w2_ref[:, cs], preferred_element_type=f32)

    @pl.when(j == nj - 1)
    def _():
        o_ref[...] = x_ref[...] + mod_ref[5:6, :] * o_ref[...]


def _ffn(x, mod, nw, w1, w2, layer, grp):
    tf = 512
    nj = D_FF // tf
    return pl.pallas_call(
        functools.partial(_ffn_kernel, nj=nj),
        grid=(grp.n // TM, nj),
        in_specs=[
            pl.BlockSpec((TM, D_MODEL), lambda i, j: (i, 0)),
            pl.BlockSpec((None, None, 6, D_MODEL), lambda i, j: (layer, _mod_row(grp, i), 0, 0)),
            pl.BlockSpec((None, 1, D_MODEL), lambda i, j: (layer, 0, 0)),
            pl.BlockSpec((None, D_MODEL, tf), lambda i, j: (layer, 0, j)),
            pl.BlockSpec((None, tf, D_MODEL), lambda i, j: (layer, j, 0)),
        ],
        out_specs=pl.BlockSpec((TM, D_MODEL), lambda i, j: (i, 0)),
        out_shape=jax.ShapeDtypeStruct((grp.n, D_MODEL), f32),
        scratch_shapes=[pltpu.VMEM((TM, D_MODEL), bf16), pltpu.VMEM((TM, 128), f32),
                        pltpu.VMEM((2, SUB, D_MODEL), f32)],
        compiler_params=_cp(2),
        name="ffn",
    )(x, mod, nw, w1, w2)


def _rope_tables(seq):
    t = jnp.arange(seq)
    row = (t // GRID_W).astype(f32)
    col = (t % GRID_W).astype(f32)
    n_pairs = HEAD_DIM // 4
    freqs = ROPE_BASE ** (-jnp.arange(n_pairs, dtype=f32) / n_pairs)
    ang = jnp.concatenate([row[:, None] * freqs, col[:, None] * freqs], axis=-1)
    cos = jnp.cos(ang)
    sin = jnp.sin(ang)
    return jnp.concatenate([cos, cos], axis=-1), jnp.concatenate([-sin, sin], axis=-1)


def _block_diag_in(w):
    eye = jnp.eye(SSM_GROUPS, dtype=w.dtype)
    return jnp.einsum("gpc,gh->gchp", w, eye).reshape(SSM_WIDTH, SSM_N)


def _block_diag_out(w):
    eye = jnp.eye(SSM_GROUPS, dtype=w.dtype)
    return jnp.einsum("gcp,gh->gphc", w, eye).reshape(SSM_N, SSM_WIDTH)


def _ssm_params(a_re, a_im, log_dt, b_re, b_im, c_re, c_im):
    dt = jnp.exp(log_dt)[..., None]
    mag = jnp.exp(a_re * dt)
    ab_re = mag * jnp.cos(a_im * dt)
    ab_im = mag * jnp.sin(a_im * dt)
    den = a_re * a_re + a_im * a_im
    q_re = (((ab_re - 1.0) * a_re + ab_im * a_im) / den)[..., None]
    q_im = ((ab_im * a_re - (ab_re - 1.0) * a_im) / den)[..., None]
    bb_re = q_re * b_re - q_im * b_im
    bb_im = q_re * b_im + q_im * b_re
    a = jnp.stack([ab_re.reshape(2, SSM_N), ab_im.reshape(2, SSM_N)], axis=1)

    def diag_blocks(full, rows, cols):
        return jnp.stack([full[i * rows:(i + 1) * rows, i * cols:(i + 1) * cols] for i in range(SSM_N // SSM_CB)])

    bd = jnp.stack([jnp.stack([diag_blocks(_block_diag_in(w[d]), SSM_CI, SSM_CB) for w in (bb_re, bb_im)])
                    for d in range(2)]).astype(bf16)
    cd = jnp.stack([jnp.stack([diag_blocks(_block_diag_out(w[d]), SSM_CB, SSM_CI) for w in (c_re, -c_im)])
                    for d in range(2)]).astype(bf16)
    return a, bd, cd


def _mixers(z, grp, layer, prm, states):
    nb, seq = grp.nb, grp.seq
    z3 = z.reshape(nb, seq, ZW)
    s_ret0, s_lru0, s_ssm0, cache = states

    o_ret, s_ret = _retention(z, prm["ret_lg"], prm["ret_gn"], s_ret0, grp)

    hf, lru_f = _lru_dir(z3, None, None if s_lru0 is None else s_lru0[:, 0], prm["lru"], grp, 0)
    o_lru, lru_b = _lru_dir(z3, hf, None if s_lru0 is None else s_lru0[:, 1], prm["lru"], grp, 1)

    att = _attention(z, prm["att_qw"], prm["att_kw"], prm["att_sink"], prm["rope"], cache, grp)

    yf, ssm_f = _s5_dir(z3, None, None if s_ssm0 is None else s_ssm0[:, 0], prm["ssm"], grp, 0)
    o_ssm, ssm_b = _s5_dir(z3, yf, None if s_ssm0 is None else s_ssm0[:, 1], prm["ssm"], grp, 1)

    branches = [o_ret, o_lru.reshape(grp.n, LRU_WIDTH), att[0], o_ssm.reshape(grp.n, SSM_WIDTH)]
    new_state = None
    if grp.ctx:
        ssm_fin = jnp.stack([ssm_f, ssm_b], axis=1)
        new_state = (att[1].reshape(nb, seq, ATT_KV_HEADS, HEAD_DIM), att[2].reshape(nb, seq, ATT_KV_HEADS, HEAD_DIM),
                     s_ret, jnp.stack([lru_f, lru_b], axis=1),
                     ssm_fin[:, :, :SSM_N].reshape(nb, 2, SSM_GROUPS, SSM_STATE),
                     ssm_fin[:, :, SSM_N:].reshape(nb, 2, SSM_GROUPS, SSM_STATE))
    return branches, new_state


def kernel(x_prompt, x_sample, cache_attn_k, cache_attn_v, state_ret, state_lru, state_ssm_re, state_ssm_im, c, c_ctx, w_mod, b_mod, norm1, w_in, ret_decay_logit, ret_gn, lru_conv_w, lru_conv_b, lru_wa, lru_ba, lru_wx, lru_bx, lru_lambda, att_q_norm, att_k_norm, att_sink, ssm_a_re, ssm_a_im, ssm_log_dt, ssm_b_re, ssm_b_im, ssm_c_re, ssm_c_im, ssm_d, ssm_w_glu, ssm_b_glu, w_br, w_out, norm2, w_ff1, w_ff2):
    xs = {PROMPT: x_prompt.reshape(PROMPT.n, D_MODEL), SAMPLE: x_sample.reshape(SAMPLE.n, D_MODEL)}
    cond = jnp.concatenate([c, c_ctx[None, :], jnp.zeros((COND_ROWS - SAMPLE.nb - 1, D_MODEL), f32)], axis=0)
    mod = _modulation(cond, w_mod, b_mod).reshape(DEPTH, COND_ROWS, 6, D_MODEL)

    w_in_b = w_in.astype(bf16)
    w_br_b = w_br.astype(bf16)
    w_out_b = w_out.astype(bf16)
    w_ff1_b = w_ff1.astype(bf16)
    w_ff2_b = w_ff2.astype(bf16)
    norm1_r = norm1.reshape(DEPTH, 1, D_MODEL)
    norm2_r = norm2.reshape(DEPTH, 1, D_MODEL)
    rope = _rope_tables(SAMPLE.seq)
    log_gamma = -jax.nn.softplus(-ret_decay_logit)
    lru_sp = jax.nn.softplus(-lru_lambda)
    kvw = ATT_KV_HEADS * HEAD_DIM

    new_states = []
    for l in range(DEPTH):
        ssm_a, ssm_bd, ssm_cd = _ssm_params(ssm_a_re[l], ssm_a_im[l], ssm_log_dt[l], ssm_b_re[l], ssm_b_im[l],
                                            ssm_c_re[l], ssm_c_im[l])
        prm = {
            "ret_lg": jnp.broadcast_to(log_gamma[l].T[:, :, None], (RET_HEADS, 2, 128)),
            "ret_gn": ret_gn[l].reshape(1, RET_HEADS * 128),
            "lru": {"cw": lru_conv_w[l], "cb": lru_conv_b[l].reshape(1, LRU_WIDTH),
                    "wa": lru_wa[l].astype(bf16), "ba": lru_ba[l].reshape(2, 1, LRU_WIDTH),
                    "wx": lru_wx[l].astype(bf16), "bx": lru_bx[l].reshape(2, 1, LRU_WIDTH),
                    "sp": lru_sp[l].reshape(2, 1, LRU_WIDTH)},
            "att_qw": att_q_norm[l].reshape(1, HEAD_DIM), "att_kw": att_k_norm[l].reshape(1, HEAD_DIM),
            "att_sink": att_sink[l], "rope": rope,
            "ssm": {"a": ssm_a, "bd": ssm_bd, "cd": ssm_cd, "d": ssm_d[l].reshape(1, SSM_WIDTH),
                    "wglu": ssm_w_glu[l].astype(bf16), "bglu": ssm_b_glu[l].reshape(1, SSM_WIDTH)},
        }
        states = {
            PROMPT: (None, None, None, None),
            SAMPLE: (state_ret[:, l], state_lru[:, l],
                     jnp.concatenate([state_ssm_re[:, l].reshape(SAMPLE.nb, 2, SSM_N),
                                      state_ssm_im[:, l].reshape(SAMPLE.nb, 2, SSM_N)], axis=-1),
                     (cache_attn_k[:, l].reshape(SAMPLE.nb, PAST_LEN, kvw),
                      cache_attn_v[:, l].reshape(SAMPLE.nb, PAST_LEN, kvw))),
        }
        for grp in (PROMPT, SAMPLE):
            x = xs[grp]
            z, h = _proj_in(x, mod, norm1_r, w_in_b, l, grp)
            branches, st = _mixers(z, grp, l, prm, states[grp])
            if grp.ctx:
                new_states.append(st)
            merged = _merge(h, branches, w_in_b, w_br_b, l, grp)
            x = _out_proj(x, mod, merged, w_out_b, l, grp)
            xs[grp] = _ffn(x, mod, norm2_r, w_ff1_b, w_ff2_b, l, grp)

    y_prompt = xs[PROMPT].reshape(PROMPT.nb, PROMPT.seq, D_MODEL)
    y_sample = xs[SAMPLE].reshape(SAMPLE.nb, SAMPLE.seq, D_MODEL)
    return (y_prompt, y_sample) + tuple(jnp.stack([st[i] for st in new_states], axis=1) for i in range(6))
```

```python
import functools
from typing import NamedTuple

import jax
import jax.numpy as jnp
from jax import lax
from jax.experimental import pallas as pl
from jax.experimental.pallas import tpu as pltpu

f32 = jnp.float32
bf16 = jnp.bfloat16

D_MODEL = 2048
DEPTH = 4
PAST_LEN = 256
GRID_W = 64
CHUNK = 128
EPS = 1e-6
NEG_INF = -1e30
RET_HEADS = 4
RET_DK = 128
LRU_WIDTH = 512
LRU_BLOCKS = 4
LRU_C = 8.0
CONV_W = 4
ATT_Q_HEADS = 4
ATT_KV_HEADS = 2
HEAD_DIM = 128
WINDOW = 128
ROPE_BASE = 10000.0
SSM_WIDTH = 512
SSM_GROUP = 16
SSM_GROUPS = 32
SSM_STATE = 64
SSM_N = SSM_GROUPS * SSM_STATE
SSM_CB = 512
SSM_CI = SSM_CB // SSM_STATE * SSM_GROUP
N_BRANCH = 4
BRANCH_WIDTH = 512
D_FF = 4 * D_MODEL
RQ, RK, RV, RG, LX, LG, AQ, AK, AV, SU, ZW = 0, 512, 1024, 1536, 2048, 2560, 3072, 3584, 3840, 4096, 4608
SUB = 8
COND_ROWS = 16
VMEM_LIMIT = 56 * 1024 * 1024

TM = 1024
LRU_TC = 128
SSM_TC = 128
ROW_BLK = 16


class Group(NamedTuple):
    nb: int
    seq: int
    ctx: bool

    @property
    def n(self):
        return self.nb * self.seq


PROMPT = Group(32, 256, True)
SAMPLE = Group(8, 2048, False)
CTX_ROW = SAMPLE.nb


def _cp(n_axes, vmem=VMEM_LIMIT):
    return pltpu.CompilerParams(dimension_semantics=("arbitrary",) * n_axes, vmem_limit_bytes=vmem)


def _sigmoid(x):
    return 0.5 * jnp.tanh(0.5 * x) + 0.5


def _mod_row(grp, i):
    return CTX_ROW if grp.ctx else i // (grp.seq // TM)


def _modnorm_rows(x_ref, nw, shift, scale, h_ref, r_scr, gs_scr):
    tm, d = x_ref.shape
    lane_tiles = d // 128

    def ssq_body(r, carry):
        rows = pl.ds(pl.multiple_of(r * SUB, SUB), SUB)
        acc = None
        for t in range(lane_tiles):
            v = x_ref[rows, t * 128:(t + 1) * 128]
            acc = v * v if acc is None else acc + v * v
        r_scr[rows, :] = acc
        return carry

    lax.fori_loop(0, tm // SUB, ssq_body, 0, unroll=4)
    ssq = jnp.sum(r_scr[...], axis=-1, keepdims=True)
    r_scr[...] = jnp.broadcast_to(lax.rsqrt(ssq * (1.0 / d) + EPS), (tm, 128))
    gs_scr[0] = jnp.broadcast_to(nw * (1.0 + scale), (SUB, d))
    gs_scr[1] = jnp.broadcast_to(shift, (SUB, d))

    def out_body(r, carry):
        r0 = pl.multiple_of(r * ROW_BLK, ROW_BLK)
        halves = [pl.ds(pl.multiple_of(r0 + k * SUB, SUB), SUB) for k in range(ROW_BLK // SUB)]
        invs = [r_scr[rows, :] for rows in halves]
        for t in range(lane_tiles):
            cols = slice(t * 128, (t + 1) * 128)
            gain = gs_scr[0, :, cols]
            shf = gs_scr[1, :, cols]
            parts = [x_ref[rows, cols] * inv * gain + shf for rows, inv in zip(halves, invs)]
            h_ref[pl.ds(r0, ROW_BLK), cols] = jnp.concatenate(parts, axis=0).astype(bf16)
        return carry

    lax.fori_loop(0, tm // ROW_BLK, out_body, 0, unroll=2)


def _mod_kernel(c_ref, w_ref, b_ref, o_ref):
    c = c_ref[...]
    s = (c * _sigmoid(c)).astype(bf16)
    o_ref[...] = jnp.dot(s, w_ref[...].astype(bf16), preferred_element_type=f32) + b_ref[...]


def _modulation(cond, w_mod, b_mod):
    tn = 1024
    n6 = 6 * D_MODEL
    return pl.pallas_call(
        _mod_kernel,
        grid=(DEPTH, n6 // tn),
        in_specs=[
            pl.BlockSpec((COND_ROWS, D_MODEL), lambda l, j: (0, 0)),
            pl.BlockSpec((None, D_MODEL, tn), lambda l, j: (l, 0, j)),
            pl.BlockSpec((None, 1, tn), lambda l, j: (l, 0, j)),
        ],
        out_specs=pl.BlockSpec((None, COND_ROWS, tn), lambda l, j: (l, 0, j)),
        out_shape=jax.ShapeDtypeStruct((DEPTH, COND_ROWS, n6), f32),
        compiler_params=_cp(2),
        name="modulation",
    )(cond, w_mod, b_mod.reshape(DEPTH, 1, n6))


def _proj_kernel(x_ref, mod_ref, nw_ref, w_ref, o_ref, h_ref, r_scr, gs_scr):
    @pl.when(pl.program_id(1) == 0)
    def _():
        _modnorm_rows(x_ref, nw_ref[...], mod_ref[0:1, :], mod_ref[1:2, :], h_ref, r_scr, gs_scr)

    o_ref[...] = jnp.dot(h_ref[...], w_ref[...], preferred_element_type=f32)


def _proj_in(x, mod, nw, w_in, layer, grp):
    tn = 1536
    return pl.pallas_call(
        _proj_kernel,
        grid=(grp.n // TM, ZW // tn),
        in_specs=[
            pl.BlockSpec((TM, D_MODEL), lambda i, j: (i, 0)),
            pl.BlockSpec((None, None, 6, D_MODEL), lambda i, j: (layer, _mod_row(grp, i), 0, 0)),
            pl.BlockSpec((None, 1, D_MODEL), lambda i, j: (layer, 0, 0)),
            pl.BlockSpec((None, D_MODEL, tn), lambda i, j: (layer, 0, j)),
        ],
        out_specs=[pl.BlockSpec((TM, tn), lambda i, j: (i, j)),
                   pl.BlockSpec((TM, D_MODEL), lambda i, j: (i, 0))],
        out_shape=[jax.ShapeDtypeStruct((grp.n, ZW), f32), jax.ShapeDtypeStruct((grp.n, D_MODEL), bf16)],
        scratch_shapes=[pltpu.VMEM((TM, 128), f32), pltpu.VMEM((2, SUB, D_MODEL), f32)],
        compiler_params=_cp(2),
        name="proj_in",
    )(x, mod, nw, w_in)


def _ret_kernel(*refs, nc, has_s0, emit_state):
    q_ref, k_ref, v_ref, g_ref, lg_ref, gn_ref = refs[:6]
    pos = 6
    s0_ref = None
    if has_s0:
        s0_ref = refs[pos]
        pos += 1
    o_ref = refs[pos]
    pos += 1
    sout_ref = None
    if emit_state:
        sout_ref = refs[pos]
        pos += 1
    w_scr, kv_scr = refs[pos], refs[pos + 1]

    C = CHUNK
    H = RET_HEADS
    DEC, W_IN_F, W_IN_B, W_END_F, W_END_B = range(5)
    tn_dims = (((0,), (0,)), ((), ()))
    nt_dims = (((1,), (1,)), ((), ()))

    @pl.when(pl.program_id(0) == 0)
    def _():
        ii = lax.broadcasted_iota(jnp.int32, (C, C), 0).astype(f32)
        jj = lax.broadcasted_iota(jnp.int32, (C, C), 1).astype(f32)
        rel = ii - jj
        for h in range(H):
            lgf = lg_ref[h, 0:1, :]
            lgb = lg_ref[h, 1:2, :]
            w_scr[h, DEC] = (jnp.where(rel >= 0, jnp.exp(jnp.maximum(rel, 0.0) * lgf), 0.0)
                             + jnp.where(rel <= 0, jnp.exp(jnp.maximum(-rel, 0.0) * lgb), 0.0))
            w_scr[h, W_IN_F] = jnp.exp((ii + 1.0) * lgf)
            w_scr[h, W_IN_B] = jnp.exp((C - ii) * lgb)
            w_scr[h, W_END_F] = jnp.exp((C - 1.0 - ii) * lgf)
            w_scr[h, W_END_B] = jnp.exp(ii * lgb)

    def kv_body(n, carry):
        r = pl.multiple_of(n * C, C)
        for h in range(H):
            cols = slice(h * C, (h + 1) * C)
            k = k_ref[pl.ds(r, C), cols]
            vb = v_ref[pl.ds(r, C), cols].astype(bf16)
            kw = jnp.concatenate([k * w_scr[h, W_END_F], k * w_scr[h, W_END_B]], axis=1).astype(bf16)
            kv = lax.dot_general(kw, vb, tn_dims, preferred_element_type=f32)
            kv_scr[0, n * H + h] = kv[:C]
            kv_scr[1, n * H + h] = kv[C:]
        return carry

    lax.fori_loop(0, nc, kv_body, 0)

    for h in range(H):
        cd_f = jnp.exp(C * lg_ref[h, 0:1, :])
        cd_b = jnp.exp(C * lg_ref[h, 1:2, :])

        def fwd_body(n, s, h=h, cd_f=cd_f):
            kv = kv_scr[0, n * H + h]
            kv_scr[0, n * H + h] = s
            return cd_f * s + kv

        def bwd_body(i, s, h=h, cd_b=cd_b):
            n = nc - 1 - i
            kv = kv_scr[1, n * H + h]
            kv_scr[1, n * H + h] = s
            return cd_b * s + kv

        s_f = lax.fori_loop(0, nc, fwd_body, s0_ref[0, h] if has_s0 else jnp.zeros((C, C), f32))
        s_b = lax.fori_loop(0, nc, bwd_body, s0_ref[1, h] if has_s0 else jnp.zeros((C, C), f32))
        if emit_state:
            sout_ref[0, h] = s_f
            sout_ref[1, h] = s_b

    def out_body(n, carry):
        r = pl.multiple_of(n * C, C)
        for h in range(H):
            cols = slice(h * C, (h + 1) * C)
            q = q_ref[pl.ds(r, C), cols] * (RET_DK ** -0.5)
            kb = k_ref[pl.ds(r, C), cols].astype(bf16)
            vb = v_ref[pl.ds(r, C), cols].astype(bf16)
            sc = lax.dot_general(q.astype(bf16), kb, nt_dims, preferred_element_type=f32) * w_scr[h, DEC]
            lhs = jnp.concatenate([sc, q * w_scr[h, W_IN_F], q * w_scr[h, W_IN_B]], axis=1).astype(bf16)
            rhs = jnp.concatenate([vb, kv_scr[0, n * H + h].astype(bf16), kv_scr[1, n * H + h].astype(bf16)], axis=0)
            o = jnp.dot(lhs, rhs, preferred_element_type=f32)
            mu = jnp.mean(o, axis=-1, keepdims=True)
            d = o - mu
            var = jnp.mean(d * d, axis=-1, keepdims=True)
            on = d * lax.rsqrt(var + EPS) * gn_ref[:, cols]
            g = g_ref[pl.ds(r, C), cols]
            o_ref[pl.ds(r, C), cols] = (g * _sigmoid(g) * on).astype(bf16)
        return carry

    lax.fori_loop(0, nc, out_body, 0, unroll=2)


def _retention(z, lg, gn, s0, grp):
    seq = grp.seq
    W = RET_HEADS * 128
    in_specs = [
        pl.BlockSpec((seq, W), lambda b: (b, RQ // W)),
        pl.BlockSpec((seq, W), lambda b: (b, RK // W)),
        pl.BlockSpec((seq, W), lambda b: (b, RV // W)),
        pl.BlockSpec((seq, W), lambda b: (b, RG // W)),
        pl.BlockSpec((RET_HEADS, 2, 128), lambda b: (0, 0, 0)),
        pl.BlockSpec((1, W), lambda b: (0, 0)),
    ]
    args = [z, z, z, z, lg, gn]
    st_spec = pl.BlockSpec((None, 2, RET_HEADS, 128, 128), lambda b: (b, 0, 0, 0, 0))
    if s0 is not None:
        in_specs.append(st_spec)
        args.append(s0)
    out_specs = [pl.BlockSpec((seq, W), lambda b: (b, 0))]
    out_shape = [jax.ShapeDtypeStruct((grp.n, W), bf16)]
    if grp.ctx:
        out_specs.append(st_spec)
        out_shape.append(jax.ShapeDtypeStruct((grp.nb, 2, RET_HEADS, 128, 128), f32))
    nc = seq // CHUNK
    res = pl.pallas_call(
        functools.partial(_ret_kernel, nc=nc, has_s0=s0 is not None, emit_state=grp.ctx),
        grid=(grp.nb,),
        in_specs=in_specs,
        out_specs=out_specs,
        out_shape=out_shape,
        scratch_shapes=[pltpu.VMEM((RET_HEADS, 5, 128, 128), f32),
                        pltpu.VMEM((2, nc * RET_HEADS, 128, 128), f32)],
        compiler_params=_cp(1),
        name="retention",
    )(*args)
    return res if grp.ctx else (res[0], None)


def _gather_tm(src_ref, nt, dst_ref, row0):
    w = src_ref.shape[-1]

    def body(i, carry):
        v = src_ref[:, pl.ds(pl.multiple_of(i * SUB, SUB), SUB), :]
        rows = pl.ds(pl.multiple_of(row0 + i * SUB * SUB, SUB), SUB * SUB)
        dst_ref[rows, :] = jnp.swapaxes(v, 0, 1).reshape(SUB * SUB, w)
        return carry

    lax.fori_loop(0, nt // SUB, body, 0, unroll=2)


def _scatter_bm(src_ref, nt, dst_ref):
    w = src_ref.shape[-1]

    def body(i, carry):
        v = src_ref[pl.ds(pl.multiple_of(i * SUB * SUB, SUB * SUB), SUB * SUB), :].reshape(SUB, SUB, w)
        dst_ref[:, pl.ds(pl.multiple_of(i * SUB, SUB), SUB), :] = jnp.swapaxes(v, 0, 1)
        return carry

    lax.fori_loop(0, nt // SUB, body, 0, unroll=2)


def _lru_kernel(*refs, tc, nch, direction, has_s0):
    xprev_ref, xcur_ref, xnext_ref, cw_ref, cb_ref, wa_ref, ba_ref, wx_ref, bx_ref, sp_ref = refs[:10]
    pos = 10
    s0_ref = None
    if has_s0:
        s0_ref = refs[pos]
        pos += 1
    if direction == 1:
        gate_ref, hf_ref = refs[pos], refs[pos + 1]
        pos += 2
    out_ref, fin_ref = refs[pos], refs[pos + 1]
    xe_scr, a_scr, b_scr, h_scr = refs[pos + 2: pos + 6]
    if direction == 1:
        g_scr, y_scr = refs[pos + 6], refs[pos + 7]

    R = tc * SUB
    W = LRU_WIDTH
    j = pl.program_id(1)
    c = j if direction == 0 else nch - 1 - j

    @pl.when(j == 0)
    def _():
        h_scr[...] = s0_ref[...] if has_s0 else jnp.zeros((SUB, W), f32)

    for t in range(2):
        v = xprev_ref[:, SUB - 2 + t:SUB - 1 + t, :].reshape(SUB, W)
        xe_scr[t * SUB:(t + 1) * SUB, :] = jnp.where(c > 0, v, 0.0)
    _gather_tm(xcur_ref, tc, xe_scr, 2 * SUB)
    xe_scr[2 * SUB + R:3 * SUB + R, :] = jnp.where(c < nch - 1, xnext_ref[:, 0:1, :].reshape(SUB, W), 0.0)
    if direction == 1:
        _gather_tm(gate_ref, tc, g_scr, 0)

    xc = cb_ref[...] + xe_scr[0:R, :] * cw_ref[0:1, :]
    for t in range(1, CONV_W):
        xc = xc + xe_scr[t * SUB:t * SUB + R, :] * cw_ref[t:t + 1, :]

    bd = W // LRU_BLOCKS
    for n in range(LRU_BLOCKS):
        sl = slice(n * bd, (n + 1) * bd)
        xs = xc[:, sl]
        xb = xs.astype(bf16)
        tr = jnp.tanh(jnp.dot(xb, wa_ref[n], preferred_element_type=f32) + ba_ref[:, sl])
        ti = jnp.tanh(jnp.dot(xb, wx_ref[n], preferred_element_type=f32) + bx_ref[:, sl])
        log_a = sp_ref[:, sl] * (tr + 1.0)
        a = jnp.exp(log_a)
        a_scr[:, sl] = a
        t = -jnp.tanh(log_a) * (a * a + 1.0)
        root = jnp.where(t > 0.0, t * lax.rsqrt(t), 0.0)
        b_scr[:, sl] = (0.5 * root) * ((ti + 1.0) * xs)

    def step(s, h):
        t = s if direction == 0 else tc - 1 - s
        r0 = pl.multiple_of(t * SUB, SUB)
        h = a_scr[pl.ds(r0, SUB), :] * h + b_scr[pl.ds(r0, SUB), :]
        b_scr[pl.ds(r0, SUB), :] = h
        return h

    h = lax.fori_loop(0, tc, step, h_scr[...], unroll=8)
    h_scr[...] = h
    fin_ref[...] = h
    if direction == 0:
        out_ref[...] = b_scr[...]
    else:
        b_scr[...] = jax.nn.gelu(g_scr[...]) * (hf_ref[...] + b_scr[...])
        _scatter_bm(b_scr, tc, y_scr)
        out_ref[...] = y_scr[...].astype(bf16)


def _lru_dir(z3, hf_tm, s0, prm, grp, direction):
    nb, seq = grp.nb, grp.seq
    nbg = nb // SUB
    tc = min(LRU_TC, seq)
    nch = seq // tc
    R = tc * SUB
    W = LRU_WIDTH

    def cidx(j):
        return j if direction == 0 else nch - 1 - j

    in_specs = [
        pl.BlockSpec((SUB, SUB, W), lambda g, j: (g, jnp.maximum(cidx(j) * (tc // SUB) - 1, 0), LX // W)),
        pl.BlockSpec((SUB, tc, W), lambda g, j: (g, cidx(j), LX // W)),
        pl.BlockSpec((SUB, SUB, W), lambda g, j: (g, jnp.minimum((cidx(j) + 1) * (tc // SUB), seq // SUB - 1), LX // W)),
        pl.BlockSpec((CONV_W, W), lambda g, j: (0, 0)),
        pl.BlockSpec((1, W), lambda g, j: (0, 0)),
        pl.BlockSpec((LRU_BLOCKS, 128, 128), lambda g, j: (0, 0, 0)),
        pl.BlockSpec((1, W), lambda g, j: (0, 0)),
        pl.BlockSpec((LRU_BLOCKS, 128, 128), lambda g, j: (0, 0, 0)),
        pl.BlockSpec((1, W), lambda g, j: (0, 0)),
        pl.BlockSpec((1, W), lambda g, j: (0, 0)),
    ]
    args = [z3, z3, z3, prm["cw"], prm["cb"], prm["wa"][direction], prm["ba"][direction],
            prm["wx"][direction], prm["bx"][direction], prm["sp"][direction]]
    st_spec = pl.BlockSpec((SUB, W), lambda g, j: (g, 0))
    if s0 is not None:
        in_specs.append(st_spec)
        args.append(s0)
    tm_spec = pl.BlockSpec((None, R, W), lambda g, j: (g, cidx(j), 0))
    scratch = [pltpu.VMEM((R + 3 * SUB, W), f32), pltpu.VMEM((R, W), f32),
               pltpu.VMEM((R, W), f32), pltpu.VMEM((SUB, W), f32)]
    if direction == 0:
        out_spec = tm_spec
        out_sds = jax.ShapeDtypeStruct((nbg, seq * SUB, W), f32)
    else:
        in_specs += [pl.BlockSpec((SUB, tc, W), lambda g, j: (g, cidx(j), LG // W)), tm_spec]
        args += [z3, hf_tm]
        out_spec = pl.BlockSpec((SUB, tc, W), lambda g, j: (g, cidx(j), 0))
        out_sds = jax.ShapeDtypeStruct((nb, seq, W), bf16)
        scratch += [pltpu.VMEM((R, W), f32), pltpu.VMEM((SUB, tc, W), f32)]
    return pl.pallas_call(
        functools.partial(_lru_kernel, tc=tc, nch=nch, direction=direction, has_s0=s0 is not None),
        grid=(nbg, nch),
        in_specs=in_specs,
        out_specs=[out_spec, st_spec],
        out_shape=[out_sds, jax.ShapeDtypeStruct((nb, W), f32)],
        scratch_shapes=scratch,
        compiler_params=_cp(2),
        name="rglru_dir%d" % direction,
    )(*args)


def _s5_kernel(*refs, tc, nch, direction, has_s0):
    u_ref, bd_ref, a_ref, cd_ref = refs[:4]
    pos = 4
    s0_ref = None
    if has_s0:
        s0_ref = refs[pos]
        pos += 1
    if direction == 1:
        yf_ref, dv_ref, wg_ref, bg_ref = refs[pos:pos + 4]
        pos += 4
    out_ref, fin_ref = refs[pos], refs[pos + 1]
    u_scr, ub_scr, hs_scr, h_scr = refs[pos + 2: pos + 6]
    if direction == 1:
        y_scr = refs[pos + 6]

    j = pl.program_id(1)

    @pl.when(j == 0)
    def _():
        h_scr[...] = s0_ref[...] if has_s0 else jnp.zeros((SUB, 2 * SSM_N), f32)

    _gather_tm(u_ref, tc, u_scr, 0)
    ub_scr[...] = u_scr[...].astype(bf16)

    cbw = SSM_CB
    for cb in range(SSM_N // cbw):
        cre = slice(cb * cbw, (cb + 1) * cbw)
        cim = slice(SSM_N + cb * cbw, SSM_N + (cb + 1) * cbw)
        cin = slice(cb * SSM_CI, (cb + 1) * SSM_CI)
        hs_scr[:, cre] = jnp.dot(ub_scr[:, cin], bd_ref[0, cb], preferred_element_type=f32)
        hs_scr[:, cim] = jnp.dot(ub_scr[:, cin], bd_ref[1, cb], preferred_element_type=f32)
        a_re = jnp.broadcast_to(a_ref[0:1, cre], (SUB, cbw))
        a_im = jnp.broadcast_to(a_ref[1:2, cre], (SUB, cbw))

        def step(s, carry):
            hr, hi = carry
            t = s if direction == 0 else tc - 1 - s
            r0 = pl.multiple_of(t * SUB, SUB)
            nr = a_re * hr - a_im * hi + hs_scr[pl.ds(r0, SUB), cre]
            ni = a_re * hi + a_im * hr + hs_scr[pl.ds(r0, SUB), cim]
            hs_scr[pl.ds(r0, SUB), cre] = nr
            hs_scr[pl.ds(r0, SUB), cim] = ni
            return nr, ni

        hr, hi = lax.fori_loop(0, tc, step, (h_scr[:, cre], h_scr[:, cim]), unroll=True)
        h_scr[:, cre] = hr
        h_scr[:, cim] = hi

    fin_ref[...] = h_scr[...]
    y_parts = []
    for cb in range(SSM_N // cbw):
        cre = slice(cb * cbw, (cb + 1) * cbw)
        cim = slice(SSM_N + cb * cbw, SSM_N + (cb + 1) * cbw)
        y_parts.append(jnp.dot(hs_scr[:, cre].astype(bf16), cd_ref[0, cb], preferred_element_type=f32)
                       + jnp.dot(hs_scr[:, cim].astype(bf16), cd_ref[1, cb], preferred_element_type=f32))
    y = jnp.concatenate(y_parts, axis=1)
    if direction == 0:
        out_ref[...] = y
    else:
        yy = jax.nn.gelu(yf_ref[...] + y + dv_ref[...] * u_scr[...])
        gl = jnp.dot(yy.astype(bf16), wg_ref[...], preferred_element_type=f32) + bg_ref[...]
        u_scr[...] = yy * _sigmoid(gl)
        _scatter_bm(u_scr, tc, y_scr)
        out_ref[...] = y_scr[...].astype(bf16)


def _s5_dir(z3, yf_tm, s0, prm, grp, direction):
    nb, seq = grp.nb, grp.seq
    nbg = nb // SUB
    tc = min(SSM_TC, seq)
    nch = seq // tc
    R = tc * SUB
    W = SSM_WIDTH
    ncb = SSM_N // SSM_CB

    def cidx(j):
        return j if direction == 0 else nch - 1 - j

    tm_spec = pl.BlockSpec((None, R, W), lambda g, j: (g, cidx(j), 0))
    st_spec = pl.BlockSpec((SUB, 2 * SSM_N), lambda g, j: (g, 0))
    in_specs = [
        pl.BlockSpec((SUB, tc, W), lambda g, j: (g, cidx(j), SU // W)),
        pl.BlockSpec((2, ncb, SSM_CI, SSM_CB), lambda g, j: (0, 0, 0, 0)),
        pl.BlockSpec((2, SSM_N), lambda g, j: (0, 0)),
        pl.BlockSpec((2, ncb, SSM_CB, SSM_CI), lambda g, j: (0, 0, 0, 0)),
    ]
    args = [z3, prm["bd"][direction], prm["a"][direction], prm["cd"][direction]]
    if s0 is not None:
        in_specs.append(st_spec)
        args.append(s0)
    scratch = [pltpu.VMEM((R, W), f32), pltpu.VMEM((R, W), bf16), pltpu.VMEM((R, 2 * SSM_N), f32),
               pltpu.VMEM((SUB, 2 * SSM_N), f32)]
    if direction == 0:
        out_spec = tm_spec
        out_sds = jax.ShapeDtypeStruct((nbg, seq * SUB, W), f32)
    else:
        in_specs += [tm_spec, pl.BlockSpec((1, W), lambda g, j: (0, 0)),
                     pl.BlockSpec((W, W), lambda g, j: (0, 0)), pl.BlockSpec((1, W), lambda g, j: (0, 0))]
        args += [yf_tm, prm["d"], prm["wglu"], prm["bglu"]]
        out_spec = pl.BlockSpec((SUB, tc, W), lambda g, j: (g, cidx(j), 0))
        out_sds = jax.ShapeDtypeStruct((nb, seq, W), bf16)
        scratch.append(pltpu.VMEM((SUB, tc, W), f32))
    return pl.pallas_call(
        functools.partial(_s5_kernel, tc=tc, nch=nch, direction=direction, has_s0=s0 is not None),
        grid=(nbg, nch),
        in_specs=in_specs,
        out_specs=[out_spec, st_spec],
        out_shape=[out_sds, jax.ShapeDtypeStruct((nb, 2 * SSM_N), f32)],
        scratch_shapes=scratch,
        compiler_params=_cp(2),
        name="s5_dir%d" % direction,
    )(*args)


def _rope(x, cos, sin_signed):
    return x * cos + pltpu.roll(x, HEAD_DIM // 2, 1) * sin_signed


def _unit_rms(x, w):
    return x * lax.rsqrt(jnp.mean(x * x, axis=-1, keepdims=True) + EPS) * w


def _attn_kernel(*refs, seq, latent):
    q_ref, k_ref, v_ref, qw_ref, kw_ref, sink_ref = refs[:6]
    pos = 6
    if latent:
        cos_ref, sin_ref, ck_ref, cv_ref = refs[pos:pos + 4]
        pos += 4
    o_ref = refs[pos]
    pos += 1
    if not latent:
        kout_ref, vout_ref = refs[pos], refs[pos + 1]
        pos += 2
    kb_scr, vb_scr = refs[pos], refs[pos + 1]
    pos += 2
    if latent:
        ckb_scr, cvb_scr = refs[pos], refs[pos + 1]

    HD = HEAD_DIM
    qb = pl.program_id(1)

    @pl.when(qb == 0)
    def _():
        for hk in range(ATT_KV_HEADS):
            sl = slice(hk * HD, (hk + 1) * HD)
            kn = _unit_rms(k_ref[:, sl], kw_ref[...])
            if latent:
                kn = _rope(kn, cos_ref[...], sin_ref[...])
            else:
                kout_ref[:, sl] = kn
            kb_scr[:, sl] = kn.astype(bf16)
        v = v_ref[...]
        vb_scr[...] = v.astype(bf16)
        if latent:
            ckb_scr[...] = ck_ref[...].astype(bf16)
            cvb_scr[...] = cv_ref[...].astype(bf16)
        else:
            vout_ref[...] = v

    q0 = pl.multiple_of(qb * CHUNK, CHUNK)
    if latent:
        win = 3 * CHUNK
        start = pl.multiple_of(jnp.clip(q0 - CHUNK, 0, seq - win), CHUNK)
        qpos = q0 + (lax.broadcasted_iota(jnp.int32, (2 * CHUNK, win), 0) & (CHUNK - 1))
        kpos = start + lax.broadcasted_iota(jnp.int32, (2 * CHUNK, win), 1)
        band = jnp.abs(qpos - kpos) <= WINDOW
        cos_q = cos_ref[pl.ds(q0, CHUNK), :]
        sin_q = sin_ref[pl.ds(q0, CHUNK), :]
    else:
        win = seq
        start = 0
    nt_dims = (((1,), (1,)), ((), ()))
    row = lax.broadcasted_iota(jnp.int32, (2 * CHUNK, 1), 0)

    def lane_tiles(x):
        return [x[:, t * HD:(t + 1) * HD] for t in range(x.shape[1] // HD)]

    def fold(op, tiles):
        acc = tiles[0]
        for t in tiles[1:]:
            acc = op(acc, t)
        return acc

    for hk in range(ATT_KV_HEADS):
        sl = slice(hk * HD, (hk + 1) * HD)
        qs = []
        for g in range(2):
            hq = hk * 2 + g
            qn = _unit_rms(q_ref[:, hq * HD:(hq + 1) * HD], qw_ref[...])
            if latent:
                qn = _rope(qn, cos_q, sin_q)
            qs.append((qn * (HD ** -0.5)).astype(bf16))
        qg = jnp.concatenate(qs, axis=0)
        snk = jnp.where(row < CHUNK, sink_ref[hk * 2], sink_ref[hk * 2 + 1])
        s1 = lax.dot_general(qg, kb_scr[pl.ds(start, win), sl], nt_dims, preferred_element_type=f32)
        if latent:
            s1 = jnp.where(band, s1, NEG_INF)
            s2 = lax.dot_general(qg, ckb_scr[:, sl], nt_dims, preferred_element_type=f32)
        s_tiles = lane_tiles(s1) + (lane_tiles(s2) if latent else [])
        m = jnp.maximum(jnp.max(fold(jnp.maximum, s_tiles), axis=-1, keepdims=True), snk)
        p1 = jnp.exp(s1 - m)
        if latent:
            p2 = jnp.exp(s2 - m)
        p_tiles = lane_tiles(p1) + (lane_tiles(p2) if latent else [])
        den = jnp.sum(fold(jnp.add, p_tiles), axis=-1, keepdims=True) + jnp.exp(snk - m)
        o = jnp.dot((p1 / den).astype(bf16), vb_scr[pl.ds(start, win), sl], preferred_element_type=f32)
        if latent:
            o += jnp.dot((p2 / den).astype(bf16), cvb_scr[:, sl], preferred_element_type=f32)
        for g in range(2):
            hq = hk * 2 + g
            o_ref[:, hq * HD:(hq + 1) * HD] = o[g * CHUNK:(g + 1) * CHUNK].astype(bf16)


def _attention(z, qw, kw, sink, rope, cache, grp):
    nb, seq = grp.nb, grp.seq
    latent = not grp.ctx
    nqb = seq // CHUNK
    kvw = ATT_KV_HEADS * HEAD_DIM
    in_specs = [
        pl.BlockSpec((CHUNK, 512), lambda b, i: (b * nqb + i, AQ // 512)),
        pl.BlockSpec((seq, kvw), lambda b, i: (b, AK // kvw)),
        pl.BlockSpec((seq, kvw), lambda b, i: (b, AV // kvw)),
        pl.BlockSpec((1, HEAD_DIM), lambda b, i: (0, 0)),
        pl.BlockSpec((1, HEAD_DIM), lambda b, i: (0, 0)),
        pl.BlockSpec(memory_space=pltpu.SMEM),
    ]
    args = [z, z, z, qw, kw, sink]
    scratch = [pltpu.VMEM((seq, kvw), bf16), pltpu.VMEM((seq, kvw), bf16)]
    out_specs = [pl.BlockSpec((CHUNK, 512), lambda b, i: (b * nqb + i, 0))]
    out_shape = [jax.ShapeDtypeStruct((nb * seq, ATT_Q_HEADS * HEAD_DIM), bf16)]
    if latent:
        in_specs += [pl.BlockSpec((seq, HEAD_DIM), lambda b, i: (0, 0)),
                     pl.BlockSpec((seq, HEAD_DIM), lambda b, i: (0, 0)),
                     pl.BlockSpec((None, PAST_LEN, kvw), lambda b, i: (b, 0, 0)),
                     pl.BlockSpec((None, PAST_LEN, kvw), lambda b, i: (b, 0, 0))]
        args += [rope[0], rope[1], cache[0], cache[1]]
        scratch += [pltpu.VMEM((PAST_LEN, kvw), bf16), pltpu.VMEM((PAST_LEN, kvw), bf16)]
    else:
        kv_out = pl.BlockSpec((seq, kvw), lambda b, i: (b, 0))
        out_specs += [kv_out, kv_out]
        out_shape += [jax.ShapeDtypeStruct((nb * seq, kvw), f32)] * 2
    return pl.pallas_call(
        functools.partial(_attn_kernel, seq=seq, latent=latent),
        grid=(nb, nqb),
        in_specs=in_specs,
        out_specs=out_specs,
        out_shape=out_shape,
        scratch_shapes=scratch,
        compiler_params=_cp(2),
        name="attention_lat" if latent else "attention_ctx",
    )(*args)


def _merge_kernel(h_ref, o0, o1, o2, o3, g0, g1, g2, g3, wbr_ref, out_ref):
    h = h_ref[...]
    acc = None
    for b, (o_ref, g_ref) in enumerate(((o0, g0), (o1, g1), (o2, g2), (o3, g3))):
        gate = _sigmoid(jnp.dot(h, g_ref[...], preferred_element_type=f32))
        t = gate * jnp.dot(o_ref[...], wbr_ref[b], preferred_element_type=f32)
        acc = t if acc is None else acc + t
    out_ref[...] = acc.astype(bf16)


def _merge(h, branches, w_in, w_br, layer, grp):
    tn = 512
    g0 = ZW // tn
    gb = D_MODEL // tn
    o_spec = pl.BlockSpec((TM, BRANCH_WIDTH), lambda i, j: (i, 0))

    def gate_spec(b):
        return pl.BlockSpec((None, D_MODEL, tn), lambda i, j: (layer, 0, g0 + b * gb + j))

    return pl.pallas_call(
        _merge_kernel,
        grid=(grp.n // TM, D_MODEL // tn),
        in_specs=[
            pl.BlockSpec((TM, D_MODEL), lambda i, j: (i, 0)),
            o_spec, o_spec, o_spec, o_spec,
            gate_spec(0), gate_spec(1), gate_spec(2), gate_spec(3),
            pl.BlockSpec((None, N_BRANCH, BRANCH_WIDTH, tn), lambda i, j: (layer, 0, 0, j)),
        ],
        out_specs=pl.BlockSpec((TM, tn), lambda i, j: (i, j)),
        out_shape=jax.ShapeDtypeStruct((grp.n, D_MODEL), bf16),
        compiler_params=_cp(2),
        name="merge",
    )(h, *branches, w_in, w_in, w_in, w_in, w_br)


def _outproj_kernel(x_ref, mod_ref, m_ref, w_ref, o_ref):
    o_ref[...] = x_ref[...] + mod_ref[2:3, :] * jnp.dot(m_ref[...], w_ref[...], preferred_element_type=f32)


def _out_proj(x, mod, merged, w_out, layer, grp):
    tn = 1024
    return pl.pallas_call(
        _outproj_kernel,
        grid=(grp.n // TM, D_MODEL // tn),
        in_specs=[
            pl.BlockSpec((TM, tn), lambda i, j: (i, j)),
            pl.BlockSpec((None, None, 6, tn), lambda i, j: (layer, _mod_row(grp, i), 0, j)),
            pl.BlockSpec((TM, D_MODEL), lambda i, j: (i, 0)),
            pl.BlockSpec((None, D_MODEL, tn), lambda i, j: (layer, 0, j)),
        ],
        out_specs=pl.BlockSpec((TM, tn), lambda i, j: (i, j)),
        out_shape=jax.ShapeDtypeStruct((grp.n, D_MODEL), f32),
        compiler_params=_cp(2),
        name="out_proj",
    )(x, mod, merged, w_out)


def _ffn_kernel(x_ref, mod_ref, nw_ref, w1_ref, w2_ref, o_ref, h_scr, r_scr, gs_scr, *, nj):
    j = pl.program_id(1)

    @pl.when(j == 0)
    def _():
        _modnorm_rows(x_ref, nw_ref[...], mod_ref[3:4, :], mod_ref[4:5, :], h_scr, r_scr, gs_scr)
        o_ref[...] = jnp.zeros_like(o_ref)

    a = jnp.maximum(jnp.dot(h_scr[...], w1_ref[...], preferred_element_type=f32), 0.0)
    a2 = (a * a).astype(bf16)
    cw = 512
    for cidx in range(D_MODEL // cw):
        cs = slice(cidx * cw, (cidx + 1) * cw)
        o_ref[:, cs] += jnp.dot(a2, w2_ref[:, cs], preferred_element_type=f32)

    @pl.when(j == nj - 1)
    def _():
        o_ref[...] = x_ref[...] + mod_ref[5:6, :] * o_ref[...]


def _ffn(x, mod, nw, w1, w2, layer, grp):
    tf = 512
    nj = D_FF // tf
    return pl.pallas_call(
        functools.partial(_ffn_kernel, nj=nj),
        grid=(grp.n // TM, nj),
        in_specs=[
            pl.BlockSpec((TM, D_MODEL), lambda i, j: (i, 0)),
            pl.BlockSpec((None, None, 6, D_MODEL), lambda i, j: (layer, _mod_row(grp, i), 0, 0)),
            pl.BlockSpec((None, 1, D_MODEL), lambda i, j: (layer, 0, 0)),
            pl.BlockSpec((None, D_MODEL, tf), lambda i, j: (layer, 0, j)),
            pl.BlockSpec((None, tf, D_MODEL), lambda i, j: (layer, j, 0)),
        ],
        out_specs=pl.BlockSpec((TM, D_MODEL), lambda i, j: (i, 0)),
        out_shape=jax.ShapeDtypeStruct((grp.n, D_MODEL), f32),
        scratch_shapes=[pltpu.VMEM((TM, D_MODEL), bf16), pltpu.VMEM((TM, 128), f32),
                        pltpu.VMEM((2, SUB, D_MODEL), f32)],
        compiler_params=_cp(2),
        name="ffn",
    )(x, mod, nw, w1, w2)


def _rope_tables(seq):
    t = jnp.arange(seq)
    row = (t // GRID_W).astype(f32)
    col = (t % GRID_W).astype(f32)
    n_pairs = HEAD_DIM // 4
    freqs = ROPE_BASE ** (-jnp.arange(n_pairs, dtype=f32) / n_pairs)
    ang = jnp.concatenate([row[:, None] * freqs, col[:, None] * freqs], axis=-1)
    cos = jnp.cos(ang)
    sin = jnp.sin(ang)
    return jnp.concatenate([cos, cos], axis=-1), jnp.concatenate([-sin, sin], axis=-1)


def _block_diag_in(w):
    eye = jnp.eye(SSM_GROUPS, dtype=w.dtype)
    return jnp.einsum("gpc,gh->gchp", w, eye).reshape(SSM_WIDTH, SSM_N)


def _block_diag_out(w):
    eye = jnp.eye(SSM_GROUPS, dtype=w.dtype)
    return jnp.einsum("gcp,gh->gphc", w, eye).reshape(SSM_N, SSM_WIDTH)


def _ssm_params(a_re, a_im, log_dt, b_re, b_im, c_re, c_im):
    dt = jnp.exp(log_dt)[..., None]
    mag = jnp.exp(a_re * dt)
    ab_re = mag * jnp.cos(a_im * dt)
    ab_im = mag * jnp.sin(a_im * dt)
    den = a_re * a_re + a_im * a_im
    q_re = (((ab_re - 1.0) * a_re + ab_im * a_im) / den)[..., None]
    q_im = ((ab_im * a_re - (ab_re - 1.0) * a_im) / den)[..., None]
    bb_re = q_re * b_re - q_im * b_im
    bb_im = q_re * b_im + q_im * b_re
    a = jnp.stack([ab_re.reshape(2, SSM_N), ab_im.reshape(2, SSM_N)], axis=1)

    def diag_blocks(full, rows, cols):
        return jnp.stack([full[i * rows:(i + 1) * rows, i * cols:(i + 1) * cols] for i in range(SSM_N // SSM_CB)])

    bd = jnp.stack([jnp.stack([diag_blocks(_block_diag_in(w[d]), SSM_CI, SSM_CB) for w in (bb_re, bb_im)])
                    for d in range(2)]).astype(bf16)
    cd = jnp.stack([jnp.stack([diag_blocks(_block_diag_out(w[d]), SSM_CB, SSM_CI) for w in (c_re, -c_im)])
                    for d in range(2)]).astype(bf16)
    return a, bd, cd


def _lru_params(conv_w, conv_b, wa, ba, wx, bx, lam):
    w = LRU_WIDTH
    return {"cw": conv_w, "cb": conv_b.reshape(1, w),
            "wa": (0.5 * wa).astype(bf16), "ba": (0.5 * ba).reshape(2, 1, w),
            "wx": (0.5 * wx).astype(bf16), "bx": (0.5 * bx).reshape(2, 1, w),
            "sp": (-0.5 * LRU_C * jax.nn.softplus(-lam)).reshape(2, 1, w)}


def _mixers(z, grp, layer, prm, states):
    nb, seq = grp.nb, grp.seq
    z3 = z.reshape(nb, seq, ZW)
    s_ret0, s_lru0, s_ssm0, cache = states

    o_ret, s_ret = _retention(z, prm["ret_lg"], prm["ret_gn"], s_ret0, grp)

    hf, lru_f = _lru_dir(z3, None, None if s_lru0 is None else s_lru0[:, 0], prm["lru"], grp, 0)
    o_lru, lru_b = _lru_dir(z3, hf, None if s_lru0 is None else s_lru0[:, 1], prm["lru"], grp, 1)

    att = _attention(z, prm["att_qw"], prm["att_kw"], prm["att_sink"], prm["rope"], cache, grp)

    yf, ssm_f = _s5_dir(z3, None, None if s_ssm0 is None else s_ssm0[:, 0], prm["ssm"], grp, 0)
    o_ssm, ssm_b = _s5_dir(z3, yf, None if s_ssm0 is None else s_ssm0[:, 1], prm["ssm"], grp, 1)

    branches = [o_ret, o_lru.reshape(grp.n, LRU_WIDTH), att[0], o_ssm.reshape(grp.n, SSM_WIDTH)]
    new_state = None
    if grp.ctx:
        ssm_fin = jnp.stack([ssm_f, ssm_b], axis=1)
        new_state = (att[1].reshape(nb, seq, ATT_KV_HEADS, HEAD_DIM), att[2].reshape(nb, seq, ATT_KV_HEADS, HEAD_DIM),
                     s_ret, jnp.stack([lru_f, lru_b], axis=1),
                     ssm_fin[:, :, :SSM_N].reshape(nb, 2, SSM_GROUPS, SSM_STATE),
                     ssm_fin[:, :, SSM_N:].reshape(nb, 2, SSM_GROUPS, SSM_STATE))
    return branches, new_state


def kernel(x_prompt, x_sample, cache_attn_k, cache_attn_v, state_ret, state_lru, state_ssm_re, state_ssm_im, c, c_ctx, w_mod, b_mod, norm1, w_in, ret_decay_logit, ret_gn, lru_conv_w, lru_conv_b, lru_wa, lru_ba, lru_wx, lru_bx, lru_lambda, att_q_norm, att_k_norm, att_sink, ssm_a_re, ssm_a_im, ssm_log_dt, ssm_b_re, ssm_b_im, ssm_c_re, ssm_c_im, ssm_d, ssm_w_glu, ssm_b_glu, w_br, w_out, norm2, w_ff1, w_ff2):
    xs = {PROMPT: x_prompt.reshape(PROMPT.n, D_MODEL), SAMPLE: x_sample.reshape(SAMPLE.n, D_MODEL)}
    cond = jnp.concatenate([c, c_ctx[None, :], jnp.zeros((COND_ROWS - SAMPLE.nb - 1, D_MODEL), f32)], axis=0)
    mod = _modulation(cond, w_mod, b_mod).reshape(DEPTH, COND_ROWS, 6, D_MODEL)

    w_in_b = w_in.astype(bf16)
    w_br_b = w_br.astype(bf16)
    w_out_b = w_out.astype(bf16)
    w_ff1_b = w_ff1.astype(bf16)
    w_ff2_b = w_ff2.astype(bf16)
    norm1_r = norm1.reshape(DEPTH, 1, D_MODEL)
    norm2_r = norm2.reshape(DEPTH, 1, D_MODEL)
    rope = _rope_tables(SAMPLE.seq)
    log_gamma = -jax.nn.softplus(-ret_decay_logit)
    kvw = ATT_KV_HEADS * HEAD_DIM

    new_states = []
    for l in range(DEPTH):
        ssm_a, ssm_bd, ssm_cd = _ssm_params(ssm_a_re[l], ssm_a_im[l], ssm_log_dt[l], ssm_b_re[l], ssm_b_im[l],
                                            ssm_c_re[l], ssm_c_im[l])
        prm = {
            "ret_lg": jnp.broadcast_to(log_gamma[l].T[:, :, None], (RET_HEADS, 2, 128)),
            "ret_gn": ret_gn[l].reshape(1, RET_HEADS * 128),
            "lru": _lru_params(lru_conv_w[l], lru_conv_b[l], lru_wa[l], lru_ba[l], lru_wx[l], lru_bx[l],
                               lru_lambda[l]),
            "att_qw": att_q_norm[l].reshape(1, HEAD_DIM), "att_kw": att_k_norm[l].reshape(1, HEAD_DIM),
            "att_sink": att_sink[l], "rope": rope,
            "ssm": {"a": ssm_a, "bd": ssm_bd, "cd": ssm_cd, "d": ssm_d[l].reshape(1, SSM_WIDTH),
                    "wglu": ssm_w_glu[l].astype(bf16), "bglu": ssm_b_glu[l].reshape(1, SSM_WIDTH)},
        }
        states = {
            PROMPT: (None, None, None, None),
            SAMPLE: (state_ret[:, l], state_lru[:, l],
                     jnp.concatenate([state_ssm_re[:, l].reshape(SAMPLE.nb, 2, SSM_N),
                                      state_ssm_im[:, l].reshape(SAMPLE.nb, 2, SSM_N)], axis=-1),
                     (cache_attn_k[:, l].reshape(SAMPLE.nb, PAST_LEN, kvw),
                      cache_attn_v[:, l].reshape(SAMPLE.nb, PAST_LEN, kvw))),
        }
        for grp in (PROMPT, SAMPLE):
            x = xs[grp]
            z, h = _proj_in(x, mod, norm1_r, w_in_b, l, grp)
            branches, st = _mixers(z, grp, l, prm, states[grp])
            if grp.ctx:
                new_states.append(st)
            merged = _merge(h, branches, w_in_b, w_br_b, l, grp)
            x = _out_proj(x, mod, merged, w_out_b, l, grp)
            xs[grp] = _ffn(x, mod, norm2_r, w_ff1_b, w_ff2_b, l, grp)

    y_prompt = xs[PROMPT].reshape(PROMPT.nb, PROMPT.seq, D_MODEL)
    y_sample = xs[SAMPLE].reshape(SAMPLE.nb, SAMPLE.seq, D_MODEL)
    return (y_prompt, y_sample) + tuple(jnp.stack([st[i] for st in new_states], axis=1) for i in range(6))
```

```python
import functools
from typing import NamedTuple

import jax
import jax.numpy as jnp
from jax import lax
from jax.experimental import pallas as pl
from jax.experimental.pallas import tpu as pltpu

f32 = jnp.float32
bf16 = jnp.bfloat16

D_MODEL = 2048
DEPTH = 4
PAST_LEN = 256
GRID_W = 64
CHUNK = 128
EPS = 1e-6
NEG_INF = -1e30
RET_HEADS = 4
RET_DK = 128
LRU_WIDTH = 512
LRU_BLOCKS = 4
LRU_C = 8.0
CONV_W = 4
ATT_Q_HEADS = 4
ATT_KV_HEADS = 2
HEAD_DIM = 128
WINDOW = 128
ROPE_BASE = 10000.0
SSM_WIDTH = 512
SSM_GROUP = 16
SSM_GROUPS = 32
SSM_STATE = 64
SSM_N = SSM_GROUPS * SSM_STATE
SSM_CB = 512
SSM_CI = SSM_CB // SSM_STATE * SSM_GROUP
N_BRANCH = 4
BRANCH_WIDTH = 512
D_FF = 4 * D_MODEL
RQ, RK, RV, RG, LX, LG, AQ, AK, AV, SU, ZW = 0, 512, 1024, 1536, 2048, 2560, 3072, 3584, 3840, 4096, 4608
SUB = 8
COND_ROWS = 16
VMEM_LIMIT = 56 * 1024 * 1024
FFN_VMEM_LIMIT = 62 * 1024 * 1024

TM = 1024
OUT_TM = 512
LRU_TC = 128
SSM_TC = 128
ROW_BLK = 16


class Group(NamedTuple):
    nb: int
    seq: int
    ctx: bool

    @property
    def n(self):
        return self.nb * self.seq


PROMPT = Group(32, 256, True)
SAMPLE = Group(8, 2048, False)
CTX_ROW = SAMPLE.nb


def _cp(n_axes, vmem=VMEM_LIMIT):
    return pltpu.CompilerParams(dimension_semantics=("arbitrary",) * n_axes, vmem_limit_bytes=vmem)


def _sigmoid(x):
    return 0.5 * jnp.tanh(0.5 * x) + 0.5


def _mod_row(grp, i, tm=None):
    return CTX_ROW if grp.ctx else i // (grp.seq // (tm or TM))


def _modnorm_rows(x_ref, nw, shift, scale, h_ref, r_scr, gs_scr, copy_ref=None):
    tm, d = x_ref.shape
    lane_tiles = d // 128

    def ssq_body(r, carry):
        rows = pl.ds(pl.multiple_of(r * SUB, SUB), SUB)
        acc = None
        for t in range(lane_tiles):
            v = x_ref[rows, t * 128:(t + 1) * 128]
            acc = v * v if acc is None else acc + v * v
        r_scr[rows, :] = acc
        return carry

    lax.fori_loop(0, tm // SUB, ssq_body, 0, unroll=4)
    ssq = jnp.sum(r_scr[...], axis=-1, keepdims=True)
    r_scr[...] = jnp.broadcast_to(lax.rsqrt(ssq * (1.0 / d) + EPS), (tm, 128))
    gs_scr[0] = jnp.broadcast_to(nw * (1.0 + scale), (SUB, d))
    gs_scr[1] = jnp.broadcast_to(shift, (SUB, d))

    def out_body(r, carry):
        r0 = pl.multiple_of(r * ROW_BLK, ROW_BLK)
        halves = [pl.ds(pl.multiple_of(r0 + k * SUB, SUB), SUB) for k in range(ROW_BLK // SUB)]
        invs = [r_scr[rows, :] for rows in halves]
        for t in range(lane_tiles):
            cols = slice(t * 128, (t + 1) * 128)
            gain = gs_scr[0, :, cols]
            shf = gs_scr[1, :, cols]
            xs = [x_ref[rows, cols] for rows in halves]
            if copy_ref is not None:
                for rows, xv in zip(halves, xs):
                    copy_ref[rows, cols] = xv
            parts = [xv * inv * gain + shf for xv, inv in zip(xs, invs)]
            h_ref[pl.ds(r0, ROW_BLK), cols] = jnp.concatenate(parts, axis=0).astype(bf16)
        return carry

    lax.fori_loop(0, tm // ROW_BLK, out_body, 0, unroll=2)


def _mod_kernel(c_ref, w_ref, b_ref, o_ref):
    c = c_ref[...]
    s = (c * _sigmoid(c)).astype(bf16)
    o_ref[...] = jnp.dot(s, w_ref[...].astype(bf16), preferred_element_type=f32) + b_ref[...]


def _modulation(cond, w_mod, b_mod):
    tn = 1024
    n6 = 6 * D_MODEL
    return pl.pallas_call(
        _mod_kernel,
        grid=(DEPTH, n6 // tn),
        in_specs=[
            pl.BlockSpec((COND_ROWS, D_MODEL), lambda l, j: (0, 0)),
            pl.BlockSpec((None, D_MODEL, tn), lambda l, j: (l, 0, j)),
            pl.BlockSpec((None, 1, tn), lambda l, j: (l, 0, j)),
        ],
        out_specs=pl.BlockSpec((None, COND_ROWS, tn), lambda l, j: (l, 0, j)),
        out_shape=jax.ShapeDtypeStruct((DEPTH, COND_ROWS, n6), f32),
        compiler_params=_cp(2),
        name="modulation",
    )(cond, w_mod, b_mod.reshape(DEPTH, 1, n6))


def _proj_kernel(x_ref, mod_ref, nw_ref, w_ref, o_ref, h_ref, r_scr, gs_scr):
    @pl.when(pl.program_id(1) == 0)
    def _():
        _modnorm_rows(x_ref, nw_ref[...], mod_ref[0:1, :], mod_ref[1:2, :], h_ref, r_scr, gs_scr)

    o_ref[...] = jnp.dot(h_ref[...], w_ref[...], preferred_element_type=f32)


def _proj_in(x, mod, nw, w_in, layer, grp):
    tn = 1536
    return pl.pallas_call(
        _proj_kernel,
        grid=(grp.n // TM, ZW // tn),
        in_specs=[
            pl.BlockSpec((TM, D_MODEL), lambda i, j: (i, 0)),
            pl.BlockSpec((None, None, 6, D_MODEL), lambda i, j: (layer, _mod_row(grp, i), 0, 0)),
            pl.BlockSpec((None, 1, D_MODEL), lambda i, j: (layer, 0, 0)),
            pl.BlockSpec((None, D_MODEL, tn), lambda i, j: (layer, 0, j)),
        ],
        out_specs=[pl.BlockSpec((TM, tn), lambda i, j: (i, j)),
                   pl.BlockSpec((TM, D_MODEL), lambda i, j: (i, 0))],
        out_shape=[jax.ShapeDtypeStruct((grp.n, ZW), f32), jax.ShapeDtypeStruct((grp.n, D_MODEL), bf16)],
        scratch_shapes=[pltpu.VMEM((TM, 128), f32), pltpu.VMEM((2, SUB, D_MODEL), f32)],
        compiler_params=_cp(2),
        name="proj_in",
    )(x, mod, nw, w_in)


def _ret_kernel(*refs, nc, has_s0, emit_state):
    q_ref, k_ref, v_ref, g_ref, lg_ref, gn_ref = refs[:6]
    pos = 6
    s0_ref = None
    if has_s0:
        s0_ref = refs[pos]
        pos += 1
    o_ref = refs[pos]
    pos += 1
    sout_ref = None
    if emit_state:
        sout_ref = refs[pos]
        pos += 1
    w_scr, kv_scr = refs[pos], refs[pos + 1]

    C = CHUNK
    H = RET_HEADS
    DEC, W_IN_F, W_IN_B, W_END_F, W_END_B = range(5)
    tn_dims = (((0,), (0,)), ((), ()))
    nt_dims = (((1,), (1,)), ((), ()))

    @pl.when(pl.program_id(0) == 0)
    def _():
        ii = lax.broadcasted_iota(jnp.int32, (C, C), 0).astype(f32)
        jj = lax.broadcasted_iota(jnp.int32, (C, C), 1).astype(f32)
        rel = ii - jj
        for h in range(H):
            lgf = lg_ref[h, 0:1, :]
            lgb = lg_ref[h, 1:2, :]
            w_scr[h, DEC] = (jnp.where(rel >= 0, jnp.exp(jnp.maximum(rel, 0.0) * lgf), 0.0)
                             + jnp.where(rel <= 0, jnp.exp(jnp.maximum(-rel, 0.0) * lgb), 0.0))
            w_scr[h, W_IN_F] = jnp.exp((ii + 1.0) * lgf)
            w_scr[h, W_IN_B] = jnp.exp((C - ii) * lgb)
            w_scr[h, W_END_F] = jnp.exp((C - 1.0 - ii) * lgf)
            w_scr[h, W_END_B] = jnp.exp(ii * lgb)

    def kv_body(n, carry):
        r = pl.multiple_of(n * C, C)
        for h in range(H):
            cols = slice(h * C, (h + 1) * C)
            k = k_ref[pl.ds(r, C), cols]
            vb = v_ref[pl.ds(r, C), cols].astype(bf16)
            kw = jnp.concatenate([k * w_scr[h, W_END_F], k * w_scr[h, W_END_B]], axis=1).astype(bf16)
            kv = lax.dot_general(kw, vb, tn_dims, preferred_element_type=f32)
            kv_scr[0, n * H + h] = kv[:C]
            kv_scr[1, n * H + h] = kv[C:]
        return carry

    lax.fori_loop(0, nc, kv_body, 0)

    for h in range(H):
        cd_f = jnp.exp(C * lg_ref[h, 0:1, :])
        cd_b = jnp.exp(C * lg_ref[h, 1:2, :])

        def fwd_body(n, s, h=h, cd_f=cd_f):
            kv = kv_scr[0, n * H + h]
            kv_scr[0, n * H + h] = s
            return cd_f * s + kv

        def bwd_body(i, s, h=h, cd_b=cd_b):
            n = nc - 1 - i
            kv = kv_scr[1, n * H + h]
            kv_scr[1, n * H + h] = s
            return cd_b * s + kv

        s_f = lax.fori_loop(0, nc, fwd_body, s0_ref[0, h] if has_s0 else jnp.zeros((C, C), f32))
        s_b = lax.fori_loop(0, nc, bwd_body, s0_ref[1, h] if has_s0 else jnp.zeros((C, C), f32))
        if emit_state:
            sout_ref[0, h] = s_f
            sout_ref[1, h] = s_b

    def out_body(n, carry):
        r = pl.multiple_of(n * C, C)
        for h in range(H):
            cols = slice(h * C, (h + 1) * C)
            q = q_ref[pl.ds(r, C), cols] * (RET_DK ** -0.5)
            kb = k_ref[pl.ds(r, C), cols].astype(bf16)
            vb = v_ref[pl.ds(r, C), cols].astype(bf16)
            sc = lax.dot_general(q.astype(bf16), kb, nt_dims, preferred_element_type=f32) * w_scr[h, DEC]
            lhs = jnp.concatenate([sc, q * w_scr[h, W_IN_F], q * w_scr[h, W_IN_B]], axis=1).astype(bf16)
            rhs = jnp.concatenate([vb, kv_scr[0, n * H + h].astype(bf16), kv_scr[1, n * H + h].astype(bf16)], axis=0)
            o = jnp.dot(lhs, rhs, preferred_element_type=f32)
            mu = jnp.mean(o, axis=-1, keepdims=True)
            d = o - mu
            var = jnp.mean(d * d, axis=-1, keepdims=True)
            on = d * lax.rsqrt(var + EPS) * gn_ref[:, cols]
            g = g_ref[pl.ds(r, C), cols]
            o_ref[pl.ds(r, C), cols] = (g * _sigmoid(g) * on).astype(bf16)
        return carry

    lax.fori_loop(0, nc, out_body, 0, unroll=min(nc, 4))


def _retention(z, lg, gn, s0, grp):
    seq = grp.seq
    W = RET_HEADS * 128
    in_specs = [
        pl.BlockSpec((seq, W), lambda b: (b, RQ // W)),
        pl.BlockSpec((seq, W), lambda b: (b, RK // W)),
        pl.BlockSpec((seq, W), lambda b: (b, RV // W)),
        pl.BlockSpec((seq, W), lambda b: (b, RG // W)),
        pl.BlockSpec((RET_HEADS, 2, 128), lambda b: (0, 0, 0)),
        pl.BlockSpec((1, W), lambda b: (0, 0)),
    ]
    args = [z, z, z, z, lg, gn]
    st_spec = pl.BlockSpec((None, 2, RET_HEADS, 128, 128), lambda b: (b, 0, 0, 0, 0))
    if s0 is not None:
        in_specs.append(st_spec)
        args.append(s0)
    out_specs = [pl.BlockSpec((seq, W), lambda b: (b, 0))]
    out_shape = [jax.ShapeDtypeStruct((grp.n, W), bf16)]
    if grp.ctx:
        out_specs.append(st_spec)
        out_shape.append(jax.ShapeDtypeStruct((grp.nb, 2, RET_HEADS, 128, 128), f32))
    nc = seq // CHUNK
    res = pl.pallas_call(
        functools.partial(_ret_kernel, nc=nc, has_s0=s0 is not None, emit_state=grp.ctx),
        grid=(grp.nb,),
        in_specs=in_specs,
        out_specs=out_specs,
        out_shape=out_shape,
        scratch_shapes=[pltpu.VMEM((RET_HEADS, 5, 128, 128), f32),
                        pltpu.VMEM((2, nc * RET_HEADS, 128, 128), f32)],
        compiler_params=_cp(1),
        name="retention",
    )(*args)
    return res if grp.ctx else (res[0], None)


def _gather_tm(src_ref, nt, dst_ref, row0):
    w = src_ref.shape[-1]

    def body(i, carry):
        v = src_ref[:, pl.ds(pl.multiple_of(i * SUB, SUB), SUB), :]
        rows = pl.ds(pl.multiple_of(row0 + i * SUB * SUB, SUB), SUB * SUB)
        dst_ref[rows, :] = jnp.swapaxes(v, 0, 1).reshape(SUB * SUB, w)
        return carry

    lax.fori_loop(0, nt // SUB, body, 0, unroll=2)


def _scatter_bm(src_ref, nt, dst_ref):
    w = src_ref.shape[-1]

    def body(i, carry):
        v = src_ref[pl.ds(pl.multiple_of(i * SUB * SUB, SUB * SUB), SUB * SUB), :].reshape(SUB, SUB, w)
        dst_ref[:, pl.ds(pl.multiple_of(i * SUB, SUB), SUB), :] = jnp.swapaxes(v, 0, 1)
        return carry

    lax.fori_loop(0, nt // SUB, body, 0, unroll=2)


def _lru_kernel(*refs, tc, nch, direction, has_s0):
    xprev_ref, xcur_ref, xnext_ref, cw_ref, cb_ref, wa_ref, ba_ref, wx_ref, bx_ref, sp_ref = refs[:10]
    pos = 10
    s0_ref = None
    if has_s0:
        s0_ref = refs[pos]
        pos += 1
    if direction == 1:
        gate_ref, hf_ref = refs[pos], refs[pos + 1]
        pos += 2
    out_ref, fin_ref = refs[pos], refs[pos + 1]
    xe_scr, a_scr, b_scr, h_scr = refs[pos + 2: pos + 6]
    if direction == 1:
        g_scr, y_scr = refs[pos + 6], refs[pos + 7]

    R = tc * SUB
    W = LRU_WIDTH
    j = pl.program_id(1)
    c = j if direction == 0 else nch - 1 - j

    @pl.when(j == 0)
    def _():
        h_scr[...] = s0_ref[...] if has_s0 else jnp.zeros((SUB, W), f32)

    for t in range(2):
        v = xprev_ref[:, SUB - 2 + t:SUB - 1 + t, :].reshape(SUB, W)
        xe_scr[t * SUB:(t + 1) * SUB, :] = jnp.where(c > 0, v, 0.0)
    _gather_tm(xcur_ref, tc, xe_scr, 2 * SUB)
    xe_scr[2 * SUB + R:3 * SUB + R, :] = jnp.where(c < nch - 1, xnext_ref[:, 0:1, :].reshape(SUB, W), 0.0)
    if direction == 1:
        _gather_tm(gate_ref, tc, g_scr, 0)

    xc = cb_ref[...] + xe_scr[0:R, :] * cw_ref[0:1, :]
    for t in range(1, CONV_W):
        xc = xc + xe_scr[t * SUB:t * SUB + R, :] * cw_ref[t:t + 1, :]

    bd = W // LRU_BLOCKS
    for n in range(LRU_BLOCKS):
        sl = slice(n * bd, (n + 1) * bd)
        xs = xc[:, sl]
        xb = xs.astype(bf16)
        tr = jnp.tanh(jnp.dot(xb, wa_ref[n], preferred_element_type=f32) + ba_ref[:, sl])
        ti = jnp.tanh(jnp.dot(xb, wx_ref[n], preferred_element_type=f32) + bx_ref[:, sl])
        log_a = sp_ref[:, sl] * (tr + 1.0)
        a = jnp.exp(log_a)
        a_scr[:, sl] = a
        t = -jnp.tanh(log_a) * (a * a + 1.0)
        root = jnp.where(t > 0.0, t * lax.rsqrt(t), 0.0)
        b_scr[:, sl] = (0.5 * root) * ((ti + 1.0) * xs)

    def step(s, h):
        t = s if direction == 0 else tc - 1 - s
        r0 = pl.multiple_of(t * SUB, SUB)
        h = a_scr[pl.ds(r0, SUB), :] * h + b_scr[pl.ds(r0, SUB), :]
        b_scr[pl.ds(r0, SUB), :] = h
        return h

    h = lax.fori_loop(0, tc, step, h_scr[...], unroll=8)
    h_scr[...] = h
    fin_ref[...] = h
    if direction == 0:
        out_ref[...] = b_scr[...]
    else:
        b_scr[...] = jax.nn.gelu(g_scr[...]) * (hf_ref[...] + b_scr[...])
        _scatter_bm(b_scr, tc, y_scr)
        out_ref[...] = y_scr[...].astype(bf16)


def _lru_dir(z3, hf_tm, s0, prm, grp, direction):
    nb, seq = grp.nb, grp.seq
    nbg = nb // SUB
    tc = min(LRU_TC, seq)
    nch = seq // tc
    R = tc * SUB
    W = LRU_WIDTH

    def cidx(j):
        return j if direction == 0 else nch - 1 - j

    in_specs = [
        pl.BlockSpec((SUB, SUB, W), lambda g, j: (g, jnp.maximum(cidx(j) * (tc // SUB) - 1, 0), LX // W)),
        pl.BlockSpec((SUB, tc, W), lambda g, j: (g, cidx(j), LX // W)),
        pl.BlockSpec((SUB, SUB, W), lambda g, j: (g, jnp.minimum((cidx(j) + 1) * (tc // SUB), seq // SUB - 1), LX // W)),
        pl.BlockSpec((CONV_W, W), lambda g, j: (0, 0)),
        pl.BlockSpec((1, W), lambda g, j: (0, 0)),
        pl.BlockSpec((LRU_BLOCKS, 128, 128), lambda g, j: (0, 0, 0)),
        pl.BlockSpec((1, W), lambda g, j: (0, 0)),
        pl.BlockSpec((LRU_BLOCKS, 128, 128), lambda g, j: (0, 0, 0)),
        pl.BlockSpec((1, W), lambda g, j: (0, 0)),
        pl.BlockSpec((1, W), lambda g, j: (0, 0)),
    ]
    args = [z3, z3, z3, prm["cw"], prm["cb"], prm["wa"][direction], prm["ba"][direction],
            prm["wx"][direction], prm["bx"][direction], prm["sp"][direction]]
    st_spec = pl.BlockSpec((SUB, W), lambda g, j: (g, 0))
    if s0 is not None:
        in_specs.append(st_spec)
        args.append(s0)
    tm_spec = pl.BlockSpec((None, R, W), lambda g, j: (g, cidx(j), 0))
    scratch = [pltpu.VMEM((R + 3 * SUB, W), f32), pltpu.VMEM((R, W), f32),
               pltpu.VMEM((R, W), f32), pltpu.VMEM((SUB, W), f32)]
    if direction == 0:
        out_spec = tm_spec
        out_sds = jax.ShapeDtypeStruct((nbg, seq * SUB, W), f32)
    else:
        in_specs += [pl.BlockSpec((SUB, tc, W), lambda g, j: (g, cidx(j), LG // W)), tm_spec]
        args += [z3, hf_tm]
        out_spec = pl.BlockSpec((SUB, tc, W), lambda g, j: (g, cidx(j), 0))
        out_sds = jax.ShapeDtypeStruct((nb, seq, W), bf16)
        scratch += [pltpu.VMEM((R, W), f32), pltpu.VMEM((SUB, tc, W), f32)]
    return pl.pallas_call(
        functools.partial(_lru_kernel, tc=tc, nch=nch, direction=direction, has_s0=s0 is not None),
        grid=(nbg, nch),
        in_specs=in_specs,
        out_specs=[out_spec, st_spec],
        out_shape=[out_sds, jax.ShapeDtypeStruct((nb, W), f32)],
        scratch_shapes=scratch,
        compiler_params=_cp(2),
        name="rglru_dir%d" % direction,
    )(*args)


def _s5_kernel(*refs, tc, nch, direction, has_s0):
    u_ref, bd_ref, a_ref, cd_ref = refs[:4]
    pos = 4
    s0_ref = None
    if has_s0:
        s0_ref = refs[pos]
        pos += 1
    if direction == 1:
        yf_ref, dv_ref, wg_ref, bg_ref = refs[pos:pos + 4]
        pos += 4
    out_ref, fin_ref = refs[pos], refs[pos + 1]
    u_scr, ub_scr, hs_scr, h_scr = refs[pos + 2: pos + 6]
    if direction == 1:
        y_scr = refs[pos + 6]

    j = pl.program_id(1)

    @pl.when(j == 0)
    def _():
        h_scr[...] = s0_ref[...] if has_s0 else jnp.zeros((SUB, 2 * SSM_N), f32)

    _gather_tm(u_ref, tc, u_scr, 0)
    ub_scr[...] = u_scr[...].astype(bf16)

    cbw = SSM_CB
    for cb in range(SSM_N // cbw):
        cre = slice(cb * cbw, (cb + 1) * cbw)
        cim = slice(SSM_N + cb * cbw, SSM_N + (cb + 1) * cbw)
        cin = slice(cb * SSM_CI, (cb + 1) * SSM_CI)
        hs_scr[:, cre] = jnp.dot(ub_scr[:, cin], bd_ref[0, cb], preferred_element_type=f32)
        hs_scr[:, cim] = jnp.dot(ub_scr[:, cin], bd_ref[1, cb], preferred_element_type=f32)
        a_re = jnp.broadcast_to(a_ref[0:1, cre], (SUB, cbw))
        a_im = jnp.broadcast_to(a_ref[1:2, cre], (SUB, cbw))

        def step(s, carry):
            hr, hi = carry
            t = s if direction == 0 else tc - 1 - s
            r0 = pl.multiple_of(t * SUB, SUB)
            nr = a_re * hr - a_im * hi + hs_scr[pl.ds(r0, SUB), cre]
            ni = a_re * hi + a_im * hr + hs_scr[pl.ds(r0, SUB), cim]
            hs_scr[pl.ds(r0, SUB), cre] = nr
            hs_scr[pl.ds(r0, SUB), cim] = ni
            return nr, ni

        hr, hi = lax.fori_loop(0, tc, step, (h_scr[:, cre], h_scr[:, cim]), unroll=True)
        h_scr[:, cre] = hr
        h_scr[:, cim] = hi

    fin_ref[...] = h_scr[...]
    y_parts = []
    for cb in range(SSM_N // cbw):
        cre = slice(cb * cbw, (cb + 1) * cbw)
        cim = slice(SSM_N + cb * cbw, SSM_N + (cb + 1) * cbw)
        y_parts.append(jnp.dot(hs_scr[:, cre].astype(bf16), cd_ref[0, cb], preferred_element_type=f32)
                       + jnp.dot(hs_scr[:, cim].astype(bf16), cd_ref[1, cb], preferred_element_type=f32))
    y = jnp.concatenate(y_parts, axis=1)
    if direction == 0:
        out_ref[...] = y
    else:
        yy = jax.nn.gelu(yf_ref[...] + y + dv_ref[...] * u_scr[...])
        gl = jnp.dot(yy.astype(bf16), wg_ref[...], preferred_element_type=f32) + bg_ref[...]
        u_scr[...] = yy * _sigmoid(gl)
        _scatter_bm(u_scr, tc, y_scr)
        out_ref[...] = y_scr[...].astype(bf16)


def _s5_dir(z3, yf_tm, s0, prm, grp, direction):
    nb, seq = grp.nb, grp.seq
    nbg = nb // SUB
    tc = min(SSM_TC, seq)
    nch = seq // tc
    R = tc * SUB
    W = SSM_WIDTH
    ncb = SSM_N // SSM_CB

    def cidx(j):
        return j if direction == 0 else nch - 1 - j

    tm_spec = pl.BlockSpec((None, R, W), lambda g, j: (g, cidx(j), 0))
    st_spec = pl.BlockSpec((SUB, 2 * SSM_N), lambda g, j: (g, 0))
    in_specs = [
        pl.BlockSpec((SUB, tc, W), lambda g, j: (g, cidx(j), SU // W)),
        pl.BlockSpec((2, ncb, SSM_CI, SSM_CB), lambda g, j: (0, 0, 0, 0)),
        pl.BlockSpec((2, SSM_N), lambda g, j: (0, 0)),
        pl.BlockSpec((2, ncb, SSM_CB, SSM_CI), lambda g, j: (0, 0, 0, 0)),
    ]
    args = [z3, prm["bd"][direction], prm["a"][direction], prm["cd"][direction]]
    if s0 is not None:
        in_specs.append(st_spec)
        args.append(s0)
    scratch = [pltpu.VMEM((R, W), f32), pltpu.VMEM((R, W), bf16), pltpu.VMEM((R, 2 * SSM_N), f32),
               pltpu.VMEM((SUB, 2 * SSM_N), f32)]
    if direction == 0:
        out_spec = tm_spec
        out_sds = jax.ShapeDtypeStruct((nbg, seq * SUB, W), f32)
    else:
        in_specs += [tm_spec, pl.BlockSpec((1, W), lambda g, j: (0, 0)),
                     pl.BlockSpec((W, W), lambda g, j: (0, 0)), pl.BlockSpec((1, W), lambda g, j: (0, 0))]
        args += [yf_tm, prm["d"], prm["wglu"], prm["bglu"]]
        out_spec = pl.BlockSpec((SUB, tc, W), lambda g, j: (g, cidx(j), 0))
        out_sds = jax.ShapeDtypeStruct((nb, seq, W), bf16)
        scratch.append(pltpu.VMEM((SUB, tc, W), f32))
    return pl.pallas_call(
        functools.partial(_s5_kernel, tc=tc, nch=nch, direction=direction, has_s0=s0 is not None),
        grid=(nbg, nch),
        in_specs=in_specs,
        out_specs=[out_spec, st_spec],
        out_shape=[out_sds, jax.ShapeDtypeStruct((nb, 2 * SSM_N), f32)],
        scratch_shapes=scratch,
        compiler_params=_cp(2),
        name="s5_dir%d" % direction,
    )(*args)


def _rope(x, cos, sin_signed):
    return x * cos + pltpu.roll(x, HEAD_DIM // 2, 1) * sin_signed


def _unit_rms(x, w):
    return x * lax.rsqrt(jnp.mean(x * x, axis=-1, keepdims=True) + EPS) * w


def _attn_kernel(*refs, seq, latent):
    q_ref, k_ref, v_ref, qw_ref, kw_ref, sink_ref = refs[:6]
    pos = 6
    if latent:
        cos_ref, sin_ref, ck_ref, cv_ref = refs[pos:pos + 4]
        pos += 4
    o_ref = refs[pos]
    pos += 1
    if not latent:
        kout_ref, vout_ref = refs[pos], refs[pos + 1]
        pos += 2
    kb_scr, vb_scr = refs[pos], refs[pos + 1]
    pos += 2
    if latent:
        ckb_scr, cvb_scr = refs[pos], refs[pos + 1]

    HD = HEAD_DIM
    qb = pl.program_id(1)

    @pl.when(qb == 0)
    def _():
        for hk in range(ATT_KV_HEADS):
            sl = slice(hk * HD, (hk + 1) * HD)
            kn = _unit_rms(k_ref[:, sl], kw_ref[...])
            if latent:
                kn = _rope(kn, cos_ref[...], sin_ref[...])
            else:
                kout_ref[:, sl] = kn
            kb_scr[:, sl] = kn.astype(bf16)
        v = v_ref[...]
        vb_scr[...] = v.astype(bf16)
        if latent:
            ckb_scr[...] = ck_ref[...].astype(bf16)
            cvb_scr[...] = cv_ref[...].astype(bf16)
        else:
            vout_ref[...] = v

    q0 = pl.multiple_of(qb * CHUNK, CHUNK)
    if latent:
        win = 3 * CHUNK
        start = pl.multiple_of(jnp.clip(q0 - CHUNK, 0, seq - win), CHUNK)
        qpos = q0 + (lax.broadcasted_iota(jnp.int32, (2 * CHUNK, win), 0) & (CHUNK - 1))
        kpos = start + lax.broadcasted_iota(jnp.int32, (2 * CHUNK, win), 1)
        band = jnp.abs(qpos - kpos) <= WINDOW
        cos_q = cos_ref[pl.ds(q0, CHUNK), :]
        sin_q = sin_ref[pl.ds(q0, CHUNK), :]
    else:
        win = seq
        start = 0
    nt_dims = (((1,), (1,)), ((), ()))
    row = lax.broadcasted_iota(jnp.int32, (2 * CHUNK, 1), 0)

    def lane_tiles(x):
        return [x[:, t * HD:(t + 1) * HD] for t in range(x.shape[1] // HD)]

    def fold(op, tiles):
        acc = tiles[0]
        for t in tiles[1:]:
            acc = op(acc, t)
        return acc

    for hk in range(ATT_KV_HEADS):
        sl = slice(hk * HD, (hk + 1) * HD)
        qs = []
        for g in range(2):
            hq = hk * 2 + g
            qn = _unit_rms(q_ref[:, hq * HD:(hq + 1) * HD], qw_ref[...])
            if latent:
                qn = _rope(qn, cos_q, sin_q)
            qs.append((qn * (HD ** -0.5)).astype(bf16))
        qg = jnp.concatenate(qs, axis=0)
        snk = jnp.where(row < CHUNK, sink_ref[hk * 2], sink_ref[hk * 2 + 1])
        s1 = lax.dot_general(qg, kb_scr[pl.ds(start, win), sl], nt_dims, preferred_element_type=f32)
        if latent:
            s1 = jnp.where(band, s1, NEG_INF)
            s2 = lax.dot_general(qg, ckb_scr[:, sl], nt_dims, preferred_element_type=f32)
        s_tiles = lane_tiles(s1) + (lane_tiles(s2) if latent else [])
        m = jnp.maximum(jnp.max(fold(jnp.maximum, s_tiles), axis=-1, keepdims=True), snk)
        p1 = jnp.exp(s1 - m)
        if latent:
            p2 = jnp.exp(s2 - m)
        p_tiles = lane_tiles(p1) + (lane_tiles(p2) if latent else [])
        den = jnp.sum(fold(jnp.add, p_tiles), axis=-1, keepdims=True) + jnp.exp(snk - m)
        o = jnp.dot((p1 / den).astype(bf16), vb_scr[pl.ds(start, win), sl], preferred_element_type=f32)
        if latent:
            o += jnp.dot((p2 / den).astype(bf16), cvb_scr[:, sl], preferred_element_type=f32)
        for g in range(2):
            hq = hk * 2 + g
            o_ref[:, hq * HD:(hq + 1) * HD] = o[g * CHUNK:(g + 1) * CHUNK].astype(bf16)


def _attention(z, qw, kw, sink, rope, cache, grp):
    nb, seq = grp.nb, grp.seq
    latent = not grp.ctx
    nqb = seq // CHUNK
    kvw = ATT_KV_HEADS * HEAD_DIM
    in_specs = [
        pl.BlockSpec((CHUNK, 512), lambda b, i: (b * nqb + i, AQ // 512)),
        pl.BlockSpec((seq, kvw), lambda b, i: (b, AK // kvw)),
        pl.BlockSpec((seq, kvw), lambda b, i: (b, AV // kvw)),
        pl.BlockSpec((1, HEAD_DIM), lambda b, i: (0, 0)),
        pl.BlockSpec((1, HEAD_DIM), lambda b, i: (0, 0)),
        pl.BlockSpec(memory_space=pltpu.SMEM),
    ]
    args = [z, z, z, qw, kw, sink]
    scratch = [pltpu.VMEM((seq, kvw), bf16), pltpu.VMEM((seq, kvw), bf16)]
    out_specs = [pl.BlockSpec((CHUNK, 512), lambda b, i: (b * nqb + i, 0))]
    out_shape = [jax.ShapeDtypeStruct((nb * seq, ATT_Q_HEADS * HEAD_DIM), bf16)]
    if latent:
        in_specs += [pl.BlockSpec((seq, HEAD_DIM), lambda b, i: (0, 0)),
                     pl.BlockSpec((seq, HEAD_DIM), lambda b, i: (0, 0)),
                     pl.BlockSpec((None, PAST_LEN, kvw), lambda b, i: (b, 0, 0)),
                     pl.BlockSpec((None, PAST_LEN, kvw), lambda b, i: (b, 0, 0))]
        args += [rope[0], rope[1], cache[0], cache[1]]
        scratch += [pltpu.VMEM((PAST_LEN, kvw), bf16), pltpu.VMEM((PAST_LEN, kvw), bf16)]
    else:
        kv_out = pl.BlockSpec((seq, kvw), lambda b, i: (b, 0))
        out_specs += [kv_out, kv_out]
        out_shape += [jax.ShapeDtypeStruct((nb * seq, kvw), f32)] * 2
    return pl.pallas_call(
        functools.partial(_attn_kernel, seq=seq, latent=latent),
        grid=(nb, nqb),
        in_specs=in_specs,
        out_specs=out_specs,
        out_shape=out_shape,
        scratch_shapes=scratch,
        compiler_params=_cp(2),
        name="attention_lat" if latent else "attention_ctx",
    )(*args)


def _merge_kernel(h_ref, o0, o1, o2, o3, g0, g1, g2, g3, wbr_ref, out_ref):
    h = h_ref[...]
    acc = None
    for b, (o_ref, g_ref) in enumerate(((o0, g0), (o1, g1), (o2, g2), (o3, g3))):
        gate = _sigmoid(jnp.dot(h, g_ref[...], preferred_element_type=f32))
        t = gate * jnp.dot(o_ref[...], wbr_ref[b], preferred_element_type=f32)
        acc = t if acc is None else acc + t
    out_ref[...] = acc.astype(bf16)


def _merge(h, branches, w_in, w_br, layer, grp):
    tn = 512
    g0 = ZW // tn
    gb = D_MODEL // tn
    o_spec = pl.BlockSpec((TM, BRANCH_WIDTH), lambda i, j: (i, 0))

    def gate_spec(b):
        return pl.BlockSpec((None, D_MODEL, tn), lambda i, j: (layer, 0, g0 + b * gb + j))

    return pl.pallas_call(
        _merge_kernel,
        grid=(grp.n // TM, D_MODEL // tn),
        in_specs=[
            pl.BlockSpec((TM, D_MODEL), lambda i, j: (i, 0)),
            o_spec, o_spec, o_spec, o_spec,
            gate_spec(0), gate_spec(1), gate_spec(2), gate_spec(3),
            pl.BlockSpec((None, N_BRANCH, BRANCH_WIDTH, tn), lambda i, j: (layer, 0, 0, j)),
        ],
        out_specs=pl.BlockSpec((TM, tn), lambda i, j: (i, j)),
        out_shape=jax.ShapeDtypeStruct((grp.n, D_MODEL), bf16),
        compiler_params=_cp(2),
        name="merge",
    )(h, *branches, w_in, w_in, w_in, w_in, w_br)


def _outproj_kernel(x_ref, mod_ref, m_ref, w_ref, o_ref):
    o_ref[...] = x_ref[...] + mod_ref[2:3, :] * jnp.dot(m_ref[...], w_ref[...], preferred_element_type=f32)


def _out_proj(x, mod, merged, w_out, layer, grp):
    tm = OUT_TM
    return pl.pallas_call(
        _outproj_kernel,
        grid=(grp.n // tm,),
        in_specs=[
            pl.BlockSpec((tm, D_MODEL), lambda i: (i, 0)),
            pl.BlockSpec((None, None, 6, D_MODEL), lambda i: (layer, _mod_row(grp, i, tm), 0, 0)),
            pl.BlockSpec((tm, D_MODEL), lambda i: (i, 0)),
            pl.BlockSpec((None, D_MODEL, D_MODEL), lambda i: (layer, 0, 0)),
        ],
        out_specs=pl.BlockSpec((tm, D_MODEL), lambda i: (i, 0)),
        out_shape=jax.ShapeDtypeStruct((grp.n, D_MODEL), f32),
        compiler_params=_cp(1),
        name="out_proj",
    )(x, mod, merged, w_out)


def _ffn_kernel(x_ref, mod_ref, nw_ref, w1_ref, w2_ref, o_ref, h_scr, r_scr, gs_scr):
    j = pl.program_id(1)

    @pl.when(j == 0)
    def _():
        _modnorm_rows(x_ref, nw_ref[...], mod_ref[3:4, :], mod_ref[4:5, :], h_scr, r_scr, gs_scr, copy_ref=o_ref)

    a = jnp.maximum(jnp.dot(h_scr[...], w1_ref[...], preferred_element_type=f32), 0.0)
    a2 = (a * a).astype(bf16)
    cw = 512
    for cidx in range(D_MODEL // cw):
        cs = slice(cidx * cw, (cidx + 1) * cw)
        o_ref[:, cs] += mod_ref[5:6, cs] * jnp.dot(a2, w2_ref[:, cs], preferred_element_type=f32)


def _ffn(x, mod, nw, w1, w2, layer, grp):
    tf = 1024
    nj = D_FF // tf
    return pl.pallas_call(
        _ffn_kernel,
        grid=(grp.n // TM, nj),
        in_specs=[
            pl.BlockSpec((TM, D_MODEL), lambda i, j: (i, 0)),
            pl.BlockSpec((None, None, 6, D_MODEL), lambda i, j: (layer, _mod_row(grp, i), 0, 0)),
            pl.BlockSpec((None, 1, D_MODEL), lambda i, j: (layer, 0, 0)),
            pl.BlockSpec((None, D_MODEL, tf), lambda i, j: (layer, 0, j)),
            pl.BlockSpec((None, tf, D_MODEL), lambda i, j: (layer, j, 0)),
        ],
        out_specs=pl.BlockSpec((TM, D_MODEL), lambda i, j: (i, 0)),
        out_shape=jax.ShapeDtypeStruct((grp.n, D_MODEL), f32),
        scratch_shapes=[pltpu.VMEM((TM, D_MODEL), bf16), pltpu.VMEM((TM, 128), f32),
                        pltpu.VMEM((2, SUB, D_MODEL), f32)],
        compiler_params=_cp(2, vmem=FFN_VMEM_LIMIT),
        name="ffn",
    )(x, mod, nw, w1, w2)


def _rope_tables(seq):
    t = jnp.arange(seq)
    row = (t // GRID_W).astype(f32)
    col = (t % GRID_W).astype(f32)
    n_pairs = HEAD_DIM // 4
    freqs = ROPE_BASE ** (-jnp.arange(n_pairs, dtype=f32) / n_pairs)
    ang = jnp.concatenate([row[:, None] * freqs, col[:, None] * freqs], axis=-1)
    cos = jnp.cos(ang)
    sin = jnp.sin(ang)
    return jnp.concatenate([cos, cos], axis=-1), jnp.concatenate([-sin, sin], axis=-1)


def _block_diag_in(w):
    eye = jnp.eye(SSM_GROUPS, dtype=w.dtype)
    return jnp.einsum("gpc,gh->gchp", w, eye).reshape(SSM_WIDTH, SSM_N)


def _block_diag_out(w):
    eye = jnp.eye(SSM_GROUPS, dtype=w.dtype)
    return jnp.einsum("gcp,gh->gphc", w, eye).reshape(SSM_N, SSM_WIDTH)


def _ssm_params(a_re, a_im, log_dt, b_re, b_im, c_re, c_im):
    dt = jnp.exp(log_dt)[..., None]
    mag = jnp.exp(a_re * dt)
    ab_re = mag * jnp.cos(a_im * dt)
    ab_im = mag * jnp.sin(a_im * dt)
    den = a_re * a_re + a_im * a_im
    q_re = (((ab_re - 1.0) * a_re + ab_im * a_im) / den)[..., None]
    q_im = ((ab_im * a_re - (ab_re - 1.0) * a_im) / den)[..., None]
    bb_re = q_re * b_re - q_im * b_im
    bb_im = q_re * b_im + q_im * b_re
    a = jnp.stack([ab_re.reshape(2, SSM_N), ab_im.reshape(2, SSM_N)], axis=1)

    def diag_blocks(full, rows, cols):
        return jnp.stack([full[i * rows:(i + 1) * rows, i * cols:(i + 1) * cols] for i in range(SSM_N // SSM_CB)])

    bd = jnp.stack([jnp.stack([diag_blocks(_block_diag_in(w[d]), SSM_CI, SSM_CB) for w in (bb_re, bb_im)])
                    for d in range(2)]).astype(bf16)
    cd = jnp.stack([jnp.stack([diag_blocks(_block_diag_out(w[d]), SSM_CB, SSM_CI) for w in (c_re, -c_im)])
                    for d in range(2)]).astype(bf16)
    return a, bd, cd


def _lru_params(conv_w, conv_b, wa, ba, wx, bx, lam):
    w = LRU_WIDTH
    return {"cw": conv_w, "cb": conv_b.reshape(1, w),
            "wa": (0.5 * wa).astype(bf16), "ba": (0.5 * ba).reshape(2, 1, w),
            "wx": (0.5 * wx).astype(bf16), "bx": (0.5 * bx).reshape(2, 1, w),
            "sp": (-0.5 * LRU_C * jax.nn.softplus(-lam)).reshape(2, 1, w)}


def _mixers(z, grp, layer, prm, states):
    nb, seq = grp.nb, grp.seq
    z3 = z.reshape(nb, seq, ZW)
    s_ret0, s_lru0, s_ssm0, cache = states

    o_ret, s_ret = _retention(z, prm["ret_lg"], prm["ret_gn"], s_ret0, grp)

    hf, lru_f = _lru_dir(z3, None, None if s_lru0 is None else s_lru0[:, 0], prm["lru"], grp, 0)
    o_lru, lru_b = _lru_dir(z3, hf, None if s_lru0 is None else s_lru0[:, 1], prm["lru"], grp, 1)

    att = _attention(z, prm["att_qw"], prm["att_kw"], prm["att_sink"], prm["rope"], cache, grp)

    yf, ssm_f = _s5_dir(z3, None, None if s_ssm0 is None else s_ssm0[:, 0], prm["ssm"], grp, 0)
    o_ssm, ssm_b = _s5_dir(z3, yf, None if s_ssm0 is None else s_ssm0[:, 1], prm["ssm"], grp, 1)

    branches = [o_ret, o_lru.reshape(grp.n, LRU_WIDTH), att[0], o_ssm.reshape(grp.n, SSM_WIDTH)]
    new_state = None
    if grp.ctx:
        ssm_fin = jnp.stack([ssm_f, ssm_b], axis=1)
        new_state = (att[1].reshape(nb, seq, ATT_KV_HEADS, HEAD_DIM), att[2].reshape(nb, seq, ATT_KV_HEADS, HEAD_DIM),
                     s_ret, jnp.stack([lru_f, lru_b], axis=1),
                     ssm_fin[:, :, :SSM_N].reshape(nb, 2, SSM_GROUPS, SSM_STATE),
                     ssm_fin[:, :, SSM_N:].reshape(nb, 2, SSM_GROUPS, SSM_STATE))
    return branches, new_state


def kernel(x_prompt, x_sample, cache_attn_k, cache_attn_v, state_ret, state_lru, state_ssm_re, state_ssm_im, c, c_ctx, w_mod, b_mod, norm1, w_in, ret_decay_logit, ret_gn, lru_conv_w, lru_conv_b, lru_wa, lru_ba, lru_wx, lru_bx, lru_lambda, att_q_norm, att_k_norm, att_sink, ssm_a_re, ssm_a_im, ssm_log_dt, ssm_b_re, ssm_b_im, ssm_c_re, ssm_c_im, ssm_d, ssm_w_glu, ssm_b_glu, w_br, w_out, norm2, w_ff1, w_ff2):
    xs = {PROMPT: x_prompt.reshape(PROMPT.n, D_MODEL), SAMPLE: x_sample.reshape(SAMPLE.n, D_MODEL)}
    cond = jnp.concatenate([c, c_ctx[None, :], jnp.zeros((COND_ROWS - SAMPLE.nb - 1, D_MODEL), f32)], axis=0)
    mod = _modulation(cond, w_mod, b_mod).reshape(DEPTH, COND_ROWS, 6, D_MODEL)

    w_in_b = w_in.astype(bf16)
    w_br_b = w_br.astype(bf16)
    w_out_b = w_out.astype(bf16)
    w_ff1_b = w_ff1.astype(bf16)
    w_ff2_b = w_ff2.astype(bf16)
    norm1_r = norm1.reshape(DEPTH, 1, D_MODEL)
    norm2_r = norm2.reshape(DEPTH, 1, D_MODEL)
    rope = _rope_tables(SAMPLE.seq)
    log_gamma = -jax.nn.softplus(-ret_decay_logit)
    kvw = ATT_KV_HEADS * HEAD_DIM

    new_states = []
    for l in range(DEPTH):
        ssm_a, ssm_bd, ssm_cd = _ssm_params(ssm_a_re[l], ssm_a_im[l], ssm_log_dt[l], ssm_b_re[l], ssm_b_im[l],
                                            ssm_c_re[l], ssm_c_im[l])
        prm = {
            "ret_lg": jnp.broadcast_to(log_gamma[l].T[:, :, None], (RET_HEADS, 2, 128)),
            "ret_gn": ret_gn[l].reshape(1, RET_HEADS * 128),
            "lru": _lru_params(lru_conv_w[l], lru_conv_b[l], lru_wa[l], lru_ba[l], lru_wx[l], lru_bx[l],
                               lru_lambda[l]),
            "att_qw": att_q_norm[l].reshape(1, HEAD_DIM), "att_kw": att_k_norm[l].reshape(1, HEAD_DIM),
            "att_sink": att_sink[l], "rope": rope,
            "ssm": {"a": ssm_a, "bd": ssm_bd, "cd": ssm_cd, "d": ssm_d[l].reshape(1, SSM_WIDTH),
                    "wglu": ssm_w_glu[l].astype(bf16), "bglu": ssm_b_glu[l].reshape(1, SSM_WIDTH)},
        }
        states = {
            PROMPT: (None, None, None, None),
            SAMPLE: (state_ret[:, l], state_lru[:, l],
                     jnp.concatenate([state_ssm_re[:, l].reshape(SAMPLE.nb, 2, SSM_N),
                                      state_ssm_im[:, l].reshape(SAMPLE.nb, 2, SSM_N)], axis=-1),
                     (cache_attn_k[:, l].reshape(SAMPLE.nb, PAST_LEN, kvw),
                      cache_attn_v[:, l].reshape(SAMPLE.nb, PAST_LEN, kvw))),
        }
        for grp in (PROMPT, SAMPLE):
            x = xs[grp]
            z, h = _proj_in(x, mod, norm1_r, w_in_b, l, grp)
            branches, st = _mixers(z, grp, l, prm, states[grp])
            if grp.ctx:
                new_states.append(st)
            merged = _merge(h, branches, w_in_b, w_br_b, l, grp)
            x = _out_proj(x, mod, merged, w_out_b, l, grp)
            xs[grp] = _ffn(x, mod, norm2_r, w_ff1_b, w_ff2_b, l, grp)

    y_prompt = xs[PROMPT].reshape(PROMPT.nb, PROMPT.seq, D_MODEL)
    y_sample = xs[SAMPLE].reshape(SAMPLE.nb, SAMPLE.seq, D_MODEL)
    return (y_prompt, y_sample) + tuple(jnp.stack([st[i] for st in new_states], axis=1) for i in range(6))
```

```python
import functools
from typing import NamedTuple

import jax
import jax.numpy as jnp
from jax import lax
from jax.experimental import pallas as pl
from jax.experimental.pallas import tpu as pltpu

f32 = jnp.float32
bf16 = jnp.bfloat16

D_MODEL = 2048
DEPTH = 4
PAST_LEN = 256
GRID_W = 64
CHUNK = 128
EPS = 1e-6
NEG_INF = -1e30
RET_HEADS = 4
RET_DK = 128
LRU_WIDTH = 512
LRU_BLOCKS = 4
LRU_C = 8.0
CONV_W = 4
ATT_Q_HEADS = 4
ATT_KV_HEADS = 2
HEAD_DIM = 128
WINDOW = 128
ROPE_BASE = 10000.0
SSM_WIDTH = 512
SSM_GROUP = 16
SSM_GROUPS = 32
SSM_STATE = 64
SSM_N = SSM_GROUPS * SSM_STATE
SSM_CB = 512
SSM_CI = SSM_CB // SSM_STATE * SSM_GROUP
N_BRANCH = 4
BRANCH_WIDTH = 512
D_FF = 4 * D_MODEL
RQ, RK, RV, RG, LX, LG, AQ, AK, AV, SU, ZW = 0, 512, 1024, 1536, 2048, 2560, 3072, 3584, 3840, 4096, 4608
SUB = 8
COND_ROWS = 16
VMEM_LIMIT = 56 * 1024 * 1024
BIG_VMEM_LIMIT = 62 * 1024 * 1024

TM = 1024
OUT_TM = 512
LRU_TC = 128
SSM_TC = 128
ROW_BLK = 16


class Group(NamedTuple):
    nb: int
    seq: int
    ctx: bool

    @property
    def n(self):
        return self.nb * self.seq


PROMPT = Group(32, 256, True)
SAMPLE = Group(8, 2048, False)
CTX_ROW = SAMPLE.nb


def _cp(n_axes, vmem=VMEM_LIMIT):
    return pltpu.CompilerParams(dimension_semantics=("arbitrary",) * n_axes, vmem_limit_bytes=vmem)


def _sigmoid(x):
    return 0.5 * jnp.tanh(0.5 * x) + 0.5


def _mod_row(grp, i, tm=None):
    return CTX_ROW if grp.ctx else i // (grp.seq // (tm or TM))


def _modnorm_rows(x_ref, nw, shift, scale, h_ref, r_scr, gs_scr, copy_ref=None):
    tm, d = x_ref.shape
    lane_tiles = d // 128

    def ssq_body(r, carry):
        rows = pl.ds(pl.multiple_of(r * SUB, SUB), SUB)
        acc = None
        for t in range(lane_tiles):
            v = x_ref[rows, t * 128:(t + 1) * 128]
            acc = v * v if acc is None else acc + v * v
        r_scr[rows, :] = acc
        return carry

    lax.fori_loop(0, tm // SUB, ssq_body, 0, unroll=4)
    ssq = jnp.sum(r_scr[...], axis=-1, keepdims=True)
    r_scr[...] = jnp.broadcast_to(lax.rsqrt(ssq * (1.0 / d) + EPS), (tm, 128))
    gs_scr[0] = jnp.broadcast_to(nw * (1.0 + scale), (SUB, d))
    gs_scr[1] = jnp.broadcast_to(shift, (SUB, d))

    def out_body(r, carry):
        r0 = pl.multiple_of(r * ROW_BLK, ROW_BLK)
        halves = [pl.ds(pl.multiple_of(r0 + k * SUB, SUB), SUB) for k in range(ROW_BLK // SUB)]
        invs = [r_scr[rows, :] for rows in halves]
        for t in range(lane_tiles):
            cols = slice(t * 128, (t + 1) * 128)
            gain = gs_scr[0, :, cols]
            shf = gs_scr[1, :, cols]
            xs = [x_ref[rows, cols] for rows in halves]
            if copy_ref is not None:
                for rows, xv in zip(halves, xs):
                    copy_ref[rows, cols] = xv
            parts = [xv * inv * gain + shf for xv, inv in zip(xs, invs)]
            h_ref[pl.ds(r0, ROW_BLK), cols] = jnp.concatenate(parts, axis=0).astype(bf16)
        return carry

    lax.fori_loop(0, tm // ROW_BLK, out_body, 0, unroll=2)


def _mod_kernel(c_ref, w_ref, b_ref, o_ref):
    c = c_ref[...]
    s = (c * _sigmoid(c)).astype(bf16)
    o_ref[...] = jnp.dot(s, w_ref[...].astype(bf16), preferred_element_type=f32) + b_ref[...]


def _modulation(cond, w_mod, b_mod):
    tn = 1024
    n6 = 6 * D_MODEL
    return pl.pallas_call(
        _mod_kernel,
        grid=(DEPTH, n6 // tn),
        in_specs=[
            pl.BlockSpec((COND_ROWS, D_MODEL), lambda l, j: (0, 0)),
            pl.BlockSpec((None, D_MODEL, tn), lambda l, j: (l, 0, j)),
            pl.BlockSpec((None, 1, tn), lambda l, j: (l, 0, j)),
        ],
        out_specs=pl.BlockSpec((None, COND_ROWS, tn), lambda l, j: (l, 0, j)),
        out_shape=jax.ShapeDtypeStruct((DEPTH, COND_ROWS, n6), f32),
        compiler_params=_cp(2),
        name="modulation",
    )(cond, w_mod, b_mod.reshape(DEPTH, 1, n6))


def _proj_kernel(x_ref, mod_ref, nw_ref, w_ref, o_ref, h_ref, r_scr, gs_scr):
    @pl.when(pl.program_id(1) == 0)
    def _():
        _modnorm_rows(x_ref, nw_ref[...], mod_ref[0:1, :], mod_ref[1:2, :], h_ref, r_scr, gs_scr)

    o_ref[...] = jnp.dot(h_ref[...], w_ref[...], preferred_element_type=f32)


def _proj_in(x, mod, nw, w_in, layer, grp):
    tn = 1536
    return pl.pallas_call(
        _proj_kernel,
        grid=(grp.n // TM, ZW // tn),
        in_specs=[
            pl.BlockSpec((TM, D_MODEL), lambda i, j: (i, 0)),
            pl.BlockSpec((None, None, 6, D_MODEL), lambda i, j: (layer, _mod_row(grp, i), 0, 0)),
            pl.BlockSpec((None, 1, D_MODEL), lambda i, j: (layer, 0, 0)),
            pl.BlockSpec((D_MODEL, tn), lambda i, j: (0, j)),
        ],
        out_specs=[pl.BlockSpec((TM, tn), lambda i, j: (i, j)),
                   pl.BlockSpec((TM, D_MODEL), lambda i, j: (i, 0))],
        out_shape=[jax.ShapeDtypeStruct((grp.n, ZW), f32), jax.ShapeDtypeStruct((grp.n, D_MODEL), bf16)],
        scratch_shapes=[pltpu.VMEM((TM, 128), f32), pltpu.VMEM((2, SUB, D_MODEL), f32)],
        compiler_params=_cp(2),
        name="proj_in",
    )(x, mod, nw, w_in)


def _ret_kernel(*refs, nc, has_s0, emit_state):
    q_ref, k_ref, v_ref, g_ref, lg_ref, gn_ref = refs[:6]
    pos = 6
    s0_ref = None
    if has_s0:
        s0_ref = refs[pos]
        pos += 1
    o_ref = refs[pos]
    pos += 1
    sout_ref = None
    if emit_state:
        sout_ref = refs[pos]
        pos += 1
    w_scr, kv_scr = refs[pos], refs[pos + 1]

    C = CHUNK
    H = RET_HEADS
    DEC, W_IN_F, W_IN_B, W_END_F, W_END_B = range(5)
    tn_dims = (((0,), (0,)), ((), ()))
    nt_dims = (((1,), (1,)), ((), ()))

    @pl.when(pl.program_id(0) == 0)
    def _():
        ii = lax.broadcasted_iota(jnp.int32, (C, C), 0).astype(f32)
        jj = lax.broadcasted_iota(jnp.int32, (C, C), 1).astype(f32)
        rel = ii - jj
        for h in range(H):
            lgf = lg_ref[h, 0:1, :]
            lgb = lg_ref[h, 1:2, :]
            w_scr[h, DEC] = (jnp.where(rel >= 0, jnp.exp(jnp.maximum(rel, 0.0) * lgf), 0.0)
                             + jnp.where(rel <= 0, jnp.exp(jnp.maximum(-rel, 0.0) * lgb), 0.0))
            w_scr[h, W_IN_F] = jnp.exp((ii + 1.0) * lgf)
            w_scr[h, W_IN_B] = jnp.exp((C - ii) * lgb)
            w_scr[h, W_END_F] = jnp.exp((C - 1.0 - ii) * lgf)
            w_scr[h, W_END_B] = jnp.exp(ii * lgb)

    def kv_body(n, carry):
        r = pl.multiple_of(n * C, C)
        for h in range(H):
            cols = slice(h * C, (h + 1) * C)
            k = k_ref[pl.ds(r, C), cols]
            vb = v_ref[pl.ds(r, C), cols].astype(bf16)
            kw = jnp.concatenate([k * w_scr[h, W_END_F], k * w_scr[h, W_END_B]], axis=1).astype(bf16)
            kv = lax.dot_general(kw, vb, tn_dims, preferred_element_type=f32)
            kv_scr[0, n * H + h] = kv[:C]
            kv_scr[1, n * H + h] = kv[C:]
        return carry

    lax.fori_loop(0, nc, kv_body, 0)

    for h in range(H):
        cd_f = jnp.exp(C * lg_ref[h, 0:1, :])
        cd_b = jnp.exp(C * lg_ref[h, 1:2, :])

        def fwd_body(n, s, h=h, cd_f=cd_f):
            kv = kv_scr[0, n * H + h]
            kv_scr[0, n * H + h] = s
            return cd_f * s + kv

        def bwd_body(i, s, h=h, cd_b=cd_b):
            n = nc - 1 - i
            kv = kv_scr[1, n * H + h]
            kv_scr[1, n * H + h] = s
            return cd_b * s + kv

        s_f = lax.fori_loop(0, nc, fwd_body, s0_ref[0, h] if has_s0 else jnp.zeros((C, C), f32))
        s_b = lax.fori_loop(0, nc, bwd_body, s0_ref[1, h] if has_s0 else jnp.zeros((C, C), f32))
        if emit_state:
            sout_ref[0, h] = s_f
            sout_ref[1, h] = s_b

    def out_body(n, carry):
        r = pl.multiple_of(n * C, C)
        for h in range(H):
            cols = slice(h * C, (h + 1) * C)
            q = q_ref[pl.ds(r, C), cols] * (RET_DK ** -0.5)
            kb = k_ref[pl.ds(r, C), cols].astype(bf16)
            vb = v_ref[pl.ds(r, C), cols].astype(bf16)
            sc = lax.dot_general(q.astype(bf16), kb, nt_dims, preferred_element_type=f32) * w_scr[h, DEC]
            lhs = jnp.concatenate([sc, q * w_scr[h, W_IN_F], q * w_scr[h, W_IN_B]], axis=1).astype(bf16)
            rhs = jnp.concatenate([vb, kv_scr[0, n * H + h].astype(bf16), kv_scr[1, n * H + h].astype(bf16)], axis=0)
            o = jnp.dot(lhs, rhs, preferred_element_type=f32)
            mu = jnp.mean(o, axis=-1, keepdims=True)
            d = o - mu
            var = jnp.mean(d * d, axis=-1, keepdims=True)
            on = d * lax.rsqrt(var + EPS) * gn_ref[:, cols]
            g = g_ref[pl.ds(r, C), cols]
            o_ref[pl.ds(r, C), cols] = (g * _sigmoid(g) * on).astype(bf16)
        return carry

    lax.fori_loop(0, nc, out_body, 0, unroll=min(nc, 4))


def _retention(z, lg, gn, s0, grp):
    seq = grp.seq
    W = RET_HEADS * 128
    in_specs = [
        pl.BlockSpec((seq, W), lambda b: (b, RQ // W)),
        pl.BlockSpec((seq, W), lambda b: (b, RK // W)),
        pl.BlockSpec((seq, W), lambda b: (b, RV // W)),
        pl.BlockSpec((seq, W), lambda b: (b, RG // W)),
        pl.BlockSpec((RET_HEADS, 2, 128), lambda b: (0, 0, 0)),
        pl.BlockSpec((1, W), lambda b: (0, 0)),
    ]
    args = [z, z, z, z, lg, gn]
    st_spec = pl.BlockSpec((None, 2, RET_HEADS, 128, 128), lambda b: (b, 0, 0, 0, 0))
    if s0 is not None:
        in_specs.append(st_spec)
        args.append(s0)
    out_specs = [pl.BlockSpec((seq, W), lambda b: (b, 0))]
    out_shape = [jax.ShapeDtypeStruct((grp.n, W), bf16)]
    if grp.ctx:
        out_specs.append(st_spec)
        out_shape.append(jax.ShapeDtypeStruct((grp.nb, 2, RET_HEADS, 128, 128), f32))
    nc = seq // CHUNK
    res = pl.pallas_call(
        functools.partial(_ret_kernel, nc=nc, has_s0=s0 is not None, emit_state=grp.ctx),
        grid=(grp.nb,),
        in_specs=in_specs,
        out_specs=out_specs,
        out_shape=out_shape,
        scratch_shapes=[pltpu.VMEM((RET_HEADS, 5, 128, 128), f32),
                        pltpu.VMEM((2, nc * RET_HEADS, 128, 128), f32)],
        compiler_params=_cp(1),
        name="retention",
    )(*args)
    return res if grp.ctx else (res[0], None)


def _gather_tm(src_ref, nt, dst_ref, row0):
    w = src_ref.shape[-1]

    def body(i, carry):
        v = src_ref[:, pl.ds(pl.multiple_of(i * SUB, SUB), SUB), :]
        rows = pl.ds(pl.multiple_of(row0 + i * SUB * SUB, SUB), SUB * SUB)
        dst_ref[rows, :] = jnp.swapaxes(v, 0, 1).reshape(SUB * SUB, w)
        return carry

    lax.fori_loop(0, nt // SUB, body, 0, unroll=2)


def _scatter_bm(src_ref, nt, dst_ref):
    w = src_ref.shape[-1]

    def body(i, carry):
        v = src_ref[pl.ds(pl.multiple_of(i * SUB * SUB, SUB * SUB), SUB * SUB), :].reshape(SUB, SUB, w)
        dst_ref[:, pl.ds(pl.multiple_of(i * SUB, SUB), SUB), :] = jnp.swapaxes(v, 0, 1)
        return carry

    lax.fori_loop(0, nt // SUB, body, 0, unroll=2)


def _lru_kernel(*refs, tc, nch, direction, has_s0):
    xprev_ref, xcur_ref, xnext_ref, cw_ref, cb_ref, wa_ref, ba_ref, wx_ref, bx_ref, sp_ref = refs[:10]
    pos = 10
    s0_ref = None
    if has_s0:
        s0_ref = refs[pos]
        pos += 1
    if direction == 1:
        gate_ref, hf_ref = refs[pos], refs[pos + 1]
        pos += 2
    out_ref, fin_ref = refs[pos], refs[pos + 1]
    xe_scr, a_scr, b_scr, h_scr = refs[pos + 2: pos + 6]
    if direction == 1:
        g_scr, y_scr = refs[pos + 6], refs[pos + 7]

    R = tc * SUB
    W = LRU_WIDTH
    j = pl.program_id(1)
    c = j if direction == 0 else nch - 1 - j

    @pl.when(j == 0)
    def _():
        h_scr[...] = s0_ref[...] if has_s0 else jnp.zeros((SUB, W), f32)

    for t in range(2):
        v = xprev_ref[:, SUB - 2 + t:SUB - 1 + t, :].reshape(SUB, W)
        xe_scr[t * SUB:(t + 1) * SUB, :] = jnp.where(c > 0, v, 0.0)
    _gather_tm(xcur_ref, tc, xe_scr, 2 * SUB)
    xe_scr[2 * SUB + R:3 * SUB + R, :] = jnp.where(c < nch - 1, xnext_ref[:, 0:1, :].reshape(SUB, W), 0.0)
    if direction == 1:
        _gather_tm(gate_ref, tc, g_scr, 0)

    xc = cb_ref[...] + xe_scr[0:R, :] * cw_ref[0:1, :]
    for t in range(1, CONV_W):
        xc = xc + xe_scr[t * SUB:t * SUB + R, :] * cw_ref[t:t + 1, :]

    bd = W // LRU_BLOCKS
    for n in range(LRU_BLOCKS):
        sl = slice(n * bd, (n + 1) * bd)
        xs = xc[:, sl]
        xb = xs.astype(bf16)
        tr = jnp.tanh(jnp.dot(xb, wa_ref[n], preferred_element_type=f32) + ba_ref[:, sl])
        ti = jnp.tanh(jnp.dot(xb, wx_ref[n], preferred_element_type=f32) + bx_ref[:, sl])
        log_a = sp_ref[:, sl] * (tr + 1.0)
        a = jnp.exp(log_a)
        a_scr[:, sl] = a
        t = -jnp.tanh(log_a) * (a * a + 1.0)
        root = jnp.where(t > 0.0, t * lax.rsqrt(t), 0.0)
        b_scr[:, sl] = (0.5 * root) * ((ti + 1.0) * xs)

    def step(s, h):
        t = s if direction == 0 else tc - 1 - s
        r0 = pl.multiple_of(t * SUB, SUB)
        h = a_scr[pl.ds(r0, SUB), :] * h + b_scr[pl.ds(r0, SUB), :]
        b_scr[pl.ds(r0, SUB), :] = h
        return h

    h = lax.fori_loop(0, tc, step, h_scr[...], unroll=8)
    h_scr[...] = h
    fin_ref[...] = h
    if direction == 0:
        out_ref[...] = b_scr[...]
    else:
        b_scr[...] = jax.nn.gelu(g_scr[...]) * (hf_ref[...] + b_scr[...])
        _scatter_bm(b_scr, tc, y_scr)
        out_ref[...] = y_scr[...].astype(bf16)


def _lru_dir(z3, hf_tm, s0, prm, grp, direction):
    nb, seq = grp.nb, grp.seq
    nbg = nb // SUB
    tc = min(LRU_TC, seq)
    nch = seq // tc
    R = tc * SUB
    W = LRU_WIDTH

    def cidx(j):
        return j if direction == 0 else nch - 1 - j

    in_specs = [
        pl.BlockSpec((SUB, SUB, W), lambda g, j: (g, jnp.maximum(cidx(j) * (tc // SUB) - 1, 0), LX // W)),
        pl.BlockSpec((SUB, tc, W), lambda g, j: (g, cidx(j), LX // W)),
        pl.BlockSpec((SUB, SUB, W), lambda g, j: (g, jnp.minimum((cidx(j) + 1) * (tc // SUB), seq // SUB - 1), LX // W)),
        pl.BlockSpec((CONV_W, W), lambda g, j: (0, 0)),
        pl.BlockSpec((1, W), lambda g, j: (0, 0)),
        pl.BlockSpec((LRU_BLOCKS, 128, 128), lambda g, j: (0, 0, 0)),
        pl.BlockSpec((1, W), lambda g, j: (0, 0)),
        pl.BlockSpec((LRU_BLOCKS, 128, 128), lambda g, j: (0, 0, 0)),
        pl.BlockSpec((1, W), lambda g, j: (0, 0)),
        pl.BlockSpec((1, W), lambda g, j: (0, 0)),
    ]
    args = [z3, z3, z3, prm["cw"], prm["cb"], prm["wa"][direction], prm["ba"][direction],
            prm["wx"][direction], prm["bx"][direction], prm["sp"][direction]]
    st_spec = pl.BlockSpec((SUB, W), lambda g, j: (g, 0))
    if s0 is not None:
        in_specs.append(st_spec)
        args.append(s0)
    tm_spec = pl.BlockSpec((None, R, W), lambda g, j: (g, cidx(j), 0))
    scratch = [pltpu.VMEM((R + 3 * SUB, W), f32), pltpu.VMEM((R, W), f32),
               pltpu.VMEM((R, W), f32), pltpu.VMEM((SUB, W), f32)]
    if direction == 0:
        out_spec = tm_spec
        out_sds = jax.ShapeDtypeStruct((nbg, seq * SUB, W), f32)
    else:
        in_specs += [pl.BlockSpec((SUB, tc, W), lambda g, j: (g, cidx(j), LG // W)), tm_spec]
        args += [z3, hf_tm]
        out_spec = pl.BlockSpec((SUB, tc, W), lambda g, j: (g, cidx(j), 0))
        out_sds = jax.ShapeDtypeStruct((nb, seq, W), bf16)
        scratch += [pltpu.VMEM((R, W), f32), pltpu.VMEM((SUB, tc, W), f32)]
    return pl.pallas_call(
        functools.partial(_lru_kernel, tc=tc, nch=nch, direction=direction, has_s0=s0 is not None),
        grid=(nbg, nch),
        in_specs=in_specs,
        out_specs=[out_spec, st_spec],
        out_shape=[out_sds, jax.ShapeDtypeStruct((nb, W), f32)],
        scratch_shapes=scratch,
        compiler_params=_cp(2),
        name="rglru_dir%d" % direction,
    )(*args)


def _s5_kernel(*refs, tc, nch, direction, has_s0):
    u_ref, bd_ref, a_ref, cd_ref = refs[:4]
    pos = 4
    s0_ref = None
    if has_s0:
        s0_ref = refs[pos]
        pos += 1
    if direction == 1:
        yf_ref, dv_ref, wg_ref, bg_ref = refs[pos:pos + 4]
        pos += 4
    out_ref, fin_ref = refs[pos], refs[pos + 1]
    u_scr, ub_scr, hs_scr, h_scr = refs[pos + 2: pos + 6]
    if direction == 1:
        y_scr = refs[pos + 6]

    j = pl.program_id(1)

    @pl.when(j == 0)
    def _():
        h_scr[...] = s0_ref[...] if has_s0 else jnp.zeros((SUB, 2 * SSM_N), f32)

    _gather_tm(u_ref, tc, u_scr, 0)
    ub_scr[...] = u_scr[...].astype(bf16)

    cbw = SSM_CB
    for cb in range(SSM_N // cbw):
        cre = slice(cb * cbw, (cb + 1) * cbw)
        cim = slice(SSM_N + cb * cbw, SSM_N + (cb + 1) * cbw)
        cin = slice(cb * SSM_CI, (cb + 1) * SSM_CI)
        hs_scr[:, cre] = jnp.dot(ub_scr[:, cin], bd_ref[0, cb], preferred_element_type=f32)
        hs_scr[:, cim] = jnp.dot(ub_scr[:, cin], bd_ref[1, cb], preferred_element_type=f32)
        a_re = jnp.broadcast_to(a_ref[0:1, cre], (SUB, cbw))
        a_im = jnp.broadcast_to(a_ref[1:2, cre], (SUB, cbw))

        def step(s, carry):
            hr, hi = carry
            t = s if direction == 0 else tc - 1 - s
            r0 = pl.multiple_of(t * SUB, SUB)
            nr = a_re * hr - a_im * hi + hs_scr[pl.ds(r0, SUB), cre]
            ni = a_re * hi + a_im * hr + hs_scr[pl.ds(r0, SUB), cim]
            hs_scr[pl.ds(r0, SUB), cre] = nr
            hs_scr[pl.ds(r0, SUB), cim] = ni
            return nr, ni

        hr, hi = lax.fori_loop(0, tc, step, (h_scr[:, cre], h_scr[:, cim]), unroll=True)
        h_scr[:, cre] = hr
        h_scr[:, cim] = hi

    fin_ref[...] = h_scr[...]
    y_parts = []
    for cb in range(SSM_N // cbw):
        cre = slice(cb * cbw, (cb + 1) * cbw)
        cim = slice(SSM_N + cb * cbw, SSM_N + (cb + 1) * cbw)
        y_parts.append(jnp.dot(hs_scr[:, cre].astype(bf16), cd_ref[0, cb], preferred_element_type=f32)
                       + jnp.dot(hs_scr[:, cim].astype(bf16), cd_ref[1, cb], preferred_element_type=f32))
    y = jnp.concatenate(y_parts, axis=1)
    if direction == 0:
        out_ref[...] = y
    else:
        yy = jax.nn.gelu(yf_ref[...] + y + dv_ref[...] * u_scr[...])
        gl = jnp.dot(yy.astype(bf16), wg_ref[...], preferred_element_type=f32) + bg_ref[...]
        u_scr[...] = yy * _sigmoid(gl)
        _scatter_bm(u_scr, tc, y_scr)
        out_ref[...] = y_scr[...].astype(bf16)


def _s5_dir(z3, yf_tm, s0, prm, grp, direction):
    nb, seq = grp.nb, grp.seq
    nbg = nb // SUB
    tc = min(SSM_TC, seq)
    nch = seq // tc
    R = tc * SUB
    W = SSM_WIDTH
    ncb = SSM_N // SSM_CB

    def cidx(j):
        return j if direction == 0 else nch - 1 - j

    tm_spec = pl.BlockSpec((None, R, W), lambda g, j: (g, cidx(j), 0))
    st_spec = pl.BlockSpec((SUB, 2 * SSM_N), lambda g, j: (g, 0))
    in_specs = [
        pl.BlockSpec((SUB, tc, W), lambda g, j: (g, cidx(j), SU // W)),
        pl.BlockSpec((2, ncb, SSM_CI, SSM_CB), lambda g, j: (0, 0, 0, 0)),
        pl.BlockSpec((2, SSM_N), lambda g, j: (0, 0)),
        pl.BlockSpec((2, ncb, SSM_CB, SSM_CI), lambda g, j: (0, 0, 0, 0)),
    ]
    args = [z3, prm["bd"][direction], prm["a"][direction], prm["cd"][direction]]
    if s0 is not None:
        in_specs.append(st_spec)
        args.append(s0)
    scratch = [pltpu.VMEM((R, W), f32), pltpu.VMEM((R, W), bf16), pltpu.VMEM((R, 2 * SSM_N), f32),
               pltpu.VMEM((SUB, 2 * SSM_N), f32)]
    if direction == 0:
        out_spec = tm_spec
        out_sds = jax.ShapeDtypeStruct((nbg, seq * SUB, W), f32)
    else:
        in_specs += [tm_spec, pl.BlockSpec((1, W), lambda g, j: (0, 0)),
                     pl.BlockSpec((W, W), lambda g, j: (0, 0)), pl.BlockSpec((1, W), lambda g, j: (0, 0))]
        args += [yf_tm, prm["d"], prm["wglu"], prm["bglu"]]
        out_spec = pl.BlockSpec((SUB, tc, W), lambda g, j: (g, cidx(j), 0))
        out_sds = jax.ShapeDtypeStruct((nb, seq, W), bf16)
        scratch.append(pltpu.VMEM((SUB, tc, W), f32))
    return pl.pallas_call(
        functools.partial(_s5_kernel, tc=tc, nch=nch, direction=direction, has_s0=s0 is not None),
        grid=(nbg, nch),
        in_specs=in_specs,
        out_specs=[out_spec, st_spec],
        out_shape=[out_sds, jax.ShapeDtypeStruct((nb, 2 * SSM_N), f32)],
        scratch_shapes=scratch,
        compiler_params=_cp(2),
        name="s5_dir%d" % direction,
    )(*args)


def _rope(x, cos, sin_signed):
    return x * cos + pltpu.roll(x, HEAD_DIM // 2, 1) * sin_signed


def _unit_rms(x, w):
    return x * lax.rsqrt(jnp.mean(x * x, axis=-1, keepdims=True) + EPS) * w


def _attn_kernel(*refs, seq, latent):
    q_ref, k_ref, v_ref, qw_ref, kw_ref, sink_ref = refs[:6]
    pos = 6
    if latent:
        cos_ref, sin_ref, ck_ref, cv_ref = refs[pos:pos + 4]
        pos += 4
    o_ref = refs[pos]
    pos += 1
    if not latent:
        kout_ref, vout_ref = refs[pos], refs[pos + 1]
        pos += 2
    kb_scr, vb_scr = refs[pos], refs[pos + 1]
    pos += 2
    if latent:
        ckb_scr, cvb_scr = refs[pos], refs[pos + 1]

    HD = HEAD_DIM
    qb = pl.program_id(1)

    @pl.when(qb == 0)
    def _():
        for hk in range(ATT_KV_HEADS):
            sl = slice(hk * HD, (hk + 1) * HD)
            kn = _unit_rms(k_ref[:, sl], kw_ref[...])
            if latent:
                kn = _rope(kn, cos_ref[...], sin_ref[...])
            else:
                kout_ref[:, sl] = kn
            kb_scr[:, sl] = kn.astype(bf16)
        v = v_ref[...]
        vb_scr[...] = v.astype(bf16)
        if latent:
            ckb_scr[...] = ck_ref[...].astype(bf16)
            cvb_scr[...] = cv_ref[...].astype(bf16)
        else:
            vout_ref[...] = v

    q0 = pl.multiple_of(qb * CHUNK, CHUNK)
    if latent:
        win = 3 * CHUNK
        start = pl.multiple_of(jnp.clip(q0 - CHUNK, 0, seq - win), CHUNK)
        qpos = q0 + (lax.broadcasted_iota(jnp.int32, (2 * CHUNK, win), 0) & (CHUNK - 1))
        kpos = start + lax.broadcasted_iota(jnp.int32, (2 * CHUNK, win), 1)
        band = jnp.abs(qpos - kpos) <= WINDOW
        cos_q = cos_ref[pl.ds(q0, CHUNK), :]
        sin_q = sin_ref[pl.ds(q0, CHUNK), :]
    else:
        win = seq
        start = 0
    nt_dims = (((1,), (1,)), ((), ()))
    row = lax.broadcasted_iota(jnp.int32, (2 * CHUNK, 1), 0)

    def lane_tiles(x):
        return [x[:, t * HD:(t + 1) * HD] for t in range(x.shape[1] // HD)]

    def fold(op, tiles):
        acc = tiles[0]
        for t in tiles[1:]:
            acc = op(acc, t)
        return acc

    for hk in range(ATT_KV_HEADS):
        sl = slice(hk * HD, (hk + 1) * HD)
        qs = []
        for g in range(2):
            hq = hk * 2 + g
            qn = _unit_rms(q_ref[:, hq * HD:(hq + 1) * HD], qw_ref[...])
            if latent:
                qn = _rope(qn, cos_q, sin_q)
            qs.append((qn * (HD ** -0.5)).astype(bf16))
        qg = jnp.concatenate(qs, axis=0)
        snk = jnp.where(row < CHUNK, sink_ref[hk * 2], sink_ref[hk * 2 + 1])
        s1 = lax.dot_general(qg, kb_scr[pl.ds(start, win), sl], nt_dims, preferred_element_type=f32)
        if latent:
            s1 = jnp.where(band, s1, NEG_INF)
            s2 = lax.dot_general(qg, ckb_scr[:, sl], nt_dims, preferred_element_type=f32)
        s_tiles = lane_tiles(s1) + (lane_tiles(s2) if latent else [])
        m = jnp.maximum(jnp.max(fold(jnp.maximum, s_tiles), axis=-1, keepdims=True), snk)
        p1 = jnp.exp(s1 - m)
        if latent:
            p2 = jnp.exp(s2 - m)
        p_tiles = lane_tiles(p1) + (lane_tiles(p2) if latent else [])
        den = jnp.sum(fold(jnp.add, p_tiles), axis=-1, keepdims=True) + jnp.exp(snk - m)
        o = jnp.dot((p1 / den).astype(bf16), vb_scr[pl.ds(start, win), sl], preferred_element_type=f32)
        if latent:
            o += jnp.dot((p2 / den).astype(bf16), cvb_scr[:, sl], preferred_element_type=f32)
        for g in range(2):
            hq = hk * 2 + g
            o_ref[:, hq * HD:(hq + 1) * HD] = o[g * CHUNK:(g + 1) * CHUNK].astype(bf16)


def _attention(z, qw, kw, sink, rope, cache, grp):
    nb, seq = grp.nb, grp.seq
    latent = not grp.ctx
    nqb = seq // CHUNK
    kvw = ATT_KV_HEADS * HEAD_DIM
    in_specs = [
        pl.BlockSpec((CHUNK, 512), lambda b, i: (b * nqb + i, AQ // 512)),
        pl.BlockSpec((seq, kvw), lambda b, i: (b, AK // kvw)),
        pl.BlockSpec((seq, kvw), lambda b, i: (b, AV // kvw)),
        pl.BlockSpec((1, HEAD_DIM), lambda b, i: (0, 0)),
        pl.BlockSpec((1, HEAD_DIM), lambda b, i: (0, 0)),
        pl.BlockSpec(memory_space=pltpu.SMEM),
    ]
    args = [z, z, z, qw, kw, sink]
    scratch = [pltpu.VMEM((seq, kvw), bf16), pltpu.VMEM((seq, kvw), bf16)]
    out_specs = [pl.BlockSpec((CHUNK, 512), lambda b, i: (b * nqb + i, 0))]
    out_shape = [jax.ShapeDtypeStruct((nb * seq, ATT_Q_HEADS * HEAD_DIM), bf16)]
    if latent:
        in_specs += [pl.BlockSpec((seq, HEAD_DIM), lambda b, i: (0, 0)),
                     pl.BlockSpec((seq, HEAD_DIM), lambda b, i: (0, 0)),
                     pl.BlockSpec((None, PAST_LEN, kvw), lambda b, i: (b, 0, 0)),
                     pl.BlockSpec((None, PAST_LEN, kvw), lambda b, i: (b, 0, 0))]
        args += [rope[0], rope[1], cache[0], cache[1]]
        scratch += [pltpu.VMEM((PAST_LEN, kvw), bf16), pltpu.VMEM((PAST_LEN, kvw), bf16)]
    else:
        kv_out = pl.BlockSpec((seq, kvw), lambda b, i: (b, 0))
        out_specs += [kv_out, kv_out]
        out_shape += [jax.ShapeDtypeStruct((nb * seq, kvw), f32)] * 2
    return pl.pallas_call(
        functools.partial(_attn_kernel, seq=seq, latent=latent),
        grid=(nb, nqb),
        in_specs=in_specs,
        out_specs=out_specs,
        out_shape=out_shape,
        scratch_shapes=scratch,
        compiler_params=_cp(2),
        name="attention_lat" if latent else "attention_ctx",
    )(*args)


def _merge_kernel(h_ref, o0, o1, o2, o3, g0, g1, g2, g3, wbr_ref, *rest):
    out_ref = rest[len(rest) // 2]
    h = h_ref[...]
    acc = None
    for b, (o_ref, g_ref) in enumerate(((o0, g0), (o1, g1), (o2, g2), (o3, g3))):
        gate = _sigmoid(jnp.dot(h, g_ref[...], preferred_element_type=f32))
        t = gate * jnp.dot(o_ref[...], wbr_ref[b], preferred_element_type=f32)
        acc = t if acc is None else acc + t
    out_ref[...] = acc.astype(bf16)
    n_cast = len(rest) // 2
    for src, dst in zip(rest[:n_cast], rest[n_cast + 1:]):
        dst[...] = src[...].astype(bf16)


def _merge(h, branches, w_in, w_br, grp, cast_next=None):
    tn = 512
    g0 = ZW // tn
    gb = D_MODEL // tn
    n_i, n_j = grp.n // TM, D_MODEL // tn
    o_spec = pl.BlockSpec((TM, BRANCH_WIDTH), lambda i, j: (i, 0))

    def gate_spec(b):
        return pl.BlockSpec((D_MODEL, tn), lambda i, j: (0, g0 + b * gb + j))

    in_specs = [
        pl.BlockSpec((TM, D_MODEL), lambda i, j: (i, 0)),
        o_spec, o_spec, o_spec, o_spec,
        gate_spec(0), gate_spec(1), gate_spec(2), gate_spec(3),
        pl.BlockSpec((N_BRANCH, BRANCH_WIDTH, tn), lambda i, j: (0, 0, j)),
    ]
    args = [h, *branches, w_in, w_in, w_in, w_in, w_br]
    out_specs = [pl.BlockSpec((TM, tn), lambda i, j: (i, j))]
    out_shape = [jax.ShapeDtypeStruct((grp.n, D_MODEL), bf16)]
    if cast_next is not None:
        nxt, stacks = cast_next
        for w in stacks:
            rows, cols = w.shape[1:]
            slab = rows // (n_i * n_j)
            in_specs.append(pl.BlockSpec((None, slab, cols), lambda i, j: (nxt, i * n_j + j, 0)))
            out_specs.append(pl.BlockSpec((slab, cols), lambda i, j: (i * n_j + j, 0)))
            out_shape.append(jax.ShapeDtypeStruct((rows, cols), bf16))
            args.append(w)
    res = pl.pallas_call(
        _merge_kernel,
        grid=(n_i, n_j),
        in_specs=in_specs,
        out_specs=out_specs,
        out_shape=out_shape,
        compiler_params=_cp(2, vmem=BIG_VMEM_LIMIT if cast_next is not None else VMEM_LIMIT),
        name="merge",
    )(*args)
    return res[0], res[1:]


def _outproj_kernel(x_ref, mod_ref, m_ref, w_ref, o_ref):
    o_ref[...] = x_ref[...] + mod_ref[2:3, :] * jnp.dot(m_ref[...], w_ref[...], preferred_element_type=f32)


def _out_proj(x, mod, merged, w_out, layer, grp):
    tm = OUT_TM
    return pl.pallas_call(
        _outproj_kernel,
        grid=(grp.n // tm,),
        in_specs=[
            pl.BlockSpec((tm, D_MODEL), lambda i: (i, 0)),
            pl.BlockSpec((None, None, 6, D_MODEL), lambda i: (layer, _mod_row(grp, i, tm), 0, 0)),
            pl.BlockSpec((tm, D_MODEL), lambda i: (i, 0)),
            pl.BlockSpec((D_MODEL, D_MODEL), lambda i: (0, 0)),
        ],
        out_specs=pl.BlockSpec((tm, D_MODEL), lambda i: (i, 0)),
        out_shape=jax.ShapeDtypeStruct((grp.n, D_MODEL), f32),
        compiler_params=_cp(1),
        name="out_proj",
    )(x, mod, merged, w_out)


def _ffn_kernel(x_ref, mod_ref, nw_ref, w1_ref, w2_ref, o_ref, h_scr, r_scr, gs_scr):
    j = pl.program_id(1)

    @pl.when(j == 0)
    def _():
        _modnorm_rows(x_ref, nw_ref[...], mod_ref[3:4, :], mod_ref[4:5, :], h_scr, r_scr, gs_scr, copy_ref=o_ref)

    a = jnp.maximum(jnp.dot(h_scr[...], w1_ref[...], preferred_element_type=f32), 0.0)
    a2 = (a * a).astype(bf16)
    cw = 512
    for cidx in range(D_MODEL // cw):
        cs = slice(cidx * cw, (cidx + 1) * cw)
        o_ref[:, cs] += mod_ref[5:6, cs] * jnp.dot(a2, w2_ref[:, cs], preferred_element_type=f32)


def _ffn(x, mod, nw, w1, w2, layer, grp):
    tf = 1024
    nj = D_FF // tf
    return pl.pallas_call(
        _ffn_kernel,
        grid=(grp.n // TM, nj),
        in_specs=[
            pl.BlockSpec((TM, D_MODEL), lambda i, j: (i, 0)),
            pl.BlockSpec((None, None, 6, D_MODEL), lambda i, j: (layer, _mod_row(grp, i), 0, 0)),
            pl.BlockSpec((None, 1, D_MODEL), lambda i, j: (layer, 0, 0)),
            pl.BlockSpec((D_MODEL, tf), lambda i, j: (0, j)),
            pl.BlockSpec((tf, D_MODEL), lambda i, j: (j, 0)),
        ],
        out_specs=pl.BlockSpec((TM, D_MODEL), lambda i, j: (i, 0)),
        out_shape=jax.ShapeDtypeStruct((grp.n, D_MODEL), f32),
        scratch_shapes=[pltpu.VMEM((TM, D_MODEL), bf16), pltpu.VMEM((TM, 128), f32),
                        pltpu.VMEM((2, SUB, D_MODEL), f32)],
        compiler_params=_cp(2, vmem=BIG_VMEM_LIMIT),
        name="ffn",
    )(x, mod, nw, w1, w2)


def _rope_tables(seq):
    t = jnp.arange(seq)
    row = (t // GRID_W).astype(f32)
    col = (t % GRID_W).astype(f32)
    n_pairs = HEAD_DIM // 4
    freqs = ROPE_BASE ** (-jnp.arange(n_pairs, dtype=f32) / n_pairs)
    ang = jnp.concatenate([row[:, None] * freqs, col[:, None] * freqs], axis=-1)
    cos = jnp.cos(ang)
    sin = jnp.sin(ang)
    return jnp.concatenate([cos, cos], axis=-1), jnp.concatenate([-sin, sin], axis=-1)


def _ssm_params(a_re, a_im, log_dt, b_re, b_im, c_re, c_im):
    lead = a_re.shape[:-2]
    dt = jnp.exp(log_dt)[..., None]
    mag = jnp.exp(a_re * dt)
    ab_re = mag * jnp.cos(a_im * dt)
    ab_im = mag * jnp.sin(a_im * dt)
    den = a_re * a_re + a_im * a_im
    q_re = (((ab_re - 1.0) * a_re + ab_im * a_im) / den)[..., None]
    q_im = ((ab_im * a_re - (ab_re - 1.0) * a_im) / den)[..., None]
    bb_re = q_re * b_re - q_im * b_im
    bb_im = q_re * b_im + q_im * b_re
    a = jnp.stack([ab_re.reshape(lead + (SSM_N,)), ab_im.reshape(lead + (SSM_N,))], axis=-2)

    ncb = SSM_N // SSM_CB
    gpb = SSM_GROUPS // ncb
    eye = jnp.eye(gpb, dtype=f32)[:, None, :, None]

    def diag_blocks(w, rows, cols):
        w = jnp.swapaxes(w.reshape(lead + (ncb, gpb) + w.shape[-2:]), -1, -2)
        return (w[..., None, :] * eye).reshape(lead + (ncb, rows, cols))

    bd = jnp.stack([diag_blocks(w, SSM_CI, SSM_CB) for w in (bb_re, bb_im)], axis=-4).astype(bf16)
    cd = jnp.stack([diag_blocks(w, SSM_CB, SSM_CI) for w in (c_re, -c_im)], axis=-4).astype(bf16)
    return a, bd, cd


def _lru_params(conv_w, conv_b, wa, ba, wx, bx, lam):
    w = LRU_WIDTH
    return {"cw": conv_w, "cb": conv_b.reshape(1, w),
            "wa": (0.5 * wa).astype(bf16), "ba": (0.5 * ba).reshape(2, 1, w),
            "wx": (0.5 * wx).astype(bf16), "bx": (0.5 * bx).reshape(2, 1, w),
            "sp": (-0.5 * LRU_C * jax.nn.softplus(-lam)).reshape(2, 1, w)}


def _mixers(z, grp, layer, prm, states):
    nb, seq = grp.nb, grp.seq
    z3 = z.reshape(nb, seq, ZW)
    s_ret0, s_lru0, s_ssm0, cache = states

    o_ret, s_ret = _retention(z, prm["ret_lg"], prm["ret_gn"], s_ret0, grp)

    hf, lru_f = _lru_dir(z3, None, None if s_lru0 is None else s_lru0[:, 0], prm["lru"], grp, 0)
    o_lru, lru_b = _lru_dir(z3, hf, None if s_lru0 is None else s_lru0[:, 1], prm["lru"], grp, 1)

    att = _attention(z, prm["att_qw"], prm["att_kw"], prm["att_sink"], prm["rope"], cache, grp)

    yf, ssm_f = _s5_dir(z3, None, None if s_ssm0 is None else s_ssm0[:, 0], prm["ssm"], grp, 0)
    o_ssm, ssm_b = _s5_dir(z3, yf, None if s_ssm0 is None else s_ssm0[:, 1], prm["ssm"], grp, 1)

    branches = [o_ret, o_lru.reshape(grp.n, LRU_WIDTH), att[0], o_ssm.reshape(grp.n, SSM_WIDTH)]
    new_state = None
    if grp.ctx:
        ssm_fin = jnp.stack([ssm_f, ssm_b], axis=1)
        new_state = (att[1].reshape(nb, seq, ATT_KV_HEADS, HEAD_DIM), att[2].reshape(nb, seq, ATT_KV_HEADS, HEAD_DIM),
                     s_ret, jnp.stack([lru_f, lru_b], axis=1),
                     ssm_fin[:, :, :SSM_N].reshape(nb, 2, SSM_GROUPS, SSM_STATE),
                     ssm_fin[:, :, SSM_N:].reshape(nb, 2, SSM_GROUPS, SSM_STATE))
    return branches, new_state


def kernel(x_prompt, x_sample, cache_attn_k, cache_attn_v, state_ret, state_lru, state_ssm_re, state_ssm_im, c, c_ctx, w_mod, b_mod, norm1, w_in, ret_decay_logit, ret_gn, lru_conv_w, lru_conv_b, lru_wa, lru_ba, lru_wx, lru_bx, lru_lambda, att_q_norm, att_k_norm, att_sink, ssm_a_re, ssm_a_im, ssm_log_dt, ssm_b_re, ssm_b_im, ssm_c_re, ssm_c_im, ssm_d, ssm_w_glu, ssm_b_glu, w_br, w_out, norm2, w_ff1, w_ff2):
    xs = {PROMPT: x_prompt.reshape(PROMPT.n, D_MODEL), SAMPLE: x_sample.reshape(SAMPLE.n, D_MODEL)}
    cond = jnp.concatenate([c, c_ctx[None, :], jnp.zeros((COND_ROWS - SAMPLE.nb - 1, D_MODEL), f32)], axis=0)
    mod = _modulation(cond, w_mod, b_mod).reshape(DEPTH, COND_ROWS, 6, D_MODEL)

    w_stacks = (w_in, w_br.reshape(DEPTH, N_BRANCH * BRANCH_WIDTH, D_MODEL), w_out, w_ff1, w_ff2)
    w_layer = [w[0].astype(bf16) for w in w_stacks]
    norm1_r = norm1.reshape(DEPTH, 1, D_MODEL)
    norm2_r = norm2.reshape(DEPTH, 1, D_MODEL)
    rope = _rope_tables(SAMPLE.seq)
    log_gamma = -jax.nn.softplus(-ret_decay_logit)
    kvw = ATT_KV_HEADS * HEAD_DIM

    ssm_a_all, ssm_bd_all, ssm_cd_all = _ssm_params(ssm_a_re, ssm_a_im, ssm_log_dt, ssm_b_re, ssm_b_im,
                                                    ssm_c_re, ssm_c_im)
    new_states = []
    for l in range(DEPTH):
        ssm_a, ssm_bd, ssm_cd = ssm_a_all[l], ssm_bd_all[l], ssm_cd_all[l]
        prm = {
            "ret_lg": jnp.broadcast_to(log_gamma[l].T[:, :, None], (RET_HEADS, 2, 128)),
            "ret_gn": ret_gn[l].reshape(1, RET_HEADS * 128),
            "lru": _lru_params(lru_conv_w[l], lru_conv_b[l], lru_wa[l], lru_ba[l], lru_wx[l], lru_bx[l],
                               lru_lambda[l]),
            "att_qw": att_q_norm[l].reshape(1, HEAD_DIM), "att_kw": att_k_norm[l].reshape(1, HEAD_DIM),
            "att_sink": att_sink[l], "rope": rope,
            "ssm": {"a": ssm_a, "bd": ssm_bd, "cd": ssm_cd, "d": ssm_d[l].reshape(1, SSM_WIDTH),
                    "wglu": ssm_w_glu[l].astype(bf16), "bglu": ssm_b_glu[l].reshape(1, SSM_WIDTH)},
        }
        states = {
            PROMPT: (None, None, None, None),
            SAMPLE: (state_ret[:, l], state_lru[:, l],
                     jnp.concatenate([state_ssm_re[:, l].reshape(SAMPLE.nb, 2, SSM_N),
                                      state_ssm_im[:, l].reshape(SAMPLE.nb, 2, SSM_N)], axis=-1),
                     (cache_attn_k[:, l].reshape(SAMPLE.nb, PAST_LEN, kvw),
                      cache_attn_v[:, l].reshape(SAMPLE.nb, PAST_LEN, kvw))),
        }
        w_in_l, w_br_l, w_out_l, w_ff1_l, w_ff2_l = w_layer
        w_br_l = w_br_l.reshape(N_BRANCH, BRANCH_WIDTH, D_MODEL)
        for grp in (PROMPT, SAMPLE):
            x = xs[grp]
            z, h = _proj_in(x, mod, norm1_r, w_in_l, l, grp)
            branches, st = _mixers(z, grp, l, prm, states[grp])
            if grp.ctx:
                new_states.append(st)
            cast_next = (l + 1, w_stacks) if (grp is SAMPLE and l + 1 < DEPTH) else None
            merged, w_next = _merge(h, branches, w_in_l, w_br_l, grp, cast_next)
            if cast_next is not None:
                w_layer = list(w_next)
            x = _out_proj(x, mod, merged, w_out_l, l, grp)
            xs[grp] = _ffn(x, mod, norm2_r, w_ff1_l, w_ff2_l, l, grp)

    y_prompt = xs[PROMPT].reshape(PROMPT.nb, PROMPT.seq, D_MODEL)
    y_sample = xs[SAMPLE].reshape(SAMPLE.nb, SAMPLE.seq, D_MODEL)
    return (y_prompt, y_sample) + tuple(jnp.stack([st[i] for st in new_states], axis=1) for i in range(6))
```

```python
import functools
from typing import NamedTuple

import jax
import jax.numpy as jnp
from jax import lax
from jax.experimental import pallas as pl
from jax.experimental.pallas import tpu as pltpu

f32 = jnp.float32
bf16 = jnp.bfloat16

D_MODEL = 2048
DEPTH = 4
PAST_LEN = 256
GRID_W = 64
CHUNK = 128
ATT_QB_LAT = 128
ATT_QB_CTX = 256
EPS = 1e-6
NEG_INF = -1e30
RET_HEADS = 4
RET_DK = 128
LRU_WIDTH = 512
LRU_BLOCKS = 4
LRU_C = 8.0
CONV_W = 4
ATT_Q_HEADS = 4
ATT_KV_HEADS = 2
HEAD_DIM = 128
WINDOW = 128
ROPE_BASE = 10000.0
SSM_WIDTH = 512
SSM_GROUP = 16
SSM_GROUPS = 32
SSM_STATE = 64
SSM_N = SSM_GROUPS * SSM_STATE
SSM_CB = 512
SSM_CI = SSM_CB // SSM_STATE * SSM_GROUP
N_BRANCH = 4
BRANCH_WIDTH = 512
D_FF = 4 * D_MODEL
RQ, RK, RV, RG, LX, LG, AQ, AK, AV, SU, ZW = 0, 512, 1024, 1536, 2048, 2560, 3072, 3584, 3840, 4096, 4608
SUB = 8
COND_ROWS = 16
VMEM_LIMIT = 56 * 1024 * 1024
BIG_VMEM_LIMIT = 62 * 1024 * 1024

TM = 1024
OUT_TM = 512
LRU_TC = 128
SSM_TC = 128
ROW_BLK = 16


class Group(NamedTuple):
    nb: int
    seq: int
    ctx: bool

    @property
    def n(self):
        return self.nb * self.seq


PROMPT = Group(32, 256, True)
SAMPLE = Group(8, 2048, False)
CTX_ROW = SAMPLE.nb


def _cp(n_axes, vmem=VMEM_LIMIT):
    return pltpu.CompilerParams(dimension_semantics=("arbitrary",) * n_axes, vmem_limit_bytes=vmem)


def _sigmoid(x):
    return 0.5 * jnp.tanh(0.5 * x) + 0.5


def _mod_row(grp, i, tm=None):
    return CTX_ROW if grp.ctx else i // (grp.seq // (tm or TM))


def _modnorm_rows(x_ref, nw, shift, scale, h_ref, r_scr, gs_scr, copy_ref=None):
    tm, d = x_ref.shape
    lane_tiles = d // 128

    def ssq_body(r, carry):
        rows = pl.ds(pl.multiple_of(r * SUB, SUB), SUB)
        acc = None
        for t in range(lane_tiles):
            v = x_ref[rows, t * 128:(t + 1) * 128]
            acc = v * v if acc is None else acc + v * v
        r_scr[rows, :] = acc
        return carry

    lax.fori_loop(0, tm // SUB, ssq_body, 0, unroll=4)
    ssq = jnp.sum(r_scr[...], axis=-1, keepdims=True)
    r_scr[...] = jnp.broadcast_to(lax.rsqrt(ssq * (1.0 / d) + EPS), (tm, 128))
    gs_scr[0] = jnp.broadcast_to(nw * (1.0 + scale), (SUB, d))
    gs_scr[1] = jnp.broadcast_to(shift, (SUB, d))

    def out_body(r, carry):
        r0 = pl.multiple_of(r * ROW_BLK, ROW_BLK)
        halves = [pl.ds(pl.multiple_of(r0 + k * SUB, SUB), SUB) for k in range(ROW_BLK // SUB)]
        invs = [r_scr[rows, :] for rows in halves]
        for t in range(lane_tiles):
            cols = slice(t * 128, (t + 1) * 128)
            gain = gs_scr[0, :, cols]
            shf = gs_scr[1, :, cols]
            xs = [x_ref[rows, cols] for rows in halves]
            if copy_ref is not None:
                for rows, xv in zip(halves, xs):
                    copy_ref[rows, cols] = xv
            parts = [xv * inv * gain + shf for xv, inv in zip(xs, invs)]
            h_ref[pl.ds(r0, ROW_BLK), cols] = jnp.concatenate(parts, axis=0).astype(bf16)
        return carry

    lax.fori_loop(0, tm // ROW_BLK, out_body, 0, unroll=2)


def _mod_kernel(c_ref, w_ref, b_ref, o_ref):
    c = c_ref[...]
    s = (c * _sigmoid(c)).astype(bf16)
    o_ref[...] = jnp.dot(s, w_ref[...].astype(bf16), preferred_element_type=f32) + b_ref[...]


def _modulation(cond, w_mod, b_mod):
    tn = 1024
    n6 = 6 * D_MODEL
    return pl.pallas_call(
        _mod_kernel,
        grid=(DEPTH, n6 // tn),
        in_specs=[
            pl.BlockSpec((COND_ROWS, D_MODEL), lambda l, j: (0, 0)),
            pl.BlockSpec((None, D_MODEL, tn), lambda l, j: (l, 0, j)),
            pl.BlockSpec((None, 1, tn), lambda l, j: (l, 0, j)),
        ],
        out_specs=pl.BlockSpec((None, COND_ROWS, tn), lambda l, j: (l, 0, j)),
        out_shape=jax.ShapeDtypeStruct((DEPTH, COND_ROWS, n6), f32),
        compiler_params=_cp(2),
        name="modulation",
    )(cond, w_mod, b_mod.reshape(DEPTH, 1, n6))


def _proj_kernel(x_ref, mod_ref, nw_ref, w_ref, o_ref, h_ref, r_scr, gs_scr):
    @pl.when(pl.program_id(1) == 0)
    def _():
        _modnorm_rows(x_ref, nw_ref[...], mod_ref[0:1, :], mod_ref[1:2, :], h_ref, r_scr, gs_scr)

    o_ref[...] = jnp.dot(h_ref[...], w_ref[...], preferred_element_type=f32)


def _proj_in(x, mod, nw, w_in, layer, grp):
    tn = 1536
    return pl.pallas_call(
        _proj_kernel,
        grid=(grp.n // TM, ZW // tn),
        in_specs=[
            pl.BlockSpec((TM, D_MODEL), lambda i, j: (i, 0)),
            pl.BlockSpec((None, None, 6, D_MODEL), lambda i, j: (layer, _mod_row(grp, i), 0, 0)),
            pl.BlockSpec((None, 1, D_MODEL), lambda i, j: (layer, 0, 0)),
            pl.BlockSpec((D_MODEL, tn), lambda i, j: (0, j)),
        ],
        out_specs=[pl.BlockSpec((TM, tn), lambda i, j: (i, j)),
                   pl.BlockSpec((TM, D_MODEL), lambda i, j: (i, 0))],
        out_shape=[jax.ShapeDtypeStruct((grp.n, ZW), f32), jax.ShapeDtypeStruct((grp.n, D_MODEL), bf16)],
        scratch_shapes=[pltpu.VMEM((TM, 128), f32), pltpu.VMEM((2, SUB, D_MODEL), f32)],
        compiler_params=_cp(2),
        name="proj_in",
    )(x, mod, nw, w_in)


def _ret_kernel(*refs, nc, has_s0, emit_state):
    q_ref, k_ref, v_ref, g_ref, lg_ref, gn_ref = refs[:6]
    pos = 6
    s0_ref = None
    if has_s0:
        s0_ref = refs[pos]
        pos += 1
    o_ref = refs[pos]
    pos += 1
    sout_ref = None
    if emit_state:
        sout_ref = refs[pos]
        pos += 1
    w_scr, kv_scr = refs[pos], refs[pos + 1]

    C = CHUNK
    H = RET_HEADS
    DEC, W_IN_F, W_IN_B, W_END_F, W_END_B = range(5)
    tn_dims = (((0,), (0,)), ((), ()))
    nt_dims = (((1,), (1,)), ((), ()))

    @pl.when(pl.program_id(0) == 0)
    def _():
        ii = lax.broadcasted_iota(jnp.int32, (C, C), 0).astype(f32)
        jj = lax.broadcasted_iota(jnp.int32, (C, C), 1).astype(f32)
        rel = ii - jj
        for h in range(H):
            lgf = lg_ref[h, 0:1, :]
            lgb = lg_ref[h, 1:2, :]
            w_scr[h, DEC] = (jnp.where(rel >= 0, jnp.exp(jnp.maximum(rel, 0.0) * lgf), 0.0)
                             + jnp.where(rel <= 0, jnp.exp(jnp.maximum(-rel, 0.0) * lgb), 0.0))
            w_scr[h, W_IN_F] = jnp.exp((ii + 1.0) * lgf)
            w_scr[h, W_IN_B] = jnp.exp((C - ii) * lgb)
            w_scr[h, W_END_F] = jnp.exp((C - 1.0 - ii) * lgf)
            w_scr[h, W_END_B] = jnp.exp(ii * lgb)

    def kv_body(n, carry):
        r = pl.multiple_of(n * C, C)
        for h in range(H):
            cols = slice(h * C, (h + 1) * C)
            k = k_ref[pl.ds(r, C), cols]
            vb = v_ref[pl.ds(r, C), cols].astype(bf16)
            kw = jnp.concatenate([k * w_scr[h, W_END_F], k * w_scr[h, W_END_B]], axis=1).astype(bf16)
            kv = lax.dot_general(kw, vb, tn_dims, preferred_element_type=f32)
            kv_scr[0, n * H + h] = kv[:C]
            kv_scr[1, n * H + h] = kv[C:]
        return carry

    lax.fori_loop(0, nc, kv_body, 0, unroll=2)

    for h in range(H):
        cd_f = jnp.exp(C * lg_ref[h, 0:1, :])
        cd_b = jnp.exp(C * lg_ref[h, 1:2, :])

        def fwd_body(n, s, h=h, cd_f=cd_f):
            kv = kv_scr[0, n * H + h]
            kv_scr[0, n * H + h] = s
            return cd_f * s + kv

        def bwd_body(i, s, h=h, cd_b=cd_b):
            n = nc - 1 - i
            kv = kv_scr[1, n * H + h]
            kv_scr[1, n * H + h] = s
            return cd_b * s + kv

        s_f = lax.fori_loop(0, nc, fwd_body, s0_ref[0, h] if has_s0 else jnp.zeros((C, C), f32))
        s_b = lax.fori_loop(0, nc, bwd_body, s0_ref[1, h] if has_s0 else jnp.zeros((C, C), f32))
        if emit_state:
            sout_ref[0, h] = s_f
            sout_ref[1, h] = s_b

    def out_body(n, carry):
        r = pl.multiple_of(n * C, C)
        for h in range(H):
            cols = slice(h * C, (h + 1) * C)
            q = q_ref[pl.ds(r, C), cols] * (RET_DK ** -0.5)
            kb = k_ref[pl.ds(r, C), cols].astype(bf16)
            vb = v_ref[pl.ds(r, C), cols].astype(bf16)
            sc = lax.dot_general(q.astype(bf16), kb, nt_dims, preferred_element_type=f32) * w_scr[h, DEC]
            lhs = jnp.concatenate([sc, q * w_scr[h, W_IN_F], q * w_scr[h, W_IN_B]], axis=1).astype(bf16)
            rhs = jnp.concatenate([vb, kv_scr[0, n * H + h].astype(bf16), kv_scr[1, n * H + h].astype(bf16)], axis=0)
            o = jnp.dot(lhs, rhs, preferred_element_type=f32)
            mu = jnp.mean(o, axis=-1, keepdims=True)
            d = o - mu
            var = jnp.mean(d * d, axis=-1, keepdims=True)
            on = d * lax.rsqrt(var + EPS) * gn_ref[:, cols]
            g = g_ref[pl.ds(r, C), cols]
            o_ref[pl.ds(r, C), cols] = (g * _sigmoid(g) * on).astype(bf16)
        return carry

    lax.fori_loop(0, nc, out_body, 0, unroll=min(nc, 4))


def _retention(z, lg, gn, s0, grp):
    seq = grp.seq
    W = RET_HEADS * 128
    in_specs = [
        pl.BlockSpec((seq, W), lambda b: (b, RQ // W)),
        pl.BlockSpec((seq, W), lambda b: (b, RK // W)),
        pl.BlockSpec((seq, W), lambda b: (b, RV // W)),
        pl.BlockSpec((seq, W), lambda b: (b, RG // W)),
        pl.BlockSpec((RET_HEADS, 2, 128), lambda b: (0, 0, 0)),
        pl.BlockSpec((1, W), lambda b: (0, 0)),
    ]
    args = [z, z, z, z, lg, gn]
    st_spec = pl.BlockSpec((None, 2, RET_HEADS, 128, 128), lambda b: (b, 0, 0, 0, 0))
    if s0 is not None:
        in_specs.append(st_spec)
        args.append(s0)
    out_specs = [pl.BlockSpec((seq, W), lambda b: (b, 0))]
    out_shape = [jax.ShapeDtypeStruct((grp.n, W), bf16)]
    if grp.ctx:
        out_specs.append(st_spec)
        out_shape.append(jax.ShapeDtypeStruct((grp.nb, 2, RET_HEADS, 128, 128), f32))
    nc = seq // CHUNK
    res = pl.pallas_call(
        functools.partial(_ret_kernel, nc=nc, has_s0=s0 is not None, emit_state=grp.ctx),
        grid=(grp.nb,),
        in_specs=in_specs,
        out_specs=out_specs,
        out_shape=out_shape,
        scratch_shapes=[pltpu.VMEM((RET_HEADS, 5, 128, 128), f32),
                        pltpu.VMEM((2, nc * RET_HEADS, 128, 128), f32)],
        compiler_params=_cp(1),
        name="retention",
    )(*args)
    return res if grp.ctx else (res[0], None)


def _gather_tm(src_ref, nt, dst_ref, row0):
    w = src_ref.shape[-1]

    def body(i, carry):
        v = src_ref[:, pl.ds(pl.multiple_of(i * SUB, SUB), SUB), :]
        rows = pl.ds(pl.multiple_of(row0 + i * SUB * SUB, SUB), SUB * SUB)
        dst_ref[rows, :] = jnp.swapaxes(v, 0, 1).reshape(SUB * SUB, w)
        return carry

    lax.fori_loop(0, nt // SUB, body, 0, unroll=2)


def _scatter_bm(src_ref, nt, dst_ref):
    w = src_ref.shape[-1]

    def body(i, carry):
        v = src_ref[pl.ds(pl.multiple_of(i * SUB * SUB, SUB * SUB), SUB * SUB), :].reshape(SUB, SUB, w)
        dst_ref[:, pl.ds(pl.multiple_of(i * SUB, SUB), SUB), :] = jnp.swapaxes(v, 0, 1)
        return carry

    lax.fori_loop(0, nt // SUB, body, 0, unroll=2)


def _lru_kernel(*refs, tc, nch, direction, has_s0):
    xprev_ref, xcur_ref, xnext_ref, cw_ref, cb_ref, wa_ref, ba_ref, wx_ref, bx_ref, sp_ref = refs[:10]
    pos = 10
    s0_ref = None
    if has_s0:
        s0_ref = refs[pos]
        pos += 1
    if direction == 1:
        gate_ref, hf_ref = refs[pos], refs[pos + 1]
        pos += 2
    out_ref, fin_ref = refs[pos], refs[pos + 1]
    xe_scr, a_scr, b_scr, h_scr = refs[pos + 2: pos + 6]
    if direction == 1:
        g_scr, y_scr = refs[pos + 6], refs[pos + 7]

    R = tc * SUB
    W = LRU_WIDTH
    j = pl.program_id(1)
    c = j if direction == 0 else nch - 1 - j

    @pl.when(j == 0)
    def _():
        h_scr[...] = s0_ref[...] if has_s0 else jnp.zeros((SUB, W), f32)

    for t in range(2):
        v = xprev_ref[:, SUB - 2 + t:SUB - 1 + t, :].reshape(SUB, W)
        xe_scr[t * SUB:(t + 1) * SUB, :] = jnp.where(c > 0, v, 0.0)
    _gather_tm(xcur_ref, tc, xe_scr, 2 * SUB)
    xe_scr[2 * SUB + R:3 * SUB + R, :] = jnp.where(c < nch - 1, xnext_ref[:, 0:1, :].reshape(SUB, W), 0.0)
    if direction == 1:
        _gather_tm(gate_ref, tc, g_scr, 0)

    xc = cb_ref[...] + xe_scr[0:R, :] * cw_ref[0:1, :]
    for t in range(1, CONV_W):
        xc = xc + xe_scr[t * SUB:t * SUB + R, :] * cw_ref[t:t + 1, :]

    bd = W // LRU_BLOCKS
    for n in range(LRU_BLOCKS):
        sl = slice(n * bd, (n + 1) * bd)
        xs = xc[:, sl]
        xb = xs.astype(bf16)
        tr = jnp.tanh(jnp.dot(xb, wa_ref[n], preferred_element_type=f32) + ba_ref[:, sl])
        ti = jnp.tanh(jnp.dot(xb, wx_ref[n], preferred_element_type=f32) + bx_ref[:, sl])
        log_a = sp_ref[:, sl] * (tr + 1.0)
        a = jnp.exp(log_a)
        a_scr[:, sl] = a
        t = -jnp.tanh(log_a) * (a * a + 1.0)
        root = jnp.where(t > 0.0, t * lax.rsqrt(t), 0.0)
        b_scr[:, sl] = (0.5 * root) * ((ti + 1.0) * xs)

    def step(s, h):
        t = s if direction == 0 else tc - 1 - s
        r0 = pl.multiple_of(t * SUB, SUB)
        h = a_scr[pl.ds(r0, SUB), :] * h + b_scr[pl.ds(r0, SUB), :]
        b_scr[pl.ds(r0, SUB), :] = h
        return h

    h = lax.fori_loop(0, tc, step, h_scr[...], unroll=8)
    h_scr[...] = h
    fin_ref[...] = h
    if direction == 0:
        out_ref[...] = b_scr[...]
    else:
        b_scr[...] = jax.nn.gelu(g_scr[...]) * (hf_ref[...] + b_scr[...])
        _scatter_bm(b_scr, tc, y_scr)
        out_ref[...] = y_scr[...].astype(bf16)


def _lru_dir(z3, hf_tm, s0, prm, grp, direction):
    nb, seq = grp.nb, grp.seq
    nbg = nb // SUB
    tc = min(LRU_TC, seq)
    nch = seq // tc
    R = tc * SUB
    W = LRU_WIDTH

    def cidx(j):
        return j if direction == 0 else nch - 1 - j

    in_specs = [
        pl.BlockSpec((SUB, SUB, W), lambda g, j: (g, jnp.maximum(cidx(j) * (tc // SUB) - 1, 0), LX // W)),
        pl.BlockSpec((SUB, tc, W), lambda g, j: (g, cidx(j), LX // W)),
        pl.BlockSpec((SUB, SUB, W), lambda g, j: (g, jnp.minimum((cidx(j) + 1) * (tc // SUB), seq // SUB - 1), LX // W)),
        pl.BlockSpec((CONV_W, W), lambda g, j: (0, 0)),
        pl.BlockSpec((1, W), lambda g, j: (0, 0)),
        pl.BlockSpec((LRU_BLOCKS, 128, 128), lambda g, j: (0, 0, 0)),
        pl.BlockSpec((1, W), lambda g, j: (0, 0)),
        pl.BlockSpec((LRU_BLOCKS, 128, 128), lambda g, j: (0, 0, 0)),
        pl.BlockSpec((1, W), lambda g, j: (0, 0)),
        pl.BlockSpec((1, W), lambda g, j: (0, 0)),
    ]
    args = [z3, z3, z3, prm["cw"], prm["cb"], prm["wa"][direction], prm["ba"][direction],
            prm["wx"][direction], prm["bx"][direction], prm["sp"][direction]]
    st_spec = pl.BlockSpec((SUB, W), lambda g, j: (g, 0))
    if s0 is not None:
        in_specs.append(st_spec)
        args.append(s0)
    tm_spec = pl.BlockSpec((None, R, W), lambda g, j: (g, cidx(j), 0))
    scratch = [pltpu.VMEM((R + 3 * SUB, W), f32), pltpu.VMEM((R, W), f32),
               pltpu.VMEM((R, W), f32), pltpu.VMEM((SUB, W), f32)]
    if direction == 0:
        out_spec = tm_spec
        out_sds = jax.ShapeDtypeStruct((nbg, seq * SUB, W), f32)
    else:
        in_specs += [pl.BlockSpec((SUB, tc, W), lambda g, j: (g, cidx(j), LG // W)), tm_spec]
        args += [z3, hf_tm]
        out_spec = pl.BlockSpec((SUB, tc, W), lambda g, j: (g, cidx(j), 0))
        out_sds = jax.ShapeDtypeStruct((nb, seq, W), bf16)
        scratch += [pltpu.VMEM((R, W), f32), pltpu.VMEM((SUB, tc, W), f32)]
    return pl.pallas_call(
        functools.partial(_lru_kernel, tc=tc, nch=nch, direction=direction, has_s0=s0 is not None),
        grid=(nbg, nch),
        in_specs=in_specs,
        out_specs=[out_spec, st_spec],
        out_shape=[out_sds, jax.ShapeDtypeStruct((nb, W), f32)],
        scratch_shapes=scratch,
        compiler_params=_cp(2),
        name="rglru_dir%d" % direction,
    )(*args)


def _s5_kernel(*refs, tc, nch, direction, has_s0):
    u_ref, bd_ref, a_ref, cd_ref = refs[:4]
    pos = 4
    s0_ref = None
    if has_s0:
        s0_ref = refs[pos]
        pos += 1
    if direction == 1:
        yf_ref, dv_ref, wg_ref, bg_ref = refs[pos:pos + 4]
        pos += 4
    out_ref, fin_ref = refs[pos], refs[pos + 1]
    u_scr, ub_scr, hs_scr, h_scr = refs[pos + 2: pos + 6]
    if direction == 1:
        y_scr = refs[pos + 6]

    j = pl.program_id(1)

    @pl.when(j == 0)
    def _():
        h_scr[...] = s0_ref[...] if has_s0 else jnp.zeros((SUB, 2 * SSM_N), f32)

    _gather_tm(u_ref, tc, u_scr, 0)
    ub_scr[...] = u_scr[...].astype(bf16)

    cbw = SSM_CB
    for cb in range(SSM_N // cbw):
        cre = slice(cb * cbw, (cb + 1) * cbw)
        cim = slice(SSM_N + cb * cbw, SSM_N + (cb + 1) * cbw)
        cin = slice(cb * SSM_CI, (cb + 1) * SSM_CI)
        hs_scr[:, cre] = jnp.dot(ub_scr[:, cin], bd_ref[0, cb], preferred_element_type=f32)
        hs_scr[:, cim] = jnp.dot(ub_scr[:, cin], bd_ref[1, cb], preferred_element_type=f32)
        a_re = jnp.broadcast_to(a_ref[0:1, cre], (SUB, cbw))
        a_im = jnp.broadcast_to(a_ref[1:2, cre], (SUB, cbw))

        def step(s, carry):
            hr, hi = carry
            t = s if direction == 0 else tc - 1 - s
            r0 = pl.multiple_of(t * SUB, SUB)
            nr = a_re * hr - a_im * hi + hs_scr[pl.ds(r0, SUB), cre]
            ni = a_re * hi + a_im * hr + hs_scr[pl.ds(r0, SUB), cim]
            hs_scr[pl.ds(r0, SUB), cre] = nr
            hs_scr[pl.ds(r0, SUB), cim] = ni
            return nr, ni

        hr, hi = lax.fori_loop(0, tc, step, (h_scr[:, cre], h_scr[:, cim]), unroll=True)
        h_scr[:, cre] = hr
        h_scr[:, cim] = hi

    fin_ref[...] = h_scr[...]
    y_parts = []
    for cb in range(SSM_N // cbw):
        cre = slice(cb * cbw, (cb + 1) * cbw)
        cim = slice(SSM_N + cb * cbw, SSM_N + (cb + 1) * cbw)
        y_parts.append(jnp.dot(hs_scr[:, cre].astype(bf16), cd_ref[0, cb], preferred_element_type=f32)
                       + jnp.dot(hs_scr[:, cim].astype(bf16), cd_ref[1, cb], preferred_element_type=f32))
    y = jnp.concatenate(y_parts, axis=1)
    if direction == 0:
        out_ref[...] = y
    else:
        yy = jax.nn.gelu(yf_ref[...] + y + dv_ref[...] * u_scr[...])
        gl = jnp.dot(yy.astype(bf16), wg_ref[...], preferred_element_type=f32) + bg_ref[...]
        u_scr[...] = yy * _sigmoid(gl)
        _scatter_bm(u_scr, tc, y_scr)
        out_ref[...] = y_scr[...].astype(bf16)


def _s5_dir(z3, yf_tm, s0, prm, grp, direction):
    nb, seq = grp.nb, grp.seq
    nbg = nb // SUB
    tc = min(SSM_TC, seq)
    nch = seq // tc
    R = tc * SUB
    W = SSM_WIDTH
    ncb = SSM_N // SSM_CB

    def cidx(j):
        return j if direction == 0 else nch - 1 - j

    tm_spec = pl.BlockSpec((None, R, W), lambda g, j: (g, cidx(j), 0))
    st_spec = pl.BlockSpec((SUB, 2 * SSM_N), lambda g, j: (g, 0))
    in_specs = [
        pl.BlockSpec((SUB, tc, W), lambda g, j: (g, cidx(j), SU // W)),
        pl.BlockSpec((2, ncb, SSM_CI, SSM_CB), lambda g, j: (0, 0, 0, 0)),
        pl.BlockSpec((2, SSM_N), lambda g, j: (0, 0)),
        pl.BlockSpec((2, ncb, SSM_CB, SSM_CI), lambda g, j: (0, 0, 0, 0)),
    ]
    args = [z3, prm["bd"][direction], prm["a"][direction], prm["cd"][direction]]
    if s0 is not None:
        in_specs.append(st_spec)
        args.append(s0)
    scratch = [pltpu.VMEM((R, W), f32), pltpu.VMEM((R, W), bf16), pltpu.VMEM((R, 2 * SSM_N), f32),
               pltpu.VMEM((SUB, 2 * SSM_N), f32)]
    if direction == 0:
        out_spec = tm_spec
        out_sds = jax.ShapeDtypeStruct((nbg, seq * SUB, W), f32)
    else:
        in_specs += [tm_spec, pl.BlockSpec((1, W), lambda g, j: (0, 0)),
                     pl.BlockSpec((W, W), lambda g, j: (0, 0)), pl.BlockSpec((1, W), lambda g, j: (0, 0))]
        args += [yf_tm, prm["d"], prm["wglu"], prm["bglu"]]
        out_spec = pl.BlockSpec((SUB, tc, W), lambda g, j: (g, cidx(j), 0))
        out_sds = jax.ShapeDtypeStruct((nb, seq, W), bf16)
        scratch.append(pltpu.VMEM((SUB, tc, W), f32))
    return pl.pallas_call(
        functools.partial(_s5_kernel, tc=tc, nch=nch, direction=direction, has_s0=s0 is not None),
        grid=(nbg, nch),
        in_specs=in_specs,
        out_specs=[out_spec, st_spec],
        out_shape=[out_sds, jax.ShapeDtypeStruct((nb, 2 * SSM_N), f32)],
        scratch_shapes=scratch,
        compiler_params=_cp(2),
        name="s5_dir%d" % direction,
    )(*args)


def _rope(x, cos, sin_signed):
    return x * cos + pltpu.roll(x, HEAD_DIM // 2, 1) * sin_signed


def _unit_rms(x, w):
    return x * lax.rsqrt(jnp.mean(x * x, axis=-1, keepdims=True) + EPS) * w


def _attn_kernel(*refs, seq, latent):
    q_ref, k_ref, v_ref, qw_ref, kw_ref, sink_ref = refs[:6]
    pos = 6
    if latent:
        cos_ref, sin_ref, ck_ref, cv_ref = refs[pos:pos + 4]
        pos += 4
    o_ref = refs[pos]
    pos += 1
    if not latent:
        kout_ref, vout_ref = refs[pos], refs[pos + 1]
        pos += 2
    kb_scr, vb_scr = refs[pos], refs[pos + 1]
    pos += 2
    if latent:
        ckb_scr, cvb_scr = refs[pos], refs[pos + 1]

    HD = HEAD_DIM
    qb = pl.program_id(1)

    @pl.when(qb == 0)
    def _():
        for hk in range(ATT_KV_HEADS):
            sl = slice(hk * HD, (hk + 1) * HD)
            kn = _unit_rms(k_ref[:, sl], kw_ref[...])
            if latent:
                kn = _rope(kn, cos_ref[...], sin_ref[...])
            else:
                kout_ref[:, sl] = kn
            kb_scr[:, sl] = kn.astype(bf16)
        v = v_ref[...]
        vb_scr[...] = v.astype(bf16)
        if latent:
            ckb_scr[...] = ck_ref[...].astype(bf16)
            cvb_scr[...] = cv_ref[...].astype(bf16)
        else:
            vout_ref[...] = v

    QB = q_ref.shape[0]
    q0 = pl.multiple_of(qb * QB, QB)
    if latent:
        win = QB + 2 * WINDOW
        start = pl.multiple_of(jnp.clip(q0 - WINDOW, 0, seq - win), WINDOW)
        qpos = q0 + (lax.broadcasted_iota(jnp.int32, (2 * QB, win), 0) & (QB - 1))
        kpos = start + lax.broadcasted_iota(jnp.int32, (2 * QB, win), 1)
        band = jnp.abs(qpos - kpos) <= WINDOW
        cos_q = cos_ref[pl.ds(q0, QB), :]
        sin_q = sin_ref[pl.ds(q0, QB), :]
    else:
        win = seq
        start = 0
    nt_dims = (((1,), (1,)), ((), ()))

    def lane_tiles(x):
        return [x[:, t * HD:(t + 1) * HD] for t in range(x.shape[1] // HD)]

    def fold(op, tiles):
        acc = tiles[0]
        for t in tiles[1:]:
            acc = op(acc, t)
        return acc

    row = lax.broadcasted_iota(jnp.int32, (2 * QB, 1), 0)
    for hk in range(ATT_KV_HEADS):
        sl = slice(hk * HD, (hk + 1) * HD)
        qs = []
        for g in range(2):
            hq = hk * 2 + g
            qn = _unit_rms(q_ref[:, hq * HD:(hq + 1) * HD], qw_ref[...])
            if latent:
                qn = _rope(qn, cos_q, sin_q)
            qs.append((qn * (HD ** -0.5)).astype(bf16))
        qg = jnp.concatenate(qs, axis=0)
        snk = jnp.where(row < QB, sink_ref[hk * 2], sink_ref[hk * 2 + 1])
        s1 = lax.dot_general(qg, kb_scr[pl.ds(start, win), sl], nt_dims, preferred_element_type=f32)
        if latent:
            s1 = jnp.where(band, s1, NEG_INF)
            s2 = lax.dot_general(qg, ckb_scr[:, sl], nt_dims, preferred_element_type=f32)
        s_tiles = lane_tiles(s1) + (lane_tiles(s2) if latent else [])
        m = jnp.maximum(jnp.max(fold(jnp.maximum, s_tiles), axis=-1, keepdims=True), snk)
        p1 = jnp.exp(s1 - m)
        if latent:
            p2 = jnp.exp(s2 - m)
        p_tiles = lane_tiles(p1) + (lane_tiles(p2) if latent else [])
        den = jnp.sum(fold(jnp.add, p_tiles), axis=-1, keepdims=True) + jnp.exp(snk - m)
        o = jnp.dot((p1 / den).astype(bf16), vb_scr[pl.ds(start, win), sl], preferred_element_type=f32)
        if latent:
            o += jnp.dot((p2 / den).astype(bf16), cvb_scr[:, sl], preferred_element_type=f32)
        for g in range(2):
            hq = hk * 2 + g
            o_ref[:, hq * HD:(hq + 1) * HD] = o[g * QB:(g + 1) * QB].astype(bf16)


def _attention(z, qw, kw, sink, rope, cache, grp):
    nb, seq = grp.nb, grp.seq
    latent = not grp.ctx
    QB = min(seq, ATT_QB_LAT if latent else ATT_QB_CTX)
    nqb = seq // QB
    kvw = ATT_KV_HEADS * HEAD_DIM
    in_specs = [
        pl.BlockSpec((QB, 512), lambda b, i: (b * nqb + i, AQ // 512)),
        pl.BlockSpec((seq, kvw), lambda b, i: (b, AK // kvw)),
        pl.BlockSpec((seq, kvw), lambda b, i: (b, AV // kvw)),
        pl.BlockSpec((1, HEAD_DIM), lambda b, i: (0, 0)),
        pl.BlockSpec((1, HEAD_DIM), lambda b, i: (0, 0)),
        pl.BlockSpec(memory_space=pltpu.SMEM),
    ]
    args = [z, z, z, qw, kw, sink]
    scratch = [pltpu.VMEM((seq, kvw), bf16), pltpu.VMEM((seq, kvw), bf16)]
    out_specs = [pl.BlockSpec((QB, 512), lambda b, i: (b * nqb + i, 0))]
    out_shape = [jax.ShapeDtypeStruct((nb * seq, ATT_Q_HEADS * HEAD_DIM), bf16)]
    if latent:
        in_specs += [pl.BlockSpec((seq, HEAD_DIM), lambda b, i: (0, 0)),
                     pl.BlockSpec((seq, HEAD_DIM), lambda b, i: (0, 0)),
                     pl.BlockSpec((None, PAST_LEN, kvw), lambda b, i: (b, 0, 0)),
                     pl.BlockSpec((None, PAST_LEN, kvw), lambda b, i: (b, 0, 0))]
        args += [rope[0], rope[1], cache[0], cache[1]]
        scratch += [pltpu.VMEM((PAST_LEN, kvw), bf16), pltpu.VMEM((PAST_LEN, kvw), bf16)]
    else:
        kv_out = pl.BlockSpec((seq, kvw), lambda b, i: (b, 0))
        out_specs += [kv_out, kv_out]
        out_shape += [jax.ShapeDtypeStruct((nb * seq, kvw), f32)] * 2
    return pl.pallas_call(
        functools.partial(_attn_kernel, seq=seq, latent=latent),
        grid=(nb, nqb),
        in_specs=in_specs,
        out_specs=out_specs,
        out_shape=out_shape,
        scratch_shapes=scratch,
        compiler_params=_cp(2),
        name="attention_lat" if latent else "attention_ctx",
    )(*args)


def _merge_kernel(h_ref, o0, o1, o2, o3, g0, g1, g2, g3, wbr_ref, *rest):
    out_ref = rest[len(rest) // 2]
    h = h_ref[...]
    acc = None
    for b, (o_ref, g_ref) in enumerate(((o0, g0), (o1, g1), (o2, g2), (o3, g3))):
        gate = _sigmoid(jnp.dot(h, g_ref[...], preferred_element_type=f32))
        t = gate * jnp.dot(o_ref[...], wbr_ref[b], preferred_element_type=f32)
        acc = t if acc is None else acc + t
    out_ref[...] = acc.astype(bf16)
    n_cast = len(rest) // 2
    for src, dst in zip(rest[:n_cast], rest[n_cast + 1:]):
        dst[...] = src[...].astype(bf16)


def _merge(h, branches, w_in, w_br, grp, cast_next=None):
    tn = 512
    g0 = ZW // tn
    gb = D_MODEL // tn
    n_i, n_j = grp.n // TM, D_MODEL // tn
    o_spec = pl.BlockSpec((TM, BRANCH_WIDTH), lambda i, j: (i, 0))

    def gate_spec(b):
        return pl.BlockSpec((D_MODEL, tn), lambda i, j: (0, g0 + b * gb + j))

    in_specs = [
        pl.BlockSpec((TM, D_MODEL), lambda i, j: (i, 0)),
        o_spec, o_spec, o_spec, o_spec,
        gate_spec(0), gate_spec(1), gate_spec(2), gate_spec(3),
        pl.BlockSpec((N_BRANCH, BRANCH_WIDTH, tn), lambda i, j: (0, 0, j)),
    ]
    args = [h, *branches, w_in, w_in, w_in, w_in, w_br]
    out_specs = [pl.BlockSpec((TM, tn), lambda i, j: (i, j))]
    out_shape = [jax.ShapeDtypeStruct((grp.n, D_MODEL), bf16)]
    if cast_next is not None:
        nxt, stacks = cast_next
        for w in stacks:
            rows, cols = w.shape[1:]
            slab = rows // (n_i * n_j)
            in_specs.append(pl.BlockSpec((None, slab, cols), lambda i, j: (nxt, i * n_j + j, 0)))
            out_specs.append(pl.BlockSpec((slab, cols), lambda i, j: (i * n_j + j, 0)))
            out_shape.append(jax.ShapeDtypeStruct((rows, cols), bf16))
            args.append(w)
    res = pl.pallas_call(
        _merge_kernel,
        grid=(n_i, n_j),
        in_specs=in_specs,
        out_specs=out_specs,
        out_shape=out_shape,
        compiler_params=_cp(2, vmem=BIG_VMEM_LIMIT if cast_next is not None else VMEM_LIMIT),
        name="merge",
    )(*args)
    return res[0], res[1:]


def _outproj_kernel(x_ref, mod_ref, m_ref, w_ref, o_ref):
    o_ref[...] = x_ref[...] + mod_ref[2:3, :] * jnp.dot(m_ref[...], w_ref[...], preferred_element_type=f32)


def _out_proj(x, mod, merged, w_out, layer, grp):
    tm = OUT_TM
    return pl.pallas_call(
        _outproj_kernel,
        grid=(grp.n // tm,),
        in_specs=[
            pl.BlockSpec((tm, D_MODEL), lambda i: (i, 0)),
            pl.BlockSpec((None, None, 6, D_MODEL), lambda i: (layer, _mod_row(grp, i, tm), 0, 0)),
            pl.BlockSpec((tm, D_MODEL), lambda i: (i, 0)),
            pl.BlockSpec((D_MODEL, D_MODEL), lambda i: (0, 0)),
        ],
        out_specs=pl.BlockSpec((tm, D_MODEL), lambda i: (i, 0)),
        out_shape=jax.ShapeDtypeStruct((grp.n, D_MODEL), f32),
        compiler_params=_cp(1),
        name="out_proj",
    )(x, mod, merged, w_out)


def _ffn_kernel(x_ref, mod_ref, nw_ref, w1_ref, w2_ref, o_ref, h_scr, r_scr, gs_scr):
    j = pl.program_id(1)

    @pl.when(j == 0)
    def _():
        _modnorm_rows(x_ref, nw_ref[...], mod_ref[3:4, :], mod_ref[4:5, :], h_scr, r_scr, gs_scr, copy_ref=o_ref)

    a = jnp.maximum(jnp.dot(h_scr[...], w1_ref[...], preferred_element_type=f32), 0.0)
    a2 = (a * a).astype(bf16)
    cw = 512
    for cidx in range(D_MODEL // cw):
        cs = slice(cidx * cw, (cidx + 1) * cw)
        o_ref[:, cs] += mod_ref[5:6, cs] * jnp.dot(a2, w2_ref[:, cs], preferred_element_type=f32)


def _ffn(x, mod, nw, w1, w2, layer, grp):
    tf = 1024
    nj = D_FF // tf
    return pl.pallas_call(
        _ffn_kernel,
        grid=(grp.n // TM, nj),
        in_specs=[
            pl.BlockSpec((TM, D_MODEL), lambda i, j: (i, 0)),
            pl.BlockSpec((None, None, 6, D_MODEL), lambda i, j: (layer, _mod_row(grp, i), 0, 0)),
            pl.BlockSpec((None, 1, D_MODEL), lambda i, j: (layer, 0, 0)),
            pl.BlockSpec((D_MODEL, tf), lambda i, j: (0, j)),
            pl.BlockSpec((tf, D_MODEL), lambda i, j: (j, 0)),
        ],
        out_specs=pl.BlockSpec((TM, D_MODEL), lambda i, j: (i, 0)),
        out_shape=jax.ShapeDtypeStruct((grp.n, D_MODEL), f32),
        scratch_shapes=[pltpu.VMEM((TM, D_MODEL), bf16), pltpu.VMEM((TM, 128), f32),
                        pltpu.VMEM((2, SUB, D_MODEL), f32)],
        compiler_params=_cp(2, vmem=BIG_VMEM_LIMIT),
        name="ffn",
    )(x, mod, nw, w1, w2)


def _rope_tables(seq):
    t = jnp.arange(seq)
    row = (t // GRID_W).astype(f32)
    col = (t % GRID_W).astype(f32)
    n_pairs = HEAD_DIM // 4
    freqs = ROPE_BASE ** (-jnp.arange(n_pairs, dtype=f32) / n_pairs)
    ang = jnp.concatenate([row[:, None] * freqs, col[:, None] * freqs], axis=-1)
    cos = jnp.cos(ang)
    sin = jnp.sin(ang)
    return jnp.concatenate([cos, cos], axis=-1), jnp.concatenate([-sin, sin], axis=-1)


def _ssm_params(a_re, a_im, log_dt, b_re, b_im, c_re, c_im):
    lead = a_re.shape[:-2]
    dt = jnp.exp(log_dt)[..., None]
    mag = jnp.exp(a_re * dt)
    ab_re = mag * jnp.cos(a_im * dt)
    ab_im = mag * jnp.sin(a_im * dt)
    den = a_re * a_re + a_im * a_im
    q_re = (((ab_re - 1.0) * a_re + ab_im * a_im) / den)[..., None]
    q_im = ((ab_im * a_re - (ab_re - 1.0) * a_im) / den)[..., None]
    bb_re = q_re * b_re - q_im * b_im
    bb_im = q_re * b_im + q_im * b_re
    a = jnp.stack([ab_re.reshape(lead + (SSM_N,)), ab_im.reshape(lead + (SSM_N,))], axis=-2)

    ncb = SSM_N // SSM_CB
    gpb = SSM_GROUPS // ncb
    eye = jnp.eye(gpb, dtype=f32)[:, None, :, None]

    def diag_blocks(w, rows, cols):
        w = jnp.swapaxes(w.reshape(lead + (ncb, gpb) + w.shape[-2:]), -1, -2)
        return (w[..., None, :] * eye).reshape(lead + (ncb, rows, cols))

    bd = jnp.stack([diag_blocks(w, SSM_CI, SSM_CB) for w in (bb_re, bb_im)], axis=-4).astype(bf16)
    cd = jnp.stack([diag_blocks(w, SSM_CB, SSM_CI) for w in (c_re, -c_im)], axis=-4).astype(bf16)
    return a, bd, cd


def _lru_params(conv_w, conv_b, wa, ba, wx, bx, lam):
    w = LRU_WIDTH
    return {"cw": conv_w, "cb": conv_b.reshape(1, w),
            "wa": (0.5 * wa).astype(bf16), "ba": (0.5 * ba).reshape(2, 1, w),
            "wx": (0.5 * wx).astype(bf16), "bx": (0.5 * bx).reshape(2, 1, w),
            "sp": (-0.5 * LRU_C * jax.nn.softplus(-lam)).reshape(2, 1, w)}


def _mixers(z, grp, layer, prm, states):
    nb, seq = grp.nb, grp.seq
    z3 = z.reshape(nb, seq, ZW)
    s_ret0, s_lru0, s_ssm0, cache = states

    o_ret, s_ret = _retention(z, prm["ret_lg"], prm["ret_gn"], s_ret0, grp)

    hf, lru_f = _lru_dir(z3, None, None if s_lru0 is None else s_lru0[:, 0], prm["lru"], grp, 0)
    o_lru, lru_b = _lru_dir(z3, hf, None if s_lru0 is None else s_lru0[:, 1], prm["lru"], grp, 1)

    att = _attention(z, prm["att_qw"], prm["att_kw"], prm["att_sink"], prm["rope"], cache, grp)

    yf, ssm_f = _s5_dir(z3, None, None if s_ssm0 is None else s_ssm0[:, 0], prm["ssm"], grp, 0)
    o_ssm, ssm_b = _s5_dir(z3, yf, None if s_ssm0 is None else s_ssm0[:, 1], prm["ssm"], grp, 1)

    branches = [o_ret, o_lru.reshape(grp.n, LRU_WIDTH), att[0], o_ssm.reshape(grp.n, SSM_WIDTH)]
    new_state = None
    if grp.ctx:
        ssm_fin = jnp.stack([ssm_f, ssm_b], axis=1)
        new_state = (att[1].reshape(nb, seq, ATT_KV_HEADS, HEAD_DIM), att[2].reshape(nb, seq, ATT_KV_HEADS, HEAD_DIM),
                     s_ret, jnp.stack([lru_f, lru_b], axis=1),
                     ssm_fin[:, :, :SSM_N].reshape(nb, 2, SSM_GROUPS, SSM_STATE),
                     ssm_fin[:, :, SSM_N:].reshape(nb, 2, SSM_GROUPS, SSM_STATE))
    return branches, new_state


def kernel(x_prompt, x_sample, cache_attn_k, cache_attn_v, state_ret, state_lru, state_ssm_re, state_ssm_im, c, c_ctx, w_mod, b_mod, norm1, w_in, ret_decay_logit, ret_gn, lru_conv_w, lru_conv_b, lru_wa, lru_ba, lru_wx, lru_bx, lru_lambda, att_q_norm, att_k_norm, att_sink, ssm_a_re, ssm_a_im, ssm_log_dt, ssm_b_re, ssm_b_im, ssm_c_re, ssm_c_im, ssm_d, ssm_w_glu, ssm_b_glu, w_br, w_out, norm2, w_ff1, w_ff2):
    xs = {PROMPT: x_prompt.reshape(PROMPT.n, D_MODEL), SAMPLE: x_sample.reshape(SAMPLE.n, D_MODEL)}
    cond = jnp.concatenate([c, c_ctx[None, :], jnp.zeros((COND_ROWS - SAMPLE.nb - 1, D_MODEL), f32)], axis=0)
    mod = _modulation(cond, w_mod, b_mod).reshape(DEPTH, COND_ROWS, 6, D_MODEL)

    w_stacks = (w_in, w_br.reshape(DEPTH, N_BRANCH * BRANCH_WIDTH, D_MODEL), w_out, w_ff1, w_ff2)
    w_layer = [w[0].astype(bf16) for w in w_stacks]
    norm1_r = norm1.reshape(DEPTH, 1, D_MODEL)
    norm2_r = norm2.reshape(DEPTH, 1, D_MODEL)
    rope = _rope_tables(SAMPLE.seq)
    log_gamma = -jax.nn.softplus(-ret_decay_logit)
    kvw = ATT_KV_HEADS * HEAD_DIM

    ssm_a_all, ssm_bd_all, ssm_cd_all = _ssm_params(ssm_a_re, ssm_a_im, ssm_log_dt, ssm_b_re, ssm_b_im,
                                                    ssm_c_re, ssm_c_im)
    new_states = []
    for l in range(DEPTH):
        ssm_a, ssm_bd, ssm_cd = ssm_a_all[l], ssm_bd_all[l], ssm_cd_all[l]
        prm = {
            "ret_lg": jnp.broadcast_to(log_gamma[l].T[:, :, None], (RET_HEADS, 2, 128)),
            "ret_gn": ret_gn[l].reshape(1, RET_HEADS * 128),
            "lru": _lru_params(lru_conv_w[l], lru_conv_b[l], lru_wa[l], lru_ba[l], lru_wx[l], lru_bx[l],
                               lru_lambda[l]),
            "att_qw": att_q_norm[l].reshape(1, HEAD_DIM), "att_kw": att_k_norm[l].reshape(1, HEAD_DIM),
            "att_sink": att_sink[l], "rope": rope,
            "ssm": {"a": ssm_a, "bd": ssm_bd, "cd": ssm_cd, "d": ssm_d[l].reshape(1, SSM_WIDTH),
                    "wglu": ssm_w_glu[l].astype(bf16), "bglu": ssm_b_glu[l].reshape(1, SSM_WIDTH)},
        }
        states = {
            PROMPT: (None, None, None, None),
            SAMPLE: (state_ret[:, l], state_lru[:, l],
                     jnp.concatenate([state_ssm_re[:, l].reshape(SAMPLE.nb, 2, SSM_N),
                                      state_ssm_im[:, l].reshape(SAMPLE.nb, 2, SSM_N)], axis=-1),
                     (cache_attn_k[:, l].reshape(SAMPLE.nb, PAST_LEN, kvw),
                      cache_attn_v[:, l].reshape(SAMPLE.nb, PAST_LEN, kvw))),
        }
        w_in_l, w_br_l, w_out_l, w_ff1_l, w_ff2_l = w_layer
        w_br_l = w_br_l.reshape(N_BRANCH, BRANCH_WIDTH, D_MODEL)
        for grp in (PROMPT, SAMPLE):
            x = xs[grp]
            z, h = _proj_in(x, mod, norm1_r, w_in_l, l, grp)
            branches, st = _mixers(z, grp, l, prm, states[grp])
            if grp.ctx:
                new_states.append(st)
            cast_next = (l + 1, w_stacks) if (grp is SAMPLE and l + 1 < DEPTH) else None
            merged, w_next = _merge(h, branches, w_in_l, w_br_l, grp, cast_next)
            if cast_next is not None:
                w_layer = list(w_next)
            x = _out_proj(x, mod, merged, w_out_l, l, grp)
            xs[grp] = _ffn(x, mod, norm2_r, w_ff1_l, w_ff2_l, l, grp)

    y_prompt = xs[PROMPT].reshape(PROMPT.nb, PROMPT.seq, D_MODEL)
    y_sample = xs[SAMPLE].reshape(SAMPLE.nb, SAMPLE.seq, D_MODEL)
    return (y_prompt, y_sample) + tuple(jnp.stack([st[i] for st in new_states], axis=1) for i in range(6))
```

```python
import functools
from typing import NamedTuple

import jax
import jax.numpy as jnp
from jax import lax
from jax.experimental import pallas as pl
from jax.experimental.pallas import tpu as pltpu

f32 = jnp.float32
bf16 = jnp.bfloat16

D_MODEL = 2048
DEPTH = 4
PAST_LEN = 256
GRID_W = 64
CHUNK = 128
ATT_QB_LAT = 128
ATT_QB_CTX = 256
EPS = 1e-6
NEG_INF = -1e30
RET_HEADS = 4
RET_DK = 128
LRU_WIDTH = 512
LRU_BLOCKS = 4
LRU_C = 8.0
CONV_W = 4
ATT_Q_HEADS = 4
ATT_KV_HEADS = 2
HEAD_DIM = 128
WINDOW = 128
ROPE_BASE = 10000.0
SSM_WIDTH = 512
SSM_GROUP = 16
SSM_GROUPS = 32
SSM_STATE = 64
SSM_N = SSM_GROUPS * SSM_STATE
SSM_CB = 512
SSM_CI = SSM_CB // SSM_STATE * SSM_GROUP
N_BRANCH = 4
BRANCH_WIDTH = 512
D_FF = 4 * D_MODEL
RQ, RK, RV, RG, LX, LG, AQ, AK, AV, SU, ZW = 0, 512, 1024, 1536, 2048, 2560, 3072, 3584, 3840, 4096, 4608
SUB = 8
COND_ROWS = 16
VMEM_LIMIT = 56 * 1024 * 1024
BIG_VMEM_LIMIT = 62 * 1024 * 1024

TM = 1024
OUT_TM = 512
LRU_TC = 128
SSM_TC = 128
ROW_BLK = 16


class Group(NamedTuple):
    nb: int
    seq: int
    ctx: bool

    @property
    def n(self):
        return self.nb * self.seq


PROMPT = Group(32, 256, True)
SAMPLE = Group(8, 2048, False)
CTX_ROW = SAMPLE.nb


def _cp(n_axes, vmem=VMEM_LIMIT):
    return pltpu.CompilerParams(dimension_semantics=("arbitrary",) * n_axes, vmem_limit_bytes=vmem)


def _sigmoid(x):
    return 0.5 * jnp.tanh(0.5 * x) + 0.5


def _mod_row(grp, i, tm=None):
    return CTX_ROW if grp.ctx else i // (grp.seq // (tm or TM))


def _modnorm_rows(x_ref, nw, shift, scale, h_ref, r_scr, gs_scr, copy_ref=None):
    tm, d = x_ref.shape
    lane_tiles = d // 128

    def ssq_body(r, carry):
        rows = pl.ds(pl.multiple_of(r * SUB, SUB), SUB)
        acc = None
        for t in range(lane_tiles):
            v = x_ref[rows, t * 128:(t + 1) * 128]
            acc = v * v if acc is None else acc + v * v
        r_scr[rows, :] = acc
        return carry

    lax.fori_loop(0, tm // SUB, ssq_body, 0, unroll=4)
    ssq = jnp.sum(r_scr[...], axis=-1, keepdims=True)
    r_scr[...] = jnp.broadcast_to(lax.rsqrt(ssq * (1.0 / d) + EPS), (tm, 128))
    gs_scr[0] = jnp.broadcast_to(nw * (1.0 + scale), (SUB, d))
    gs_scr[1] = jnp.broadcast_to(shift, (SUB, d))

    def out_body(r, carry):
        r0 = pl.multiple_of(r * ROW_BLK, ROW_BLK)
        halves = [pl.ds(pl.multiple_of(r0 + k * SUB, SUB), SUB) for k in range(ROW_BLK // SUB)]
        invs = [r_scr[rows, :] for rows in halves]
        for t in range(lane_tiles):
            cols = slice(t * 128, (t + 1) * 128)
            gain = gs_scr[0, :, cols]
            shf = gs_scr[1, :, cols]
            xs = [x_ref[rows, cols] for rows in halves]
            if copy_ref is not None:
                for rows, xv in zip(halves, xs):
                    copy_ref[rows, cols] = xv
            parts = [xv * inv * gain + shf for xv, inv in zip(xs, invs)]
            h_ref[pl.ds(r0, ROW_BLK), cols] = jnp.concatenate(parts, axis=0).astype(bf16)
        return carry

    lax.fori_loop(0, tm // ROW_BLK, out_body, 0, unroll=2)


def _mod_kernel(c_ref, w_ref, b_ref, o_ref):
    c = c_ref[...]
    s = (c * _sigmoid(c)).astype(bf16)
    o_ref[...] = jnp.dot(s, w_ref[...].astype(bf16), preferred_element_type=f32) + b_ref[...]


def _modulation(cond, w_mod, b_mod):
    tn = 1024
    n6 = 6 * D_MODEL
    return pl.pallas_call(
        _mod_kernel,
        grid=(DEPTH, n6 // tn),
        in_specs=[
            pl.BlockSpec((COND_ROWS, D_MODEL), lambda l, j: (0, 0)),
            pl.BlockSpec((None, D_MODEL, tn), lambda l, j: (l, 0, j)),
            pl.BlockSpec((None, 1, tn), lambda l, j: (l, 0, j)),
        ],
        out_specs=pl.BlockSpec((None, COND_ROWS, tn), lambda l, j: (l, 0, j)),
        out_shape=jax.ShapeDtypeStruct((DEPTH, COND_ROWS, n6), f32),
        compiler_params=_cp(2),
        name="modulation",
    )(cond, w_mod, b_mod.reshape(DEPTH, 1, n6))


def _proj_kernel(x_ref, mod_ref, nw_ref, w_ref, o_ref, h_ref, r_scr, gs_scr):
    @pl.when(pl.program_id(1) == 0)
    def _():
        _modnorm_rows(x_ref, nw_ref[...], mod_ref[0:1, :], mod_ref[1:2, :], h_ref, r_scr, gs_scr)

    o_ref[...] = jnp.dot(h_ref[...], w_ref[...], preferred_element_type=f32)


def _proj_in(x, mod, nw, w_in, layer, grp):
    tn = 1536
    return pl.pallas_call(
        _proj_kernel,
        grid=(grp.n // TM, ZW // tn),
        in_specs=[
            pl.BlockSpec((TM, D_MODEL), lambda i, j: (i, 0)),
            pl.BlockSpec((None, None, 6, D_MODEL), lambda i, j: (layer, _mod_row(grp, i), 0, 0)),
            pl.BlockSpec((None, 1, D_MODEL), lambda i, j: (layer, 0, 0)),
            pl.BlockSpec((D_MODEL, tn), lambda i, j: (0, j)),
        ],
        out_specs=[pl.BlockSpec((TM, tn), lambda i, j: (i, j)),
                   pl.BlockSpec((TM, D_MODEL), lambda i, j: (i, 0))],
        out_shape=[jax.ShapeDtypeStruct((grp.n, ZW), f32), jax.ShapeDtypeStruct((grp.n, D_MODEL), bf16)],
        scratch_shapes=[pltpu.VMEM((TM, 128), f32), pltpu.VMEM((2, SUB, D_MODEL), f32)],
        compiler_params=_cp(2),
        name="proj_in",
    )(x, mod, nw, w_in)


def _ret_kernel(*refs, nc, has_s0, emit_state):
    q_ref, k_ref, v_ref, g_ref, lg_ref, gn_ref = refs[:6]
    pos = 6
    s0_ref = None
    if has_s0:
        s0_ref = refs[pos]
        pos += 1
    o_ref = refs[pos]
    pos += 1
    sout_ref = None
    if emit_state:
        sout_ref = refs[pos]
        pos += 1
    w_scr, kv_scr = refs[pos], refs[pos + 1]

    C = CHUNK
    H = RET_HEADS
    DEC, W_IN_F, W_IN_B, W_END_F, W_END_B = range(5)
    tn_dims = (((0,), (0,)), ((), ()))
    nt_dims = (((1,), (1,)), ((), ()))

    @pl.when(pl.program_id(0) == 0)
    def _():
        ii = lax.broadcasted_iota(jnp.int32, (C, C), 0).astype(f32)
        jj = lax.broadcasted_iota(jnp.int32, (C, C), 1).astype(f32)
        rel = ii - jj
        for h in range(H):
            lgf = lg_ref[h, 0:1, :]
            lgb = lg_ref[h, 1:2, :]
            w_scr[h, DEC] = (jnp.where(rel >= 0, jnp.exp(jnp.maximum(rel, 0.0) * lgf), 0.0)
                             + jnp.where(rel <= 0, jnp.exp(jnp.maximum(-rel, 0.0) * lgb), 0.0))
            w_scr[h, W_IN_F] = jnp.exp((ii + 1.0) * lgf)
            w_scr[h, W_IN_B] = jnp.exp((C - ii) * lgb)
            w_scr[h, W_END_F] = jnp.exp((C - 1.0 - ii) * lgf)
            w_scr[h, W_END_B] = jnp.exp(ii * lgb)

    def kv_body(n, carry):
        r = pl.multiple_of(n * C, C)
        for h in range(H):
            cols = slice(h * C, (h + 1) * C)
            k = k_ref[pl.ds(r, C), cols]
            vb = v_ref[pl.ds(r, C), cols].astype(bf16)
            kw = jnp.concatenate([k * w_scr[h, W_END_F], k * w_scr[h, W_END_B]], axis=1).astype(bf16)
            kv = lax.dot_general(kw, vb, tn_dims, preferred_element_type=f32)
            kv_scr[0, n * H + h] = kv[:C]
            kv_scr[1, n * H + h] = kv[C:]
        return carry

    lax.fori_loop(0, nc, kv_body, 0, unroll=min(nc, 4))

    for h in range(H):
        cd_f = jnp.exp(C * lg_ref[h, 0:1, :])
        cd_b = jnp.exp(C * lg_ref[h, 1:2, :])

        def fwd_body(n, s, h=h, cd_f=cd_f):
            kv = kv_scr[0, n * H + h]
            kv_scr[0, n * H + h] = s
            return cd_f * s + kv

        def bwd_body(i, s, h=h, cd_b=cd_b):
            n = nc - 1 - i
            kv = kv_scr[1, n * H + h]
            kv_scr[1, n * H + h] = s
            return cd_b * s + kv

        s_f = lax.fori_loop(0, nc, fwd_body, s0_ref[0, h] if has_s0 else jnp.zeros((C, C), f32))
        s_b = lax.fori_loop(0, nc, bwd_body, s0_ref[1, h] if has_s0 else jnp.zeros((C, C), f32))
        if emit_state:
            sout_ref[0, h] = s_f
            sout_ref[1, h] = s_b

    def out_body(n, carry):
        r = pl.multiple_of(n * C, C)
        for h in range(H):
            cols = slice(h * C, (h + 1) * C)
            q = q_ref[pl.ds(r, C), cols] * (RET_DK ** -0.5)
            kb = k_ref[pl.ds(r, C), cols].astype(bf16)
            vb = v_ref[pl.ds(r, C), cols].astype(bf16)
            sc = lax.dot_general(q.astype(bf16), kb, nt_dims, preferred_element_type=f32) * w_scr[h, DEC]
            lhs = jnp.concatenate([sc, q * w_scr[h, W_IN_F], q * w_scr[h, W_IN_B]], axis=1).astype(bf16)
            rhs = jnp.concatenate([vb, kv_scr[0, n * H + h].astype(bf16), kv_scr[1, n * H + h].astype(bf16)], axis=0)
            o = jnp.dot(lhs, rhs, preferred_element_type=f32)
            mu = jnp.mean(o, axis=-1, keepdims=True)
            d = o - mu
            var = jnp.mean(d * d, axis=-1, keepdims=True)
            on = d * lax.rsqrt(var + EPS) * gn_ref[:, cols]
            g = g_ref[pl.ds(r, C), cols]
            o_ref[pl.ds(r, C), cols] = (g * _sigmoid(g) * on).astype(bf16)
        return carry

    lax.fori_loop(0, nc, out_body, 0, unroll=min(nc, 8))


def _retention(z, lg, gn, s0, grp):
    seq = grp.seq
    W = RET_HEADS * 128
    in_specs = [
        pl.BlockSpec((seq, W), lambda b: (b, RQ // W)),
        pl.BlockSpec((seq, W), lambda b: (b, RK // W)),
        pl.BlockSpec((seq, W), lambda b: (b, RV // W)),
        pl.BlockSpec((seq, W), lambda b: (b, RG // W)),
        pl.BlockSpec((RET_HEADS, 2, 128), lambda b: (0, 0, 0)),
        pl.BlockSpec((1, W), lambda b: (0, 0)),
    ]
    args = [z, z, z, z, lg, gn]
    st_spec = pl.BlockSpec((None, 2, RET_HEADS, 128, 128), lambda b: (b, 0, 0, 0, 0))
    if s0 is not None:
        in_specs.append(st_spec)
        args.append(s0)
    out_specs = [pl.BlockSpec((seq, W), lambda b: (b, 0))]
    out_shape = [jax.ShapeDtypeStruct((grp.n, W), bf16)]
    if grp.ctx:
        out_specs.append(st_spec)
        out_shape.append(jax.ShapeDtypeStruct((grp.nb, 2, RET_HEADS, 128, 128), f32))
    nc = seq // CHUNK
    res = pl.pallas_call(
        functools.partial(_ret_kernel, nc=nc, has_s0=s0 is not None, emit_state=grp.ctx),
        grid=(grp.nb,),
        in_specs=in_specs,
        out_specs=out_specs,
        out_shape=out_shape,
        scratch_shapes=[pltpu.VMEM((RET_HEADS, 5, 128, 128), f32),
                        pltpu.VMEM((2, nc * RET_HEADS, 128, 128), f32)],
        compiler_params=_cp(1),
        name="retention",
    )(*args)
    return res if grp.ctx else (res[0], None)


def _gather_tm(src_ref, nt, dst_ref, row0, unroll=2):
    w = src_ref.shape[-1]

    def body(i, carry):
        v = src_ref[:, pl.ds(pl.multiple_of(i * SUB, SUB), SUB), :]
        rows = pl.ds(pl.multiple_of(row0 + i * SUB * SUB, SUB), SUB * SUB)
        dst_ref[rows, :] = jnp.swapaxes(v, 0, 1).reshape(SUB * SUB, w)
        return carry

    lax.fori_loop(0, nt // SUB, body, 0, unroll=unroll)


def _scatter_bm(src_ref, nt, dst_ref, unroll=2):
    w = src_ref.shape[-1]

    def body(i, carry):
        v = src_ref[pl.ds(pl.multiple_of(i * SUB * SUB, SUB * SUB), SUB * SUB), :].reshape(SUB, SUB, w)
        dst_ref[:, pl.ds(pl.multiple_of(i * SUB, SUB), SUB), :] = jnp.swapaxes(v, 0, 1)
        return carry

    lax.fori_loop(0, nt // SUB, body, 0, unroll=unroll)


def _lru_kernel(*refs, tc, nch, direction, has_s0):
    if direction == 0:
        xprev_ref, xcur_ref, xnext_ref, cw_ref, cb_ref = refs[:5]
        pos = 5
    else:
        xc_ref = refs[0]
        pos = 1
    wa_ref, ba_ref, wx_ref, bx_ref, sp_ref = refs[pos:pos + 5]
    pos += 5
    s0_ref = None
    if has_s0:
        s0_ref = refs[pos]
        pos += 1
    if direction == 0:
        out_ref, xc_out_ref, fin_ref = refs[pos:pos + 3]
        xe_scr, a_scr, b_scr, h_scr = refs[pos + 3:pos + 7]
    else:
        gate_ref, hf_ref, out_ref, fin_ref = refs[pos:pos + 4]
        a_scr, b_scr, h_scr, g_scr, y_scr = refs[pos + 4:pos + 9]

    R = tc * SUB
    W = LRU_WIDTH
    j = pl.program_id(1)
    c = j if direction == 0 else nch - 1 - j

    @pl.when(j == 0)
    def _():
        h_scr[...] = s0_ref[...] if has_s0 else jnp.zeros((SUB, W), f32)

    if direction == 0:
        for t in range(2):
            v = xprev_ref[:, SUB - 2 + t:SUB - 1 + t, :].reshape(SUB, W)
            xe_scr[t * SUB:(t + 1) * SUB, :] = jnp.where(c > 0, v, 0.0)
        _gather_tm(xcur_ref, tc, xe_scr, 2 * SUB)
        xe_scr[2 * SUB + R:3 * SUB + R, :] = jnp.where(c < nch - 1, xnext_ref[:, 0:1, :].reshape(SUB, W), 0.0)
        xc = cb_ref[...] + xe_scr[0:R, :] * cw_ref[0:1, :]
        for t in range(1, CONV_W):
            xc = xc + xe_scr[t * SUB:t * SUB + R, :] * cw_ref[t:t + 1, :]
        xc_out_ref[...] = xc
    else:
        xc = xc_ref[...]
        _gather_tm(gate_ref, tc, g_scr, 0)

    bd = W // LRU_BLOCKS
    for n in range(LRU_BLOCKS):
        sl = slice(n * bd, (n + 1) * bd)
        xs = xc[:, sl]
        xb = xs.astype(bf16)
        tr = jnp.tanh(jnp.dot(xb, wa_ref[n], preferred_element_type=f32) + ba_ref[:, sl])
        ti = jnp.tanh(jnp.dot(xb, wx_ref[n], preferred_element_type=f32) + bx_ref[:, sl])
        log_a = sp_ref[:, sl] * (tr + 1.0)
        a = jnp.exp(log_a)
        a_scr[:, sl] = a
        t = -jnp.tanh(log_a) * (a * a + 1.0)
        root = jnp.where(t > 0.0, t * lax.rsqrt(t), 0.0)
        b_scr[:, sl] = (0.5 * root) * ((ti + 1.0) * xs)

    def step(s, h):
        t = s if direction == 0 else tc - 1 - s
        r0 = pl.multiple_of(t * SUB, SUB)
        h = a_scr[pl.ds(r0, SUB), :] * h + b_scr[pl.ds(r0, SUB), :]
        b_scr[pl.ds(r0, SUB), :] = h
        return h

    h = lax.fori_loop(0, tc, step, h_scr[...], unroll=8)
    h_scr[...] = h
    fin_ref[...] = h
    if direction == 0:
        out_ref[...] = b_scr[...]
    else:
        b_scr[...] = jax.nn.gelu(g_scr[...]) * (hf_ref[...] + b_scr[...])
        _scatter_bm(b_scr, tc, y_scr)
        out_ref[...] = y_scr[...].astype(bf16)


def _lru_dir(z3, prev, s0, prm, grp, direction):
    nb, seq = grp.nb, grp.seq
    nbg = nb // SUB
    tc = min(LRU_TC, seq)
    nch = seq // tc
    R = tc * SUB
    W = LRU_WIDTH

    def cidx(j):
        return j if direction == 0 else nch - 1 - j

    tm_spec = pl.BlockSpec((None, R, W), lambda g, j: (g, cidx(j), 0))
    tm_sds = jax.ShapeDtypeStruct((nbg, seq * SUB, W), f32)
    st_spec = pl.BlockSpec((SUB, W), lambda g, j: (g, 0))
    st_sds = jax.ShapeDtypeStruct((nb, W), f32)
    row_spec = pl.BlockSpec((1, W), lambda g, j: (0, 0))
    gate_w_spec = pl.BlockSpec((LRU_BLOCKS, 128, 128), lambda g, j: (0, 0, 0))
    if direction == 0:
        in_specs = [
            pl.BlockSpec((SUB, SUB, W), lambda g, j: (g, jnp.maximum(cidx(j) * (tc // SUB) - 1, 0), LX // W)),
            pl.BlockSpec((SUB, tc, W), lambda g, j: (g, cidx(j), LX // W)),
            pl.BlockSpec((SUB, SUB, W),
                         lambda g, j: (g, jnp.minimum((cidx(j) + 1) * (tc // SUB), seq // SUB - 1), LX // W)),
            pl.BlockSpec((CONV_W, W), lambda g, j: (0, 0)),
            row_spec,
        ]
        args = [z3, z3, z3, prm["cw"], prm["cb"]]
    else:
        hf_tm, xc_tm = prev
        in_specs = [tm_spec]
        args = [xc_tm]
    in_specs += [gate_w_spec, row_spec, gate_w_spec, row_spec, row_spec]
    args += [prm["wa"][direction], prm["ba"][direction], prm["wx"][direction], prm["bx"][direction],
             prm["sp"][direction]]
    if s0 is not None:
        in_specs.append(st_spec)
        args.append(s0)
    chunk_scr = pltpu.VMEM((R, W), f32)
    if direction == 0:
        out_specs = [tm_spec, tm_spec, st_spec]
        out_shape = [tm_sds, tm_sds, st_sds]
        scratch = [pltpu.VMEM((R + 3 * SUB, W), f32), chunk_scr, chunk_scr, pltpu.VMEM((SUB, W), f32)]
    else:
        in_specs += [pl.BlockSpec((SUB, tc, W), lambda g, j: (g, cidx(j), LG // W)), tm_spec]
        args += [z3, hf_tm]
        out_specs = [pl.BlockSpec((SUB, tc, W), lambda g, j: (g, cidx(j), 0)), st_spec]
        out_shape = [jax.ShapeDtypeStruct((nb, seq, W), bf16), st_sds]
        scratch = [chunk_scr, chunk_scr, pltpu.VMEM((SUB, W), f32), chunk_scr, pltpu.VMEM((SUB, tc, W), f32)]
    return pl.pallas_call(
        functools.partial(_lru_kernel, tc=tc, nch=nch, direction=direction, has_s0=s0 is not None),
        grid=(nbg, nch),
        in_specs=in_specs,
        out_specs=out_specs,
        out_shape=out_shape,
        scratch_shapes=scratch,
        compiler_params=_cp(2),
        name="rglru_dir%d" % direction,
    )(*args)


def _s5_kernel(*refs, tc, nch, direction, has_s0):
    u_ref, bd_ref, a_ref, cd_ref = refs[:4]
    pos = 4
    s0_ref = None
    if has_s0:
        s0_ref = refs[pos]
        pos += 1
    if direction == 1:
        yf_ref, dv_ref, wg_ref, bg_ref = refs[pos:pos + 4]
        pos += 4
    out_ref, fin_ref = refs[pos], refs[pos + 1]
    u_scr, ub_scr, hs_scr, h_scr = refs[pos + 2: pos + 6]
    if direction == 1:
        y_scr = refs[pos + 6]

    j = pl.program_id(1)

    @pl.when(j == 0)
    def _():
        h_scr[...] = s0_ref[...] if has_s0 else jnp.zeros((SUB, 2 * SSM_N), f32)

    _gather_tm(u_ref, tc, u_scr, 0, unroll=True)
    ub_scr[...] = u_scr[...].astype(bf16)

    cbw = SSM_CB
    for cb in range(SSM_N // cbw):
        cre = slice(cb * cbw, (cb + 1) * cbw)
        cim = slice(SSM_N + cb * cbw, SSM_N + (cb + 1) * cbw)
        cin = slice(cb * SSM_CI, (cb + 1) * SSM_CI)
        hs_scr[:, cre] = jnp.dot(ub_scr[:, cin], bd_ref[0, cb], preferred_element_type=f32)
        hs_scr[:, cim] = jnp.dot(ub_scr[:, cin], bd_ref[1, cb], preferred_element_type=f32)
        a_re = jnp.broadcast_to(a_ref[0:1, cre], (SUB, cbw))
        a_im = jnp.broadcast_to(a_ref[1:2, cre], (SUB, cbw))

        def step(s, carry):
            hr, hi = carry
            t = s if direction == 0 else tc - 1 - s
            r0 = pl.multiple_of(t * SUB, SUB)
            nr = a_re * hr - a_im * hi + hs_scr[pl.ds(r0, SUB), cre]
            ni = a_re * hi + a_im * hr + hs_scr[pl.ds(r0, SUB), cim]
            hs_scr[pl.ds(r0, SUB), cre] = nr
            hs_scr[pl.ds(r0, SUB), cim] = ni
            return nr, ni

        hr, hi = lax.fori_loop(0, tc, step, (h_scr[:, cre], h_scr[:, cim]), unroll=True)
        h_scr[:, cre] = hr
        h_scr[:, cim] = hi

    fin_ref[...] = h_scr[...]
    y_parts = []
    for cb in range(SSM_N // cbw):
        cre = slice(cb * cbw, (cb + 1) * cbw)
        cim = slice(SSM_N + cb * cbw, SSM_N + (cb + 1) * cbw)
        y_parts.append(jnp.dot(hs_scr[:, cre].astype(bf16), cd_ref[0, cb], preferred_element_type=f32)
                       + jnp.dot(hs_scr[:, cim].astype(bf16), cd_ref[1, cb], preferred_element_type=f32))
    y = jnp.concatenate(y_parts, axis=1)
    if direction == 0:
        out_ref[...] = y
    else:
        yy = jax.nn.gelu(yf_ref[...] + y + dv_ref[...] * u_scr[...])
        gl = jnp.dot(yy.astype(bf16), wg_ref[...], preferred_element_type=f32) + bg_ref[...]
        u_scr[...] = yy * _sigmoid(gl)
        _scatter_bm(u_scr, tc, y_scr, unroll=True)
        out_ref[...] = y_scr[...].astype(bf16)


def _s5_dir(z3, yf_tm, s0, prm, grp, direction):
    nb, seq = grp.nb, grp.seq
    nbg = nb // SUB
    tc = min(SSM_TC, seq)
    nch = seq // tc
    R = tc * SUB
    W = SSM_WIDTH
    ncb = SSM_N // SSM_CB

    def cidx(j):
        return j if direction == 0 else nch - 1 - j

    tm_spec = pl.BlockSpec((None, R, W), lambda g, j: (g, cidx(j), 0))
    st_spec = pl.BlockSpec((SUB, 2 * SSM_N), lambda g, j: (g, 0))
    in_specs = [
        pl.BlockSpec((SUB, tc, W), lambda g, j: (g, cidx(j), SU // W)),
        pl.BlockSpec((2, ncb, SSM_CI, SSM_CB), lambda g, j: (0, 0, 0, 0)),
        pl.BlockSpec((2, SSM_N), lambda g, j: (0, 0)),
        pl.BlockSpec((2, ncb, SSM_CB, SSM_CI), lambda g, j: (0, 0, 0, 0)),
    ]
    args = [z3, prm["bd"][direction], prm["a"][direction], prm["cd"][direction]]
    if s0 is not None:
        in_specs.append(st_spec)
        args.append(s0)
    scratch = [pltpu.VMEM((R, W), f32), pltpu.VMEM((R, W), bf16), pltpu.VMEM((R, 2 * SSM_N), f32),
               pltpu.VMEM((SUB, 2 * SSM_N), f32)]
    if direction == 0:
        out_spec = tm_spec
        out_sds = jax.ShapeDtypeStruct((nbg, seq * SUB, W), f32)
    else:
        in_specs += [tm_spec, pl.BlockSpec((1, W), lambda g, j: (0, 0)),
                     pl.BlockSpec((W, W), lambda g, j: (0, 0)), pl.BlockSpec((1, W), lambda g, j: (0, 0))]
        args += [yf_tm, prm["d"], prm["wglu"], prm["bglu"]]
        out_spec = pl.BlockSpec((SUB, tc, W), lambda g, j: (g, cidx(j), 0))
        out_sds = jax.ShapeDtypeStruct((nb, seq, W), bf16)
        scratch.append(pltpu.VMEM((SUB, tc, W), f32))
    return pl.pallas_call(
        functools.partial(_s5_kernel, tc=tc, nch=nch, direction=direction, has_s0=s0 is not None),
        grid=(nbg, nch),
        in_specs=in_specs,
        out_specs=[out_spec, st_spec],
        out_shape=[out_sds, jax.ShapeDtypeStruct((nb, 2 * SSM_N), f32)],
        scratch_shapes=scratch,
        compiler_params=_cp(2),
        name="s5_dir%d" % direction,
    )(*args)


def _rope(x, cos, sin_signed):
    return x * cos + pltpu.roll(x, HEAD_DIM // 2, 1) * sin_signed


def _unit_rms(x, w):
    return x * lax.rsqrt(jnp.mean(x * x, axis=-1, keepdims=True) + EPS) * w


def _attn_kernel(*refs, seq, latent):
    q_ref, k_ref, v_ref, qw_ref, kw_ref, sink_ref = refs[:6]
    pos = 6
    if latent:
        cos_ref, sin_ref, ck_ref, cv_ref = refs[pos:pos + 4]
        pos += 4
    o_ref = refs[pos]
    pos += 1
    if not latent:
        kout_ref, vout_ref = refs[pos], refs[pos + 1]
        pos += 2
    kb_scr, vb_scr = refs[pos], refs[pos + 1]
    pos += 2
    if latent:
        ckb_scr, cvb_scr = refs[pos], refs[pos + 1]

    HD = HEAD_DIM
    qb = pl.program_id(1)

    @pl.when(qb == 0)
    def _():
        for hk in range(ATT_KV_HEADS):
            sl = slice(hk * HD, (hk + 1) * HD)
            kn = _unit_rms(k_ref[:, sl], kw_ref[...])
            if latent:
                kn = _rope(kn, cos_ref[...], sin_ref[...])
            else:
                kout_ref[:, sl] = kn
            kb_scr[:, sl] = kn.astype(bf16)
        v = v_ref[...]
        vb_scr[...] = v.astype(bf16)
        if latent:
            ckb_scr[...] = ck_ref[...].astype(bf16)
            cvb_scr[...] = cv_ref[...].astype(bf16)
        else:
            vout_ref[...] = v

    QB = q_ref.shape[0]
    q0 = pl.multiple_of(qb * QB, QB)
    if latent:
        win = QB + 2 * WINDOW
        start = pl.multiple_of(jnp.clip(q0 - WINDOW, 0, seq - win), WINDOW)
        qpos = q0 + (lax.broadcasted_iota(jnp.int32, (2 * QB, win), 0) & (QB - 1))
        kpos = start + lax.broadcasted_iota(jnp.int32, (2 * QB, win), 1)
        band = jnp.abs(qpos - kpos) <= WINDOW
        cos_q = cos_ref[pl.ds(q0, QB), :]
        sin_q = sin_ref[pl.ds(q0, QB), :]
    else:
        win = seq
        start = 0
    nt_dims = (((1,), (1,)), ((), ()))

    def lane_tiles(x):
        return [x[:, t * HD:(t + 1) * HD] for t in range(x.shape[1] // HD)]

    def fold(op, tiles):
        acc = tiles[0]
        for t in tiles[1:]:
            acc = op(acc, t)
        return acc

    row = lax.broadcasted_iota(jnp.int32, (2 * QB, 1), 0)
    for hk in range(ATT_KV_HEADS):
        sl = slice(hk * HD, (hk + 1) * HD)
        qs = []
        for g in range(2):
            hq = hk * 2 + g
            qn = _unit_rms(q_ref[:, hq * HD:(hq + 1) * HD], qw_ref[...])
            if latent:
                qn = _rope(qn, cos_q, sin_q)
            qs.append((qn * (HD ** -0.5)).astype(bf16))
        qg = jnp.concatenate(qs, axis=0)
        snk = jnp.where(row < QB, sink_ref[hk * 2], sink_ref[hk * 2 + 1])
        s1 = lax.dot_general(qg, kb_scr[pl.ds(start, win), sl], nt_dims, preferred_element_type=f32)
        if latent:
            s1 = jnp.where(band, s1, NEG_INF)
            s2 = lax.dot_general(qg, ckb_scr[:, sl], nt_dims, preferred_element_type=f32)
        s_tiles = lane_tiles(s1) + (lane_tiles(s2) if latent else [])
        m = jnp.maximum(jnp.max(fold(jnp.maximum, s_tiles), axis=-1, keepdims=True), snk)
        p1 = jnp.exp(s1 - m)
        if latent:
            p2 = jnp.exp(s2 - m)
        p_tiles = lane_tiles(p1) + (lane_tiles(p2) if latent else [])
        den = jnp.sum(fold(jnp.add, p_tiles), axis=-1, keepdims=True) + jnp.exp(snk - m)
        o = jnp.dot((p1 / den).astype(bf16), vb_scr[pl.ds(start, win), sl], preferred_element_type=f32)
        if latent:
            o += jnp.dot((p2 / den).astype(bf16), cvb_scr[:, sl], preferred_element_type=f32)
        for g in range(2):
            hq = hk * 2 + g
            o_ref[:, hq * HD:(hq + 1) * HD] = o[g * QB:(g + 1) * QB].astype(bf16)


def _attention(z, qw, kw, sink, rope, cache, grp):
    nb, seq = grp.nb, grp.seq
    latent = not grp.ctx
    QB = min(seq, ATT_QB_LAT if latent else ATT_QB_CTX)
    nqb = seq // QB
    kvw = ATT_KV_HEADS * HEAD_DIM
    in_specs = [
        pl.BlockSpec((QB, 512), lambda b, i: (b * nqb + i, AQ // 512)),
        pl.BlockSpec((seq, kvw), lambda b, i: (b, AK // kvw)),
        pl.BlockSpec((seq, kvw), lambda b, i: (b, AV // kvw)),
        pl.BlockSpec((1, HEAD_DIM), lambda b, i: (0, 0)),
        pl.BlockSpec((1, HEAD_DIM), lambda b, i: (0, 0)),
        pl.BlockSpec(memory_space=pltpu.SMEM),
    ]
    args = [z, z, z, qw, kw, sink]
    scratch = [pltpu.VMEM((seq, kvw), bf16), pltpu.VMEM((seq, kvw), bf16)]
    out_specs = [pl.BlockSpec((QB, 512), lambda b, i: (b * nqb + i, 0))]
    out_shape = [jax.ShapeDtypeStruct((nb * seq, ATT_Q_HEADS * HEAD_DIM), bf16)]
    if latent:
        in_specs += [pl.BlockSpec((seq, HEAD_DIM), lambda b, i: (0, 0)),
                     pl.BlockSpec((seq, HEAD_DIM), lambda b, i: (0, 0)),
                     pl.BlockSpec((None, PAST_LEN, kvw), lambda b, i: (b, 0, 0)),
                     pl.BlockSpec((None, PAST_LEN, kvw), lambda b, i: (b, 0, 0))]
        args += [rope[0], rope[1], cache[0], cache[1]]
        scratch += [pltpu.VMEM((PAST_LEN, kvw), bf16), pltpu.VMEM((PAST_LEN, kvw), bf16)]
    else:
        kv_out = pl.BlockSpec((seq, kvw), lambda b, i: (b, 0))
        out_specs += [kv_out, kv_out]
        out_shape += [jax.ShapeDtypeStruct((nb * seq, kvw), f32)] * 2
    return pl.pallas_call(
        functools.partial(_attn_kernel, seq=seq, latent=latent),
        grid=(nb, nqb),
        in_specs=in_specs,
        out_specs=out_specs,
        out_shape=out_shape,
        scratch_shapes=scratch,
        compiler_params=_cp(2),
        name="attention_lat" if latent else "attention_ctx",
    )(*args)


def _merge_kernel(h_ref, o0, o1, o2, o3, g0, g1, g2, g3, wbr_ref, *rest):
    out_ref = rest[len(rest) // 2]
    h = h_ref[...]
    acc = None
    for b, (o_ref, g_ref) in enumerate(((o0, g0), (o1, g1), (o2, g2), (o3, g3))):
        gate = _sigmoid(jnp.dot(h, g_ref[...], preferred_element_type=f32))
        t = gate * jnp.dot(o_ref[...], wbr_ref[b], preferred_element_type=f32)
        acc = t if acc is None else acc + t
    out_ref[...] = acc.astype(bf16)
    n_cast = len(rest) // 2
    for src, dst in zip(rest[:n_cast], rest[n_cast + 1:]):
        dst[...] = src[...].astype(bf16)


def _merge(h, branches, w_in, w_br, grp, cast_next=None):
    tn = 512
    g0 = ZW // tn
    gb = D_MODEL // tn
    n_i, n_j = grp.n // TM, D_MODEL // tn
    o_spec = pl.BlockSpec((TM, BRANCH_WIDTH), lambda i, j: (i, 0))

    def gate_spec(b):
        return pl.BlockSpec((D_MODEL, tn), lambda i, j: (0, g0 + b * gb + j))

    in_specs = [
        pl.BlockSpec((TM, D_MODEL), lambda i, j: (i, 0)),
        o_spec, o_spec, o_spec, o_spec,
        gate_spec(0), gate_spec(1), gate_spec(2), gate_spec(3),
        pl.BlockSpec((N_BRANCH, BRANCH_WIDTH, tn), lambda i, j: (0, 0, j)),
    ]
    args = [h, *branches, w_in, w_in, w_in, w_in, w_br]
    out_specs = [pl.BlockSpec((TM, tn), lambda i, j: (i, j))]
    out_shape = [jax.ShapeDtypeStruct((grp.n, D_MODEL), bf16)]
    if cast_next is not None:
        nxt, stacks = cast_next
        for w in stacks:
            rows, cols = w.shape[1:]
            slab = rows // (n_i * n_j)
            in_specs.append(pl.BlockSpec((None, slab, cols), lambda i, j: (nxt, i * n_j + j, 0)))
            out_specs.append(pl.BlockSpec((slab, cols), lambda i, j: (i * n_j + j, 0)))
            out_shape.append(jax.ShapeDtypeStruct((rows, cols), bf16))
            args.append(w)
    res = pl.pallas_call(
        _merge_kernel,
        grid=(n_i, n_j),
        in_specs=in_specs,
        out_specs=out_specs,
        out_shape=out_shape,
        compiler_params=_cp(2, vmem=BIG_VMEM_LIMIT if cast_next is not None else VMEM_LIMIT),
        name="merge",
    )(*args)
    return res[0], res[1:]


def _outproj_kernel(x_ref, mod_ref, m_ref, w_ref, o_ref):
    o_ref[...] = x_ref[...] + mod_ref[2:3, :] * jnp.dot(m_ref[...], w_ref[...], preferred_element_type=f32)


def _out_proj(x, mod, merged, w_out, layer, grp):
    tm = OUT_TM
    return pl.pallas_call(
        _outproj_kernel,
        grid=(grp.n // tm,),
        in_specs=[
            pl.BlockSpec((tm, D_MODEL), lambda i: (i, 0)),
            pl.BlockSpec((None, None, 6, D_MODEL), lambda i: (layer, _mod_row(grp, i, tm), 0, 0)),
            pl.BlockSpec((tm, D_MODEL), lambda i: (i, 0)),
            pl.BlockSpec((D_MODEL, D_MODEL), lambda i: (0, 0)),
        ],
        out_specs=pl.BlockSpec((tm, D_MODEL), lambda i: (i, 0)),
        out_shape=jax.ShapeDtypeStruct((grp.n, D_MODEL), f32),
        compiler_params=_cp(1),
        name="out_proj",
    )(x, mod, merged, w_out)


def _ffn_kernel(x_ref, mod_ref, nw_ref, w1_ref, w2_ref, o_ref, h_scr, r_scr, gs_scr):
    j = pl.program_id(1)

    @pl.when(j == 0)
    def _():
        _modnorm_rows(x_ref, nw_ref[...], mod_ref[3:4, :], mod_ref[4:5, :], h_scr, r_scr, gs_scr, copy_ref=o_ref)

    a = jnp.maximum(jnp.dot(h_scr[...], w1_ref[...], preferred_element_type=f32), 0.0)
    a2 = (a * a).astype(bf16)
    cw = 512
    for cidx in range(D_MODEL // cw):
        cs = slice(cidx * cw, (cidx + 1) * cw)
        o_ref[:, cs] += mod_ref[5:6, cs] * jnp.dot(a2, w2_ref[:, cs], preferred_element_type=f32)


def _ffn(x, mod, nw, w1, w2, layer, grp):
    tf = 1024
    nj = D_FF // tf
    return pl.pallas_call(
        _ffn_kernel,
        grid=(grp.n // TM, nj),
        in_specs=[
            pl.BlockSpec((TM, D_MODEL), lambda i, j: (i, 0)),
            pl.BlockSpec((None, None, 6, D_MODEL), lambda i, j: (layer, _mod_row(grp, i), 0, 0)),
            pl.BlockSpec((None, 1, D_MODEL), lambda i, j: (layer, 0, 0)),
            pl.BlockSpec((D_MODEL, tf), lambda i, j: (0, j)),
            pl.BlockSpec((tf, D_MODEL), lambda i, j: (j, 0)),
        ],
        out_specs=pl.BlockSpec((TM, D_MODEL), lambda i, j: (i, 0)),
        out_shape=jax.ShapeDtypeStruct((grp.n, D_MODEL), f32),
        scratch_shapes=[pltpu.VMEM((TM, D_MODEL), bf16), pltpu.VMEM((TM, 128), f32),
                        pltpu.VMEM((2, SUB, D_MODEL), f32)],
        compiler_params=_cp(2, vmem=BIG_VMEM_LIMIT),
        name="ffn",
    )(x, mod, nw, w1, w2)


def _rope_tables(seq):
    t = jnp.arange(seq)
    row = (t // GRID_W).astype(f32)
    col = (t % GRID_W).astype(f32)
    n_pairs = HEAD_DIM // 4
    freqs = ROPE_BASE ** (-jnp.arange(n_pairs, dtype=f32) / n_pairs)
    ang = jnp.concatenate([row[:, None] * freqs, col[:, None] * freqs], axis=-1)
    cos = jnp.cos(ang)
    sin = jnp.sin(ang)
    return jnp.concatenate([cos, cos], axis=-1), jnp.concatenate([-sin, sin], axis=-1)


def _ssm_params(a_re, a_im, log_dt, b_re, b_im, c_re, c_im):
    lead = a_re.shape[:-2]
    dt = jnp.exp(log_dt)[..., None]
    mag = jnp.exp(a_re * dt)
    ab_re = mag * jnp.cos(a_im * dt)
    ab_im = mag * jnp.sin(a_im * dt)
    den = a_re * a_re + a_im * a_im
    q_re = (((ab_re - 1.0) * a_re + ab_im * a_im) / den)[..., None]
    q_im = ((ab_im * a_re - (ab_re - 1.0) * a_im) / den)[..., None]
    bb_re = q_re * b_re - q_im * b_im
    bb_im = q_re * b_im + q_im * b_re
    a = jnp.stack([ab_re.reshape(lead + (SSM_N,)), ab_im.reshape(lead + (SSM_N,))], axis=-2)

    ncb = SSM_N // SSM_CB
    gpb = SSM_GROUPS // ncb
    eye = jnp.eye(gpb, dtype=f32)[:, None, :, None]

    def diag_blocks(w, rows, cols):
        w = jnp.swapaxes(w.reshape(lead + (ncb, gpb) + w.shape[-2:]), -1, -2)
        return (w[..., None, :] * eye).reshape(lead + (ncb, rows, cols))

    bd = jnp.stack([diag_blocks(w, SSM_CI, SSM_CB) for w in (bb_re, bb_im)], axis=-4).astype(bf16)
    cd = jnp.stack([diag_blocks(w, SSM_CB, SSM_CI) for w in (c_re, -c_im)], axis=-4).astype(bf16)
    return a, bd, cd


def _lru_params(conv_w, conv_b, wa, ba, wx, bx, lam):
    w = LRU_WIDTH
    return {"cw": conv_w, "cb": conv_b.reshape(1, w),
            "wa": (0.5 * wa).astype(bf16), "ba": (0.5 * ba).reshape(2, 1, w),
            "wx": (0.5 * wx).astype(bf16), "bx": (0.5 * bx).reshape(2, 1, w),
            "sp": (-0.5 * LRU_C * jax.nn.softplus(-lam)).reshape(2, 1, w)}


def _mixers(z, grp, layer, prm, states):
    nb, seq = grp.nb, grp.seq
    z3 = z.reshape(nb, seq, ZW)
    s_ret0, s_lru0, s_ssm0, cache = states

    o_ret, s_ret = _retention(z, prm["ret_lg"], prm["ret_gn"], s_ret0, grp)

    hf, xc, lru_f = _lru_dir(z3, None, None if s_lru0 is None else s_lru0[:, 0], prm["lru"], grp, 0)
    o_lru, lru_b = _lru_dir(z3, (hf, xc), None if s_lru0 is None else s_lru0[:, 1], prm["lru"], grp, 1)

    att = _attention(z, prm["att_qw"], prm["att_kw"], prm["att_sink"], prm["rope"], cache, grp)

    yf, ssm_f = _s5_dir(z3, None, None if s_ssm0 is None else s_ssm0[:, 0], prm["ssm"], grp, 0)
    o_ssm, ssm_b = _s5_dir(z3, yf, None if s_ssm0 is None else s_ssm0[:, 1], prm["ssm"], grp, 1)

    branches = [o_ret, o_lru.reshape(grp.n, LRU_WIDTH), att[0], o_ssm.reshape(grp.n, SSM_WIDTH)]
    new_state = None
    if grp.ctx:
        ssm_fin = jnp.stack([ssm_f, ssm_b], axis=1)
        new_state = (att[1].reshape(nb, seq, ATT_KV_HEADS, HEAD_DIM), att[2].reshape(nb, seq, ATT_KV_HEADS, HEAD_DIM),
                     s_ret, jnp.stack([lru_f, lru_b], axis=1),
                     ssm_fin[:, :, :SSM_N].reshape(nb, 2, SSM_GROUPS, SSM_STATE),
                     ssm_fin[:, :, SSM_N:].reshape(nb, 2, SSM_GROUPS, SSM_STATE))
    return branches, new_state


def kernel(x_prompt, x_sample, cache_attn_k, cache_attn_v, state_ret, state_lru, state_ssm_re, state_ssm_im, c, c_ctx, w_mod, b_mod, norm1, w_in, ret_decay_logit, ret_gn, lru_conv_w, lru_conv_b, lru_wa, lru_ba, lru_wx, lru_bx, lru_lambda, att_q_norm, att_k_norm, att_sink, ssm_a_re, ssm_a_im, ssm_log_dt, ssm_b_re, ssm_b_im, ssm_c_re, ssm_c_im, ssm_d, ssm_w_glu, ssm_b_glu, w_br, w_out, norm2, w_ff1, w_ff2):
    xs = {PROMPT: x_prompt.reshape(PROMPT.n, D_MODEL), SAMPLE: x_sample.reshape(SAMPLE.n, D_MODEL)}
    cond = jnp.concatenate([c, c_ctx[None, :], jnp.zeros((COND_ROWS - SAMPLE.nb - 1, D_MODEL), f32)], axis=0)
    mod = _modulation(cond, w_mod, b_mod).reshape(DEPTH, COND_ROWS, 6, D_MODEL)

    w_stacks = (w_in, w_br.reshape(DEPTH, N_BRANCH * BRANCH_WIDTH, D_MODEL), w_out, w_ff1, w_ff2)
    w_layer = [w[0].astype(bf16) for w in w_stacks]
    norm1_r = norm1.reshape(DEPTH, 1, D_MODEL)
    norm2_r = norm2.reshape(DEPTH, 1, D_MODEL)
    rope = _rope_tables(SAMPLE.seq)
    log_gamma = -jax.nn.softplus(-ret_decay_logit)
    kvw = ATT_KV_HEADS * HEAD_DIM

    ssm_a_all, ssm_bd_all, ssm_cd_all = _ssm_params(ssm_a_re, ssm_a_im, ssm_log_dt, ssm_b_re, ssm_b_im,
                                                    ssm_c_re, ssm_c_im)
    new_states = []
    for l in range(DEPTH):
        ssm_a, ssm_bd, ssm_cd = ssm_a_all[l], ssm_bd_all[l], ssm_cd_all[l]
        prm = {
            "ret_lg": jnp.broadcast_to(log_gamma[l].T[:, :, None], (RET_HEADS, 2, 128)),
            "ret_gn": ret_gn[l].reshape(1, RET_HEADS * 128),
            "lru": _lru_params(lru_conv_w[l], lru_conv_b[l], lru_wa[l], lru_ba[l], lru_wx[l], lru_bx[l],
                               lru_lambda[l]),
            "att_qw": att_q_norm[l].reshape(1, HEAD_DIM), "att_kw": att_k_norm[l].reshape(1, HEAD_DIM),
            "att_sink": att_sink[l], "rope": rope,
            "ssm": {"a": ssm_a, "bd": ssm_bd, "cd": ssm_cd, "d": ssm_d[l].reshape(1, SSM_WIDTH),
                    "wglu": ssm_w_glu[l].astype(bf16), "bglu": ssm_b_glu[l].reshape(1, SSM_WIDTH)},
        }
        states = {
            PROMPT: (None, None, None, None),
            SAMPLE: (state_ret[:, l], state_lru[:, l],
                     jnp.concatenate([state_ssm_re[:, l].reshape(SAMPLE.nb, 2, SSM_N),
                                      state_ssm_im[:, l].reshape(SAMPLE.nb, 2, SSM_N)], axis=-1),
                     (cache_attn_k[:, l].reshape(SAMPLE.nb, PAST_LEN, kvw),
                      cache_attn_v[:, l].reshape(SAMPLE.nb, PAST_LEN, kvw))),
        }
        w_in_l, w_br_l, w_out_l, w_ff1_l, w_ff2_l = w_layer
        w_br_l = w_br_l.reshape(N_BRANCH, BRANCH_WIDTH, D_MODEL)
        for grp in (PROMPT, SAMPLE):
            x = xs[grp]
            z, h = _proj_in(x, mod, norm1_r, w_in_l, l, grp)
            branches, st = _mixers(z, grp, l, prm, states[grp])
            if grp.ctx:
                new_states.append(st)
            cast_next = (l + 1, w_stacks) if (grp is SAMPLE and l + 1 < DEPTH) else None
            merged, w_next = _merge(h, branches, w_in_l, w_br_l, grp, cast_next)
            if cast_next is not None:
                w_layer = list(w_next)
            x = _out_proj(x, mod, merged, w_out_l, l, grp)
            xs[grp] = _ffn(x, mod, norm2_r, w_ff1_l, w_ff2_l, l, grp)

    y_prompt = xs[PROMPT].reshape(PROMPT.nb, PROMPT.seq, D_MODEL)
    y_sample = xs[SAMPLE].reshape(SAMPLE.nb, SAMPLE.seq, D_MODEL)
    return (y_prompt, y_sample) + tuple(jnp.stack([st[i] for st in new_states], axis=1) for i in range(6))
```

```python
import functools
from typing import NamedTuple

import jax
import jax.numpy as jnp
from jax import lax
from jax.experimental import pallas as pl
from jax.experimental.pallas import tpu as pltpu

f32 = jnp.float32
bf16 = jnp.bfloat16

D_MODEL = 2048
DEPTH = 4
PAST_LEN = 256
GRID_W = 64
CHUNK = 128
ATT_QB_LAT = 128
ATT_QB_CTX = 256
EPS = 1e-6
NEG_INF = -1e30
RET_HEADS = 4
RET_DK = 128
LRU_WIDTH = 512
LRU_BLOCKS = 4
LRU_C = 8.0
CONV_W = 4
ATT_Q_HEADS = 4
ATT_KV_HEADS = 2
HEAD_DIM = 128
WINDOW = 128
ROPE_BASE = 10000.0
SSM_WIDTH = 512
SSM_GROUP = 16
SSM_GROUPS = 32
SSM_STATE = 64
SSM_N = SSM_GROUPS * SSM_STATE
SSM_CB = 512
SSM_CI = SSM_CB // SSM_STATE * SSM_GROUP
N_BRANCH = 4
BRANCH_WIDTH = 512
D_FF = 4 * D_MODEL
RQ, RK, RV, RG, LX, LG, AQ, AK, AV, SU, ZW = 0, 512, 1024, 1536, 2048, 2560, 3072, 3584, 3840, 4096, 4608
SUB = 8
COND_ROWS = 16
VMEM_LIMIT = 56 * 1024 * 1024
BIG_VMEM_LIMIT = 62 * 1024 * 1024

TM = 1024
OUT_TM = 512
LRU_TC = 128
SSM_TC = 128
ROW_BLK = 16


class Group(NamedTuple):
    nb: int
    seq: int
    ctx: bool

    @property
    def n(self):
        return self.nb * self.seq


PROMPT = Group(32, 256, True)
SAMPLE = Group(8, 2048, False)
CTX_ROW = SAMPLE.nb


def _cp(n_axes, vmem=VMEM_LIMIT):
    return pltpu.CompilerParams(dimension_semantics=("arbitrary",) * n_axes, vmem_limit_bytes=vmem)


def _sigmoid(x):
    return 0.5 * jnp.tanh(0.5 * x) + 0.5


def _mod_row(grp, i, tm=None):
    return CTX_ROW if grp.ctx else i // (grp.seq // (tm or TM))


def _modnorm_rows(x_ref, nw, shift, scale, h_ref, r_scr, gs_scr, copy_ref=None):
    tm, d = x_ref.shape
    lane_tiles = d // 128

    def ssq_body(r, carry):
        rows = pl.ds(pl.multiple_of(r * SUB, SUB), SUB)
        acc = None
        for t in range(lane_tiles):
            v = x_ref[rows, t * 128:(t + 1) * 128]
            acc = v * v if acc is None else acc + v * v
        r_scr[rows, :] = acc
        return carry

    lax.fori_loop(0, tm // SUB, ssq_body, 0, unroll=4)
    ssq = jnp.sum(r_scr[...], axis=-1, keepdims=True)
    r_scr[...] = jnp.broadcast_to(lax.rsqrt(ssq * (1.0 / d) + EPS), (tm, 128))
    gs_scr[0] = jnp.broadcast_to(nw * (1.0 + scale), (SUB, d))
    gs_scr[1] = jnp.broadcast_to(shift, (SUB, d))

    def out_body(r, carry):
        r0 = pl.multiple_of(r * ROW_BLK, ROW_BLK)
        halves = [pl.ds(pl.multiple_of(r0 + k * SUB, SUB), SUB) for k in range(ROW_BLK // SUB)]
        invs = [r_scr[rows, :] for rows in halves]
        for t in range(lane_tiles):
            cols = slice(t * 128, (t + 1) * 128)
            gain = gs_scr[0, :, cols]
            shf = gs_scr[1, :, cols]
            xs = [x_ref[rows, cols] for rows in halves]
            if copy_ref is not None:
                for rows, xv in zip(halves, xs):
                    copy_ref[rows, cols] = xv
            parts = [xv * inv * gain + shf for xv, inv in zip(xs, invs)]
            h_ref[pl.ds(r0, ROW_BLK), cols] = jnp.concatenate(parts, axis=0).astype(bf16)
        return carry

    lax.fori_loop(0, tm // ROW_BLK, out_body, 0, unroll=2)


def _mod_kernel(c_ref, w_ref, b_ref, o_ref):
    c = c_ref[...]
    s = (c * _sigmoid(c)).astype(bf16)
    o_ref[...] = jnp.dot(s, w_ref[...].astype(bf16), preferred_element_type=f32) + b_ref[...]


def _modulation(cond, w_mod, b_mod):
    tn = 1024
    n6 = 6 * D_MODEL
    return pl.pallas_call(
        _mod_kernel,
        grid=(DEPTH, n6 // tn),
        in_specs=[
            pl.BlockSpec((COND_ROWS, D_MODEL), lambda l, j: (0, 0)),
            pl.BlockSpec((None, D_MODEL, tn), lambda l, j: (l, 0, j)),
            pl.BlockSpec((None, 1, tn), lambda l, j: (l, 0, j)),
        ],
        out_specs=pl.BlockSpec((None, COND_ROWS, tn), lambda l, j: (l, 0, j)),
        out_shape=jax.ShapeDtypeStruct((DEPTH, COND_ROWS, n6), f32),
        compiler_params=_cp(2),
        name="modulation",
    )(cond, w_mod, b_mod.reshape(DEPTH, 1, n6))


def _proj_kernel(x_ref, mod_ref, nw_ref, w_ref, o_ref, h_ref, r_scr, gs_scr):
    @pl.when(pl.program_id(1) == 0)
    def _():
        _modnorm_rows(x_ref, nw_ref[...], mod_ref[0:1, :], mod_ref[1:2, :], h_ref, r_scr, gs_scr)

    o_ref[...] = jnp.dot(h_ref[...], w_ref[...], preferred_element_type=f32)


def _proj_in(x, mod, nw, w_in, layer, grp):
    tn = 1536
    return pl.pallas_call(
        _proj_kernel,
        grid=(grp.n // TM, ZW // tn),
        in_specs=[
            pl.BlockSpec((TM, D_MODEL), lambda i, j: (i, 0)),
            pl.BlockSpec((None, None, 6, D_MODEL), lambda i, j: (layer, _mod_row(grp, i), 0, 0)),
            pl.BlockSpec((None, 1, D_MODEL), lambda i, j: (layer, 0, 0)),
            pl.BlockSpec((D_MODEL, tn), lambda i, j: (0, j)),
        ],
        out_specs=[pl.BlockSpec((TM, tn), lambda i, j: (i, j)),
                   pl.BlockSpec((TM, D_MODEL), lambda i, j: (i, 0))],
        out_shape=[jax.ShapeDtypeStruct((grp.n, ZW), f32), jax.ShapeDtypeStruct((grp.n, D_MODEL), bf16)],
        scratch_shapes=[pltpu.VMEM((TM, 128), f32), pltpu.VMEM((2, SUB, D_MODEL), f32)],
        compiler_params=_cp(2),
        name="proj_in",
    )(x, mod, nw, w_in)


def _ret_kernel(*refs, nc, has_s0, emit_state):
    q_ref, k_ref, v_ref, g_ref, lg_ref, gn_ref = refs[:6]
    pos = 6
    s0_ref = None
    if has_s0:
        s0_ref = refs[pos]
        pos += 1
    o_ref = refs[pos]
    pos += 1
    sout_ref = None
    if emit_state:
        sout_ref = refs[pos]
        pos += 1
    w_scr, kv_scr = refs[pos], refs[pos + 1]

    C = CHUNK
    H = RET_HEADS
    DEC, W_IN_F, W_IN_B, W_END_F, W_END_B = range(5)
    tn_dims = (((0,), (0,)), ((), ()))
    nt_dims = (((1,), (1,)), ((), ()))

    @pl.when(pl.program_id(0) == 0)
    def _():
        ii = lax.broadcasted_iota(jnp.int32, (C, C), 0).astype(f32)
        jj = lax.broadcasted_iota(jnp.int32, (C, C), 1).astype(f32)
        rel = ii - jj
        for h in range(H):
            lgf = lg_ref[h, 0:1, :]
            lgb = lg_ref[h, 1:2, :]
            w_scr[h, DEC] = (jnp.where(rel >= 0, jnp.exp(jnp.maximum(rel, 0.0) * lgf), 0.0)
                             + jnp.where(rel <= 0, jnp.exp(jnp.maximum(-rel, 0.0) * lgb), 0.0))
            w_scr[h, W_IN_F] = jnp.exp((ii + 1.0) * lgf)
            w_scr[h, W_IN_B] = jnp.exp((C - ii) * lgb)
            w_scr[h, W_END_F] = jnp.exp((C - 1.0 - ii) * lgf)
            w_scr[h, W_END_B] = jnp.exp(ii * lgb)

    def kv_body(n, carry):
        r = pl.multiple_of(n * C, C)
        for h in range(H):
            cols = slice(h * C, (h + 1) * C)
            k = k_ref[pl.ds(r, C), cols]
            vb = v_ref[pl.ds(r, C), cols].astype(bf16)
            kw = jnp.concatenate([k * w_scr[h, W_END_F], k * w_scr[h, W_END_B]], axis=1).astype(bf16)
            kv = lax.dot_general(kw, vb, tn_dims, preferred_element_type=f32)
            kv_scr[0, n * H + h] = kv[:C]
            kv_scr[1, n * H + h] = kv[C:]
        return carry

    lax.fori_loop(0, nc, kv_body, 0, unroll=min(nc, 4))

    for h in range(H):
        cd_f = jnp.exp(C * lg_ref[h, 0:1, :])
        cd_b = jnp.exp(C * lg_ref[h, 1:2, :])

        def fwd_body(n, s, h=h, cd_f=cd_f):
            kv = kv_scr[0, n * H + h]
            kv_scr[0, n * H + h] = s
            return cd_f * s + kv

        def bwd_body(i, s, h=h, cd_b=cd_b):
            n = nc - 1 - i
            kv = kv_scr[1, n * H + h]
            kv_scr[1, n * H + h] = s
            return cd_b * s + kv

        s_f = lax.fori_loop(0, nc, fwd_body, s0_ref[0, h] if has_s0 else jnp.zeros((C, C), f32))
        s_b = lax.fori_loop(0, nc, bwd_body, s0_ref[1, h] if has_s0 else jnp.zeros((C, C), f32))
        if emit_state:
            sout_ref[0, h] = s_f
            sout_ref[1, h] = s_b

    def out_body(n, carry):
        r = pl.multiple_of(n * C, C)
        for h in range(H):
            cols = slice(h * C, (h + 1) * C)
            q = q_ref[pl.ds(r, C), cols] * (RET_DK ** -0.5)
            kb = k_ref[pl.ds(r, C), cols].astype(bf16)
            vb = v_ref[pl.ds(r, C), cols].astype(bf16)
            sc = lax.dot_general(q.astype(bf16), kb, nt_dims, preferred_element_type=f32) * w_scr[h, DEC]
            lhs = jnp.concatenate([sc, q * w_scr[h, W_IN_F], q * w_scr[h, W_IN_B]], axis=1).astype(bf16)
            rhs = jnp.concatenate([vb, kv_scr[0, n * H + h].astype(bf16), kv_scr[1, n * H + h].astype(bf16)], axis=0)
            o = jnp.dot(lhs, rhs, preferred_element_type=f32)
            mu = jnp.mean(o, axis=-1, keepdims=True)
            d = o - mu
            var = jnp.mean(d * d, axis=-1, keepdims=True)
            on = d * lax.rsqrt(var + EPS) * gn_ref[:, cols]
            g = g_ref[pl.ds(r, C), cols]
            o_ref[pl.ds(r, C), cols] = (g * _sigmoid(g) * on).astype(bf16)
        return carry

    lax.fori_loop(0, nc, out_body, 0, unroll=True)


def _retention(z, lg, gn, s0, grp):
    seq = grp.seq
    W = RET_HEADS * 128
    in_specs = [
        pl.BlockSpec((seq, W), lambda b: (b, RQ // W)),
        pl.BlockSpec((seq, W), lambda b: (b, RK // W)),
        pl.BlockSpec((seq, W), lambda b: (b, RV // W)),
        pl.BlockSpec((seq, W), lambda b: (b, RG // W)),
        pl.BlockSpec((RET_HEADS, 2, 128), lambda b: (0, 0, 0)),
        pl.BlockSpec((1, W), lambda b: (0, 0)),
    ]
    args = [z, z, z, z, lg, gn]
    st_spec = pl.BlockSpec((None, 2, RET_HEADS, 128, 128), lambda b: (b, 0, 0, 0, 0))
    if s0 is not None:
        in_specs.append(st_spec)
        args.append(s0)
    out_specs = [pl.BlockSpec((seq, W), lambda b: (b, 0))]
    out_shape = [jax.ShapeDtypeStruct((grp.n, W), bf16)]
    if grp.ctx:
        out_specs.append(st_spec)
        out_shape.append(jax.ShapeDtypeStruct((grp.nb, 2, RET_HEADS, 128, 128), f32))
    nc = seq // CHUNK
    res = pl.pallas_call(
        functools.partial(_ret_kernel, nc=nc, has_s0=s0 is not None, emit_state=grp.ctx),
        grid=(grp.nb,),
        in_specs=in_specs,
        out_specs=out_specs,
        out_shape=out_shape,
        scratch_shapes=[pltpu.VMEM((RET_HEADS, 5, 128, 128), f32),
                        pltpu.VMEM((2, nc * RET_HEADS, 128, 128), f32)],
        compiler_params=_cp(1),
        name="retention",
    )(*args)
    return res if grp.ctx else (res[0], None)


def _gather_tm(src_ref, nt, dst_ref, row0, unroll=2):
    w = src_ref.shape[-1]

    def body(i, carry):
        v = src_ref[:, pl.ds(pl.multiple_of(i * SUB, SUB), SUB), :]
        rows = pl.ds(pl.multiple_of(row0 + i * SUB * SUB, SUB), SUB * SUB)
        dst_ref[rows, :] = jnp.swapaxes(v, 0, 1).reshape(SUB * SUB, w)
        return carry

    lax.fori_loop(0, nt // SUB, body, 0, unroll=unroll)


def _scatter_bm(src_ref, nt, dst_ref, unroll=2):
    w = src_ref.shape[-1]

    def body(i, carry):
        v = src_ref[pl.ds(pl.multiple_of(i * SUB * SUB, SUB * SUB), SUB * SUB), :].reshape(SUB, SUB, w)
        dst_ref[:, pl.ds(pl.multiple_of(i * SUB, SUB), SUB), :] = jnp.swapaxes(v, 0, 1)
        return carry

    lax.fori_loop(0, nt // SUB, body, 0, unroll=unroll)


def _lru_kernel(*refs, tc, nch, direction, has_s0):
    if direction == 0:
        xprev_ref, xcur_ref, xnext_ref, cw_ref, cb_ref = refs[:5]
        pos = 5
    else:
        xc_ref = refs[0]
        pos = 1
    wa_ref, ba_ref, wx_ref, bx_ref, sp_ref = refs[pos:pos + 5]
    pos += 5
    s0_ref = None
    if has_s0:
        s0_ref = refs[pos]
        pos += 1
    if direction == 0:
        out_ref, xc_out_ref, fin_ref = refs[pos:pos + 3]
        xe_scr, a_scr, b_scr, h_scr = refs[pos + 3:pos + 7]
    else:
        gate_ref, hf_ref, out_ref, fin_ref = refs[pos:pos + 4]
        a_scr, b_scr, h_scr, g_scr, y_scr = refs[pos + 4:pos + 9]

    R = tc * SUB
    W = LRU_WIDTH
    j = pl.program_id(1)
    c = j if direction == 0 else nch - 1 - j

    @pl.when(j == 0)
    def _():
        h_scr[...] = s0_ref[...] if has_s0 else jnp.zeros((SUB, W), f32)

    if direction == 0:
        for t in range(2):
            v = xprev_ref[:, SUB - 2 + t:SUB - 1 + t, :].reshape(SUB, W)
            xe_scr[t * SUB:(t + 1) * SUB, :] = jnp.where(c > 0, v, 0.0)
        _gather_tm(xcur_ref, tc, xe_scr, 2 * SUB)
        xe_scr[2 * SUB + R:3 * SUB + R, :] = jnp.where(c < nch - 1, xnext_ref[:, 0:1, :].reshape(SUB, W), 0.0)
        xc = cb_ref[...] + xe_scr[0:R, :] * cw_ref[0:1, :]
        for t in range(1, CONV_W):
            xc = xc + xe_scr[t * SUB:t * SUB + R, :] * cw_ref[t:t + 1, :]
        xc_out_ref[...] = xc
    else:
        xc = xc_ref[...]
        _gather_tm(gate_ref, tc, g_scr, 0)

    bd = W // LRU_BLOCKS
    for n in range(LRU_BLOCKS):
        sl = slice(n * bd, (n + 1) * bd)
        xs = xc[:, sl]
        xb = xs.astype(bf16)
        tr = jnp.tanh(jnp.dot(xb, wa_ref[n], preferred_element_type=f32) + ba_ref[:, sl])
        ti = jnp.tanh(jnp.dot(xb, wx_ref[n], preferred_element_type=f32) + bx_ref[:, sl])
        log_a = sp_ref[:, sl] * (tr + 1.0)
        a = jnp.exp(log_a)
        a_scr[:, sl] = a
        t = -jnp.tanh(log_a) * (a * a + 1.0)
        root = jnp.where(t > 0.0, t * lax.rsqrt(t), 0.0)
        b_scr[:, sl] = (0.5 * root) * ((ti + 1.0) * xs)

    def step(s, h):
        t = s if direction == 0 else tc - 1 - s
        r0 = pl.multiple_of(t * SUB, SUB)
        h = a_scr[pl.ds(r0, SUB), :] * h + b_scr[pl.ds(r0, SUB), :]
        b_scr[pl.ds(r0, SUB), :] = h
        return h

    h = lax.fori_loop(0, tc, step, h_scr[...], unroll=8)
    h_scr[...] = h
    fin_ref[...] = h
    if direction == 0:
        out_ref[...] = b_scr[...]
    else:
        b_scr[...] = jax.nn.gelu(g_scr[...]) * (hf_ref[...] + b_scr[...])
        _scatter_bm(b_scr, tc, y_scr)
        out_ref[...] = y_scr[...].astype(bf16)


def _lru_dir(z3, prev, s0, prm, grp, direction):
    nb, seq = grp.nb, grp.seq
    nbg = nb // SUB
    tc = min(LRU_TC, seq)
    nch = seq // tc
    R = tc * SUB
    W = LRU_WIDTH

    def cidx(j):
        return j if direction == 0 else nch - 1 - j

    tm_spec = pl.BlockSpec((None, R, W), lambda g, j: (g, cidx(j), 0))
    tm_sds = jax.ShapeDtypeStruct((nbg, seq * SUB, W), f32)
    st_spec = pl.BlockSpec((SUB, W), lambda g, j: (g, 0))
    st_sds = jax.ShapeDtypeStruct((nb, W), f32)
    row_spec = pl.BlockSpec((1, W), lambda g, j: (0, 0))
    gate_w_spec = pl.BlockSpec((LRU_BLOCKS, 128, 128), lambda g, j: (0, 0, 0))
    if direction == 0:
        in_specs = [
            pl.BlockSpec((SUB, SUB, W), lambda g, j: (g, jnp.maximum(cidx(j) * (tc // SUB) - 1, 0), LX // W)),
            pl.BlockSpec((SUB, tc, W), lambda g, j: (g, cidx(j), LX // W)),
            pl.BlockSpec((SUB, SUB, W),
                         lambda g, j: (g, jnp.minimum((cidx(j) + 1) * (tc // SUB), seq // SUB - 1), LX // W)),
            pl.BlockSpec((CONV_W, W), lambda g, j: (0, 0)),
            row_spec,
        ]
        args = [z3, z3, z3, prm["cw"], prm["cb"]]
    else:
        hf_tm, xc_tm = prev
        in_specs = [tm_spec]
        args = [xc_tm]
    in_specs += [gate_w_spec, row_spec, gate_w_spec, row_spec, row_spec]
    args += [prm["wa"][direction], prm["ba"][direction], prm["wx"][direction], prm["bx"][direction],
             prm["sp"][direction]]
    if s0 is not None:
        in_specs.append(st_spec)
        args.append(s0)
    chunk_scr = pltpu.VMEM((R, W), f32)
    if direction == 0:
        out_specs = [tm_spec, tm_spec, st_spec]
        out_shape = [tm_sds, tm_sds, st_sds]
        scratch = [pltpu.VMEM((R + 3 * SUB, W), f32), chunk_scr, chunk_scr, pltpu.VMEM((SUB, W), f32)]
    else:
        in_specs += [pl.BlockSpec((SUB, tc, W), lambda g, j: (g, cidx(j), LG // W)), tm_spec]
        args += [z3, hf_tm]
        out_specs = [pl.BlockSpec((SUB, tc, W), lambda g, j: (g, cidx(j), 0)), st_spec]
        out_shape = [jax.ShapeDtypeStruct((nb, seq, W), bf16), st_sds]
        scratch = [chunk_scr, chunk_scr, pltpu.VMEM((SUB, W), f32), chunk_scr, pltpu.VMEM((SUB, tc, W), f32)]
    return pl.pallas_call(
        functools.partial(_lru_kernel, tc=tc, nch=nch, direction=direction, has_s0=s0 is not None),
        grid=(nbg, nch),
        in_specs=in_specs,
        out_specs=out_specs,
        out_shape=out_shape,
        scratch_shapes=scratch,
        compiler_params=_cp(2),
        name="rglru_dir%d" % direction,
    )(*args)


def _s5_kernel(*refs, tc, nch, direction, has_s0):
    u_ref, bd_ref, a_ref, cd_ref = refs[:4]
    pos = 4
    s0_ref = None
    if has_s0:
        s0_ref = refs[pos]
        pos += 1
    if direction == 1:
        yf_ref, dv_ref, wg_ref, bg_ref = refs[pos:pos + 4]
        pos += 4
    out_ref, fin_ref = refs[pos], refs[pos + 1]
    u_scr, ub_scr, hs_scr, h_scr = refs[pos + 2: pos + 6]
    if direction == 1:
        y_scr = refs[pos + 6]

    j = pl.program_id(1)

    @pl.when(j == 0)
    def _():
        h_scr[...] = s0_ref[...] if has_s0 else jnp.zeros((SUB, 2 * SSM_N), f32)

    _gather_tm(u_ref, tc, u_scr, 0, unroll=True)
    ub_scr[...] = u_scr[...].astype(bf16)

    cbw = SSM_CB
    for cb in range(SSM_N // cbw):
        cre = slice(cb * cbw, (cb + 1) * cbw)
        cim = slice(SSM_N + cb * cbw, SSM_N + (cb + 1) * cbw)
        cin = slice(cb * SSM_CI, (cb + 1) * SSM_CI)
        hs_scr[:, cre] = jnp.dot(ub_scr[:, cin], bd_ref[0, cb], preferred_element_type=f32)
        hs_scr[:, cim] = jnp.dot(ub_scr[:, cin], bd_ref[1, cb], preferred_element_type=f32)
        a_re = jnp.broadcast_to(a_ref[0:1, cre], (SUB, cbw))
        a_im = jnp.broadcast_to(a_ref[1:2, cre], (SUB, cbw))

        def step(s, carry):
            hr, hi = carry
            t = s if direction == 0 else tc - 1 - s
            r0 = pl.multiple_of(t * SUB, SUB)
            nr = a_re * hr - a_im * hi + hs_scr[pl.ds(r0, SUB), cre]
            ni = a_re * hi + a_im * hr + hs_scr[pl.ds(r0, SUB), cim]
            hs_scr[pl.ds(r0, SUB), cre] = nr
            hs_scr[pl.ds(r0, SUB), cim] = ni
            return nr, ni

        hr, hi = lax.fori_loop(0, tc, step, (h_scr[:, cre], h_scr[:, cim]), unroll=True)
        h_scr[:, cre] = hr
        h_scr[:, cim] = hi

    fin_ref[...] = h_scr[...]
    y_parts = []
    for cb in range(SSM_N // cbw):
        cre = slice(cb * cbw, (cb + 1) * cbw)
        cim = slice(SSM_N + cb * cbw, SSM_N + (cb + 1) * cbw)
        y_parts.append(jnp.dot(hs_scr[:, cre].astype(bf16), cd_ref[0, cb], preferred_element_type=f32)
                       + jnp.dot(hs_scr[:, cim].astype(bf16), cd_ref[1, cb], preferred_element_type=f32))
    y = jnp.concatenate(y_parts, axis=1)
    if direction == 0:
        out_ref[...] = y
    else:
        yy = jax.nn.gelu(yf_ref[...] + y + dv_ref[...] * u_scr[...])
        gl = jnp.dot(yy.astype(bf16), wg_ref[...], preferred_element_type=f32) + bg_ref[...]
        u_scr[...] = yy * _sigmoid(gl)
        _scatter_bm(u_scr, tc, y_scr, unroll=True)
        out_ref[...] = y_scr[...].astype(bf16)


def _s5_dir(z3, yf_tm, s0, prm, grp, direction):
    nb, seq = grp.nb, grp.seq
    nbg = nb // SUB
    tc = min(SSM_TC, seq)
    nch = seq // tc
    R = tc * SUB
    W = SSM_WIDTH
    ncb = SSM_N // SSM_CB

    def cidx(j):
        return j if direction == 0 else nch - 1 - j

    tm_spec = pl.BlockSpec((None, R, W), lambda g, j: (g, cidx(j), 0))
    st_spec = pl.BlockSpec((SUB, 2 * SSM_N), lambda g, j: (g, 0))
    in_specs = [
        pl.BlockSpec((SUB, tc, W), lambda g, j: (g, cidx(j), SU // W)),
        pl.BlockSpec((2, ncb, SSM_CI, SSM_CB), lambda g, j: (0, 0, 0, 0)),
        pl.BlockSpec((2, SSM_N), lambda g, j: (0, 0)),
        pl.BlockSpec((2, ncb, SSM_CB, SSM_CI), lambda g, j: (0, 0, 0, 0)),
    ]
    args = [z3, prm["bd"][direction], prm["a"][direction], prm["cd"][direction]]
    if s0 is not None:
        in_specs.append(st_spec)
        args.append(s0)
    scratch = [pltpu.VMEM((R, W), f32), pltpu.VMEM((R, W), bf16), pltpu.VMEM((R, 2 * SSM_N), f32),
               pltpu.VMEM((SUB, 2 * SSM_N), f32)]
    if direction == 0:
        out_spec = tm_spec
        out_sds = jax.ShapeDtypeStruct((nbg, seq * SUB, W), f32)
    else:
        in_specs += [tm_spec, pl.BlockSpec((1, W), lambda g, j: (0, 0)),
                     pl.BlockSpec((W, W), lambda g, j: (0, 0)), pl.BlockSpec((1, W), lambda g, j: (0, 0))]
        args += [yf_tm, prm["d"], prm["wglu"], prm["bglu"]]
        out_spec = pl.BlockSpec((SUB, tc, W), lambda g, j: (g, cidx(j), 0))
        out_sds = jax.ShapeDtypeStruct((nb, seq, W), bf16)
        scratch.append(pltpu.VMEM((SUB, tc, W), f32))
    return pl.pallas_call(
        functools.partial(_s5_kernel, tc=tc, nch=nch, direction=direction, has_s0=s0 is not None),
        grid=(nbg, nch),
        in_specs=in_specs,
        out_specs=[out_spec, st_spec],
        out_shape=[out_sds, jax.ShapeDtypeStruct((nb, 2 * SSM_N), f32)],
        scratch_shapes=scratch,
        compiler_params=_cp(2),
        name="s5_dir%d" % direction,
    )(*args)


def _rope(x, cos, sin_signed):
    return x * cos + pltpu.roll(x, HEAD_DIM // 2, 1) * sin_signed


def _unit_rms(x, w):
    return x * lax.rsqrt(jnp.mean(x * x, axis=-1, keepdims=True) + EPS) * w


def _attn_kernel(*refs, seq, latent):
    q_ref, k_ref, v_ref, qw_ref, kw_ref, sink_ref = refs[:6]
    pos = 6
    if latent:
        cos_ref, sin_ref, ck_ref, cv_ref = refs[pos:pos + 4]
        pos += 4
    o_ref = refs[pos]
    pos += 1
    if not latent:
        kout_ref, vout_ref = refs[pos], refs[pos + 1]
        pos += 2
    kb_scr, vb_scr = refs[pos], refs[pos + 1]
    pos += 2
    if latent:
        ckb_scr, cvb_scr = refs[pos], refs[pos + 1]

    HD = HEAD_DIM
    qb = pl.program_id(1)

    @pl.when(qb == 0)
    def _():
        for hk in range(ATT_KV_HEADS):
            sl = slice(hk * HD, (hk + 1) * HD)
            kn = _unit_rms(k_ref[:, sl], kw_ref[...])
            if latent:
                kn = _rope(kn, cos_ref[...], sin_ref[...])
            else:
                kout_ref[:, sl] = kn
            kb_scr[:, sl] = kn.astype(bf16)
        v = v_ref[...]
        vb_scr[...] = v.astype(bf16)
        if latent:
            ckb_scr[...] = ck_ref[...].astype(bf16)
            cvb_scr[...] = cv_ref[...].astype(bf16)
        else:
            vout_ref[...] = v

    QB = q_ref.shape[0]
    q0 = pl.multiple_of(qb * QB, QB)
    if latent:
        win = QB + 2 * WINDOW
        start = pl.multiple_of(jnp.clip(q0 - WINDOW, 0, seq - win), WINDOW)
        qpos = q0 + (lax.broadcasted_iota(jnp.int32, (2 * QB, win), 0) & (QB - 1))
        kpos = start + lax.broadcasted_iota(jnp.int32, (2 * QB, win), 1)
        band = jnp.abs(qpos - kpos) <= WINDOW
        cos_q = cos_ref[pl.ds(q0, QB), :]
        sin_q = sin_ref[pl.ds(q0, QB), :]
    else:
        win = seq
        start = 0
    nt_dims = (((1,), (1,)), ((), ()))

    def lane_tiles(x):
        return [x[:, t * HD:(t + 1) * HD] for t in range(x.shape[1] // HD)]

    def fold(op, tiles):
        acc = tiles[0]
        for t in tiles[1:]:
            acc = op(acc, t)
        return acc

    row = lax.broadcasted_iota(jnp.int32, (2 * QB, 1), 0)
    for hk in range(ATT_KV_HEADS):
        sl = slice(hk * HD, (hk + 1) * HD)
        qs = []
        for g in range(2):
            hq = hk * 2 + g
            qn = _unit_rms(q_ref[:, hq * HD:(hq + 1) * HD], qw_ref[...])
            if latent:
                qn = _rope(qn, cos_q, sin_q)
            qs.append((qn * (HD ** -0.5)).astype(bf16))
        qg = jnp.concatenate(qs, axis=0)
        snk = jnp.where(row < QB, sink_ref[hk * 2], sink_ref[hk * 2 + 1])
        s1 = lax.dot_general(qg, kb_scr[pl.ds(start, win), sl], nt_dims, preferred_element_type=f32)
        if latent:
            s1 = jnp.where(band, s1, NEG_INF)
            s2 = lax.dot_general(qg, ckb_scr[:, sl], nt_dims, preferred_element_type=f32)
        s_tiles = lane_tiles(s1) + (lane_tiles(s2) if latent else [])
        m = jnp.maximum(jnp.max(fold(jnp.maximum, s_tiles), axis=-1, keepdims=True), snk)
        p1 = jnp.exp(s1 - m)
        if latent:
            p2 = jnp.exp(s2 - m)
        p_tiles = lane_tiles(p1) + (lane_tiles(p2) if latent else [])
        den = jnp.sum(fold(jnp.add, p_tiles), axis=-1, keepdims=True) + jnp.exp(snk - m)
        o = jnp.dot((p1 / den).astype(bf16), vb_scr[pl.ds(start, win), sl], preferred_element_type=f32)
        if latent:
            o += jnp.dot((p2 / den).astype(bf16), cvb_scr[:, sl], preferred_element_type=f32)
        for g in range(2):
            hq = hk * 2 + g
            o_ref[:, hq * HD:(hq + 1) * HD] = o[g * QB:(g + 1) * QB].astype(bf16)


def _attention(z, qw, kw, sink, rope, cache, grp):
    nb, seq = grp.nb, grp.seq
    latent = not grp.ctx
    QB = min(seq, ATT_QB_LAT if latent else ATT_QB_CTX)
    nqb = seq // QB
    kvw = ATT_KV_HEADS * HEAD_DIM
    in_specs = [
        pl.BlockSpec((QB, 512), lambda b, i: (b * nqb + i, AQ // 512)),
        pl.BlockSpec((seq, kvw), lambda b, i: (b, AK // kvw)),
        pl.BlockSpec((seq, kvw), lambda b, i: (b, AV // kvw)),
        pl.BlockSpec((1, HEAD_DIM), lambda b, i: (0, 0)),
        pl.BlockSpec((1, HEAD_DIM), lambda b, i: (0, 0)),
        pl.BlockSpec(memory_space=pltpu.SMEM),
    ]
    args = [z, z, z, qw, kw, sink]
    scratch = [pltpu.VMEM((seq, kvw), bf16), pltpu.VMEM((seq, kvw), bf16)]
    out_specs = [pl.BlockSpec((QB, 512), lambda b, i: (b * nqb + i, 0))]
    out_shape = [jax.ShapeDtypeStruct((nb * seq, ATT_Q_HEADS * HEAD_DIM), bf16)]
    if latent:
        in_specs += [pl.BlockSpec((seq, HEAD_DIM), lambda b, i: (0, 0)),
                     pl.BlockSpec((seq, HEAD_DIM), lambda b, i: (0, 0)),
                     pl.BlockSpec((None, PAST_LEN, kvw), lambda b, i: (b, 0, 0)),
                     pl.BlockSpec((None, PAST_LEN, kvw), lambda b, i: (b, 0, 0))]
        args += [rope[0], rope[1], cache[0], cache[1]]
        scratch += [pltpu.VMEM((PAST_LEN, kvw), bf16), pltpu.VMEM((PAST_LEN, kvw), bf16)]
    else:
        kv_out = pl.BlockSpec((seq, kvw), lambda b, i: (b, 0))
        out_specs += [kv_out, kv_out]
        out_shape += [jax.ShapeDtypeStruct((nb * seq, kvw), f32)] * 2
    return pl.pallas_call(
        functools.partial(_attn_kernel, seq=seq, latent=latent),
        grid=(nb, nqb),
        in_specs=in_specs,
        out_specs=out_specs,
        out_shape=out_shape,
        scratch_shapes=scratch,
        compiler_params=_cp(2),
        name="attention_lat" if latent else "attention_ctx",
    )(*args)


def _merge_kernel(h_ref, o0, o1, o2, o3, g0, g1, g2, g3, wbr_ref, *rest):
    out_ref = rest[len(rest) // 2]
    h = h_ref[...]
    acc = None
    for b, (o_ref, g_ref) in enumerate(((o0, g0), (o1, g1), (o2, g2), (o3, g3))):
        gate = _sigmoid(jnp.dot(h, g_ref[...], preferred_element_type=f32))
        t = gate * jnp.dot(o_ref[...], wbr_ref[b], preferred_element_type=f32)
        acc = t if acc is None else acc + t
    out_ref[...] = acc.astype(bf16)
    n_cast = len(rest) // 2
    for src, dst in zip(rest[:n_cast], rest[n_cast + 1:]):
        dst[...] = src[...].astype(bf16)


def _merge(h, branches, w_in, w_br, grp, cast_next=None):
    tn = 512
    g0 = ZW // tn
    gb = D_MODEL // tn
    n_i, n_j = grp.n // TM, D_MODEL // tn
    o_spec = pl.BlockSpec((TM, BRANCH_WIDTH), lambda i, j: (i, 0))

    def gate_spec(b):
        return pl.BlockSpec((D_MODEL, tn), lambda i, j: (0, g0 + b * gb + j))

    in_specs = [
        pl.BlockSpec((TM, D_MODEL), lambda i, j: (i, 0)),
        o_spec, o_spec, o_spec, o_spec,
        gate_spec(0), gate_spec(1), gate_spec(2), gate_spec(3),
        pl.BlockSpec((N_BRANCH, BRANCH_WIDTH, tn), lambda i, j: (0, 0, j)),
    ]
    args = [h, *branches, w_in, w_in, w_in, w_in, w_br]
    out_specs = [pl.BlockSpec((TM, tn), lambda i, j: (i, j))]
    out_shape = [jax.ShapeDtypeStruct((grp.n, D_MODEL), bf16)]
    if cast_next is not None:
        nxt, stacks = cast_next
        for w in stacks:
            rows, cols = w.shape[1:]
            slab = rows // (n_i * n_j)
            in_specs.append(pl.BlockSpec((None, slab, cols), lambda i, j: (nxt, i * n_j + j, 0)))
            out_specs.append(pl.BlockSpec((slab, cols), lambda i, j: (i * n_j + j, 0)))
            out_shape.append(jax.ShapeDtypeStruct((rows, cols), bf16))
            args.append(w)
    res = pl.pallas_call(
        _merge_kernel,
        grid=(n_i, n_j),
        in_specs=in_specs,
        out_specs=out_specs,
        out_shape=out_shape,
        compiler_params=_cp(2, vmem=BIG_VMEM_LIMIT if cast_next is not None else VMEM_LIMIT),
        name="merge",
    )(*args)
    return res[0], res[1:]


def _outproj_kernel(x_ref, mod_ref, m_ref, w_ref, o_ref):
    o_ref[...] = x_ref[...] + mod_ref[2:3, :] * jnp.dot(m_ref[...], w_ref[...], preferred_element_type=f32)


def _out_proj(x, mod, merged, w_out, layer, grp):
    tm = OUT_TM
    return pl.pallas_call(
        _outproj_kernel,
        grid=(grp.n // tm,),
        in_specs=[
            pl.BlockSpec((tm, D_MODEL), lambda i: (i, 0)),
            pl.BlockSpec((None, None, 6, D_MODEL), lambda i: (layer, _mod_row(grp, i, tm), 0, 0)),
            pl.BlockSpec((tm, D_MODEL), lambda i: (i, 0)),
            pl.BlockSpec((D_MODEL, D_MODEL), lambda i: (0, 0)),
        ],
        out_specs=pl.BlockSpec((tm, D_MODEL), lambda i: (i, 0)),
        out_shape=jax.ShapeDtypeStruct((grp.n, D_MODEL), f32),
        compiler_params=_cp(1),
        name="out_proj",
    )(x, mod, merged, w_out)


def _ffn_kernel(x_ref, mod_ref, nw_ref, w1_ref, w2_ref, o_ref, h_scr, r_scr, gs_scr):
    j = pl.program_id(1)

    @pl.when(j == 0)
    def _():
        _modnorm_rows(x_ref, nw_ref[...], mod_ref[3:4, :], mod_ref[4:5, :], h_scr, r_scr, gs_scr, copy_ref=o_ref)

    a = jnp.maximum(jnp.dot(h_scr[...], w1_ref[...], preferred_element_type=f32), 0.0)
    a2 = (a * a).astype(bf16)
    cw = 512
    for cidx in range(D_MODEL // cw):
        cs = slice(cidx * cw, (cidx + 1) * cw)
        o_ref[:, cs] += mod_ref[5:6, cs] * jnp.dot(a2, w2_ref[:, cs], preferred_element_type=f32)


def _ffn(x, mod, nw, w1, w2, layer, grp):
    tf = 1024
    nj = D_FF // tf
    return pl.pallas_call(
        _ffn_kernel,
        grid=(grp.n // TM, nj),
        in_specs=[
            pl.BlockSpec((TM, D_MODEL), lambda i, j: (i, 0)),
            pl.BlockSpec((None, None, 6, D_MODEL), lambda i, j: (layer, _mod_row(grp, i), 0, 0)),
            pl.BlockSpec((None, 1, D_MODEL), lambda i, j: (layer, 0, 0)),
            pl.BlockSpec((D_MODEL, tf), lambda i, j: (0, j)),
            pl.BlockSpec((tf, D_MODEL), lambda i, j: (j, 0)),
        ],
        out_specs=pl.BlockSpec((TM, D_MODEL), lambda i, j: (i, 0)),
        out_shape=jax.ShapeDtypeStruct((grp.n, D_MODEL), f32),
        scratch_shapes=[pltpu.VMEM((TM, D_MODEL), bf16), pltpu.VMEM((TM, 128), f32),
                        pltpu.VMEM((2, SUB, D_MODEL), f32)],
        compiler_params=_cp(2, vmem=BIG_VMEM_LIMIT),
        name="ffn",
    )(x, mod, nw, w1, w2)


def _rope_tables(seq):
    t = jnp.arange(seq)
    row = (t // GRID_W).astype(f32)
    col = (t % GRID_W).astype(f32)
    n_pairs = HEAD_DIM // 4
    freqs = ROPE_BASE ** (-jnp.arange(n_pairs, dtype=f32) / n_pairs)
    ang = jnp.concatenate([row[:, None] * freqs, col[:, None] * freqs], axis=-1)
    cos = jnp.cos(ang)
    sin = jnp.sin(ang)
    return jnp.concatenate([cos, cos], axis=-1), jnp.concatenate([-sin, sin], axis=-1)


def _ssm_params(a_re, a_im, log_dt, b_re, b_im, c_re, c_im):
    lead = a_re.shape[:-2]
    dt = jnp.exp(log_dt)[..., None]
    mag = jnp.exp(a_re * dt)
    ab_re = mag * jnp.cos(a_im * dt)
    ab_im = mag * jnp.sin(a_im * dt)
    den = a_re * a_re + a_im * a_im
    q_re = (((ab_re - 1.0) * a_re + ab_im * a_im) / den)[..., None]
    q_im = ((ab_im * a_re - (ab_re - 1.0) * a_im) / den)[..., None]
    bb_re = q_re * b_re - q_im * b_im
    bb_im = q_re * b_im + q_im * b_re
    a = jnp.stack([ab_re.reshape(lead + (SSM_N,)), ab_im.reshape(lead + (SSM_N,))], axis=-2)

    ncb = SSM_N // SSM_CB
    gpb = SSM_GROUPS // ncb
    eye = jnp.eye(gpb, dtype=f32)[:, None, :, None]

    def diag_blocks(w, rows, cols):
        w = jnp.swapaxes(w.reshape(lead + (ncb, gpb) + w.shape[-2:]), -1, -2)
        return (w[..., None, :] * eye).reshape(lead + (ncb, rows, cols))

    bd = jnp.stack([diag_blocks(w, SSM_CI, SSM_CB) for w in (bb_re, bb_im)], axis=-4).astype(bf16)
    cd = jnp.stack([diag_blocks(w, SSM_CB, SSM_CI) for w in (c_re, -c_im)], axis=-4).astype(bf16)
    return a, bd, cd


def _lru_params(conv_w, conv_b, wa, ba, wx, bx, lam):
    w = LRU_WIDTH
    return {"cw": conv_w, "cb": conv_b.reshape(1, w),
            "wa": (0.5 * wa).astype(bf16), "ba": (0.5 * ba).reshape(2, 1, w),
            "wx": (0.5 * wx).astype(bf16), "bx": (0.5 * bx).reshape(2, 1, w),
            "sp": (-0.5 * LRU_C * jax.nn.softplus(-lam)).reshape(2, 1, w)}


def _mixers(z, grp, layer, prm, states):
    nb, seq = grp.nb, grp.seq
    z3 = z.reshape(nb, seq, ZW)
    s_ret0, s_lru0, s_ssm0, cache = states

    o_ret, s_ret = _retention(z, prm["ret_lg"], prm["ret_gn"], s_ret0, grp)

    hf, xc, lru_f = _lru_dir(z3, None, None if s_lru0 is None else s_lru0[:, 0], prm["lru"], grp, 0)
    o_lru, lru_b = _lru_dir(z3, (hf, xc), None if s_lru0 is None else s_lru0[:, 1], prm["lru"], grp, 1)

    att = _attention(z, prm["att_qw"], prm["att_kw"], prm["att_sink"], prm["rope"], cache, grp)

    yf, ssm_f = _s5_dir(z3, None, None if s_ssm0 is None else s_ssm0[:, 0], prm["ssm"], grp, 0)
    o_ssm, ssm_b = _s5_dir(z3, yf, None if s_ssm0 is None else s_ssm0[:, 1], prm["ssm"], grp, 1)

    branches = [o_ret, o_lru.reshape(grp.n, LRU_WIDTH), att[0], o_ssm.reshape(grp.n, SSM_WIDTH)]
    new_state = None
    if grp.ctx:
        ssm_fin = jnp.stack([ssm_f, ssm_b], axis=1)
        new_state = (att[1].reshape(nb, seq, ATT_KV_HEADS, HEAD_DIM), att[2].reshape(nb, seq, ATT_KV_HEADS, HEAD_DIM),
                     s_ret, jnp.stack([lru_f, lru_b], axis=1),
                     ssm_fin[:, :, :SSM_N].reshape(nb, 2, SSM_GROUPS, SSM_STATE),
                     ssm_fin[:, :, SSM_N:].reshape(nb, 2, SSM_GROUPS, SSM_STATE))
    return branches, new_state


def kernel(x_prompt, x_sample, cache_attn_k, cache_attn_v, state_ret, state_lru, state_ssm_re, state_ssm_im, c, c_ctx, w_mod, b_mod, norm1, w_in, ret_decay_logit, ret_gn, lru_conv_w, lru_conv_b, lru_wa, lru_ba, lru_wx, lru_bx, lru_lambda, att_q_norm, att_k_norm, att_sink, ssm_a_re, ssm_a_im, ssm_log_dt, ssm_b_re, ssm_b_im, ssm_c_re, ssm_c_im, ssm_d, ssm_w_glu, ssm_b_glu, w_br, w_out, norm2, w_ff1, w_ff2):
    xs = {PROMPT: x_prompt.reshape(PROMPT.n, D_MODEL), SAMPLE: x_sample.reshape(SAMPLE.n, D_MODEL)}
    cond = jnp.concatenate([c, c_ctx[None, :], jnp.zeros((COND_ROWS - SAMPLE.nb - 1, D_MODEL), f32)], axis=0)
    mod = _modulation(cond, w_mod, b_mod).reshape(DEPTH, COND_ROWS, 6, D_MODEL)

    w_stacks = (w_in, w_br.reshape(DEPTH, N_BRANCH * BRANCH_WIDTH, D_MODEL), w_out, w_ff1, w_ff2)
    w_layer = [w_stacks[0][0].astype(bf16), w_stacks[1][0].astype(bf16), None, None, None]
    norm1_r = norm1.reshape(DEPTH, 1, D_MODEL)
    norm2_r = norm2.reshape(DEPTH, 1, D_MODEL)
    rope = _rope_tables(SAMPLE.seq)
    log_gamma = -jax.nn.softplus(-ret_decay_logit)
    kvw = ATT_KV_HEADS * HEAD_DIM

    ssm_a_all, ssm_bd_all, ssm_cd_all = _ssm_params(ssm_a_re, ssm_a_im, ssm_log_dt, ssm_b_re, ssm_b_im,
                                                    ssm_c_re, ssm_c_im)
    new_states = []
    for l in range(DEPTH):
        ssm_a, ssm_bd, ssm_cd = ssm_a_all[l], ssm_bd_all[l], ssm_cd_all[l]
        prm = {
            "ret_lg": jnp.broadcast_to(log_gamma[l].T[:, :, None], (RET_HEADS, 2, 128)),
            "ret_gn": ret_gn[l].reshape(1, RET_HEADS * 128),
            "lru": _lru_params(lru_conv_w[l], lru_conv_b[l], lru_wa[l], lru_ba[l], lru_wx[l], lru_bx[l],
                               lru_lambda[l]),
            "att_qw": att_q_norm[l].reshape(1, HEAD_DIM), "att_kw": att_k_norm[l].reshape(1, HEAD_DIM),
            "att_sink": att_sink[l], "rope": rope,
            "ssm": {"a": ssm_a, "bd": ssm_bd, "cd": ssm_cd, "d": ssm_d[l].reshape(1, SSM_WIDTH),
                    "wglu": ssm_w_glu[l].astype(bf16), "bglu": ssm_b_glu[l].reshape(1, SSM_WIDTH)},
        }
        states = {
            PROMPT: (None, None, None, None),
            SAMPLE: (state_ret[:, l], state_lru[:, l],
                     jnp.concatenate([state_ssm_re[:, l].reshape(SAMPLE.nb, 2, SSM_N),
                                      state_ssm_im[:, l].reshape(SAMPLE.nb, 2, SSM_N)], axis=-1),
                     (cache_attn_k[:, l].reshape(SAMPLE.nb, PAST_LEN, kvw),
                      cache_attn_v[:, l].reshape(SAMPLE.nb, PAST_LEN, kvw))),
        }
        w_in_l, w_br_l, w_out_l, w_ff1_l, w_ff2_l = w_layer
        w_br_l = w_br_l.reshape(N_BRANCH, BRANCH_WIDTH, D_MODEL)
        for grp in (PROMPT, SAMPLE):
            x = xs[grp]
            z, h = _proj_in(x, mod, norm1_r, w_in_l, l, grp)
            branches, st = _mixers(z, grp, l, prm, states[grp])
            if grp.ctx:
                new_states.append(st)
            cast = None
            if grp is SAMPLE and l + 1 < DEPTH:
                cast = (l + 1, w_stacks)
            elif grp is PROMPT and l == 0:
                cast = (0, w_stacks[2:])
            merged, w_cast = _merge(h, branches, w_in_l, w_br_l, grp, cast)
            if cast is not None and cast[0] == l:
                w_out_l, w_ff1_l, w_ff2_l = w_cast
            elif cast is not None:
                w_layer = list(w_cast)
            x = _out_proj(x, mod, merged, w_out_l, l, grp)
            xs[grp] = _ffn(x, mod, norm2_r, w_ff1_l, w_ff2_l, l, grp)

    y_prompt = xs[PROMPT].reshape(PROMPT.nb, PROMPT.seq, D_MODEL)
    y_sample = xs[SAMPLE].reshape(SAMPLE.nb, SAMPLE.seq, D_MODEL)
    return (y_prompt, y_sample) + tuple(jnp.stack([st[i] for st in new_states], axis=1) for i in range(6))
```

```python
import functools
from typing import NamedTuple

import jax
import jax.numpy as jnp
from jax import lax
from jax.experimental import pallas as pl
from jax.experimental.pallas import tpu as pltpu

f32 = jnp.float32
bf16 = jnp.bfloat16

D_MODEL = 2048
DEPTH = 4
PAST_LEN = 256
GRID_W = 64
CHUNK = 128
ATT_QB_LAT = 256
ATT_SUB_LAT = 128
ATT_QB_CTX = 256
EPS = 1e-6
NEG_INF = -1e30
RET_HEADS = 4
RET_DK = 128
LRU_WIDTH = 512
LRU_BLOCKS = 4
LRU_C = 8.0
CONV_W = 4
ATT_Q_HEADS = 4
ATT_KV_HEADS = 2
HEAD_DIM = 128
WINDOW = 128
ROPE_BASE = 10000.0
SSM_WIDTH = 512
SSM_GROUP = 16
SSM_GROUPS = 32
SSM_STATE = 64
SSM_N = SSM_GROUPS * SSM_STATE
SSM_CB = 512
SSM_CI = SSM_CB // SSM_STATE * SSM_GROUP
N_BRANCH = 4
BRANCH_WIDTH = 512
D_FF = 4 * D_MODEL
RQ, RK, RV, RG, LX, LG, AQ, AK, AV, SU, ZW = 0, 512, 1024, 1536, 2048, 2560, 3072, 3584, 3840, 4096, 4608
SUB = 8
COND_ROWS = 16
VMEM_LIMIT = 56 * 1024 * 1024
BIG_VMEM_LIMIT = 62 * 1024 * 1024

TM = 1024
OUT_TM = 1024
LRU_TC = 128
SSM_TC = 128
ROW_BLK = 16


class Group(NamedTuple):
    nb: int
    seq: int
    ctx: bool

    @property
    def n(self):
        return self.nb * self.seq


PROMPT = Group(32, 256, True)
SAMPLE = Group(8, 2048, False)
CTX_ROW = SAMPLE.nb


def _cp(n_axes, vmem=VMEM_LIMIT):
    return pltpu.CompilerParams(dimension_semantics=("arbitrary",) * n_axes, vmem_limit_bytes=vmem)


def _sigmoid(x):
    return 0.5 * jnp.tanh(0.5 * x) + 0.5


def _mod_row(grp, i, tm=None):
    return CTX_ROW if grp.ctx else i // (grp.seq // (tm or TM))


def _modnorm_rows(x_ref, nw, shift, scale, h_ref, r_scr, gs_scr, copy_ref=None):
    tm, d = x_ref.shape
    lane_tiles = d // 128

    def ssq_body(r, carry):
        rows = pl.ds(pl.multiple_of(r * SUB, SUB), SUB)
        acc = None
        for t in range(lane_tiles):
            v = x_ref[rows, t * 128:(t + 1) * 128]
            acc = v * v if acc is None else acc + v * v
        r_scr[rows, :] = acc
        return carry

    lax.fori_loop(0, tm // SUB, ssq_body, 0, unroll=4)
    ssq = jnp.sum(r_scr[...], axis=-1, keepdims=True)
    r_scr[...] = jnp.broadcast_to(lax.rsqrt(ssq * (1.0 / d) + EPS), (tm, 128))
    gs_scr[0] = jnp.broadcast_to(nw * (1.0 + scale), (SUB, d))
    gs_scr[1] = jnp.broadcast_to(shift, (SUB, d))

    def out_body(r, carry):
        r0 = pl.multiple_of(r * ROW_BLK, ROW_BLK)
        halves = [pl.ds(pl.multiple_of(r0 + k * SUB, SUB), SUB) for k in range(ROW_BLK // SUB)]
        invs = [r_scr[rows, :] for rows in halves]
        for t in range(lane_tiles):
            cols = slice(t * 128, (t + 1) * 128)
            gain = gs_scr[0, :, cols]
            shf = gs_scr[1, :, cols]
            xs = [x_ref[rows, cols] for rows in halves]
            if copy_ref is not None:
                for rows, xv in zip(halves, xs):
                    copy_ref[rows, cols] = xv
            parts = [xv * inv * gain + shf for xv, inv in zip(xs, invs)]
            h_ref[pl.ds(r0, ROW_BLK), cols] = jnp.concatenate(parts, axis=0).astype(bf16)
        return carry

    lax.fori_loop(0, tm // ROW_BLK, out_body, 0, unroll=2)


def _mod_kernel(c_ref, w_ref, b_ref, o_ref):
    c = c_ref[...]
    s = (c * _sigmoid(c)).astype(bf16)
    o_ref[...] = jnp.dot(s, w_ref[...].astype(bf16), preferred_element_type=f32) + b_ref[...]


def _modulation(cond, w_mod, b_mod):
    tn = 1024
    n6 = 6 * D_MODEL
    return pl.pallas_call(
        _mod_kernel,
        grid=(DEPTH, n6 // tn),
        in_specs=[
            pl.BlockSpec((COND_ROWS, D_MODEL), lambda l, j: (0, 0)),
            pl.BlockSpec((None, D_MODEL, tn), lambda l, j: (l, 0, j)),
            pl.BlockSpec((None, 1, tn), lambda l, j: (l, 0, j)),
        ],
        out_specs=pl.BlockSpec((None, COND_ROWS, tn), lambda l, j: (l, 0, j)),
        out_shape=jax.ShapeDtypeStruct((DEPTH, COND_ROWS, n6), f32),
        compiler_params=_cp(2),
        name="modulation",
    )(cond, w_mod, b_mod.reshape(DEPTH, 1, n6))


def _proj_kernel(x_ref, mod_ref, nw_ref, w_ref, o_ref, h_ref, r_scr, gs_scr):
    @pl.when(pl.program_id(1) == 0)
    def _():
        _modnorm_rows(x_ref, nw_ref[...], mod_ref[0:1, :], mod_ref[1:2, :], h_ref, r_scr, gs_scr)

    o_ref[...] = jnp.dot(h_ref[...], w_ref[...], preferred_element_type=f32)


def _proj_in(x, mod, nw, w_in, layer, grp):
    tn = 1536
    return pl.pallas_call(
        _proj_kernel,
        grid=(grp.n // TM, ZW // tn),
        in_specs=[
            pl.BlockSpec((TM, D_MODEL), lambda i, j: (i, 0)),
            pl.BlockSpec((None, None, 6, D_MODEL), lambda i, j: (layer, _mod_row(grp, i), 0, 0)),
            pl.BlockSpec((None, 1, D_MODEL), lambda i, j: (layer, 0, 0)),
            pl.BlockSpec((D_MODEL, tn), lambda i, j: (0, j)),
        ],
        out_specs=[pl.BlockSpec((TM, tn), lambda i, j: (i, j)),
                   pl.BlockSpec((TM, D_MODEL), lambda i, j: (i, 0))],
        out_shape=[jax.ShapeDtypeStruct((grp.n, ZW), f32), jax.ShapeDtypeStruct((grp.n, D_MODEL), bf16)],
        scratch_shapes=[pltpu.VMEM((TM, 128), f32), pltpu.VMEM((2, SUB, D_MODEL), f32)],
        compiler_params=_cp(2),
        name="proj_in",
    )(x, mod, nw, w_in)


def _ret_kernel(*refs, nc, has_s0, emit_state):
    q_ref, k_ref, v_ref, g_ref, lg_ref, gn_ref = refs[:6]
    pos = 6
    s0_ref = None
    if has_s0:
        s0_ref = refs[pos]
        pos += 1
    o_ref = refs[pos]
    pos += 1
    sout_ref = None
    if emit_state:
        sout_ref = refs[pos]
        pos += 1
    w_scr, kv_scr = refs[pos], refs[pos + 1]

    C = CHUNK
    H = RET_HEADS
    DEC, W_IN_F, W_IN_B, W_END_F, W_END_B = range(5)
    tn_dims = (((0,), (0,)), ((), ()))
    nt_dims = (((1,), (1,)), ((), ()))

    @pl.when(pl.program_id(0) == 0)
    def _():
        ii = lax.broadcasted_iota(jnp.int32, (C, C), 0).astype(f32)
        jj = lax.broadcasted_iota(jnp.int32, (C, C), 1).astype(f32)
        rel = ii - jj
        for h in range(H):
            lgf = lg_ref[h, 0:1, :]
            lgb = lg_ref[h, 1:2, :]
            w_scr[h, DEC] = (jnp.where(rel >= 0, jnp.exp(jnp.maximum(rel, 0.0) * lgf), 0.0)
                             + jnp.where(rel <= 0, jnp.exp(jnp.maximum(-rel, 0.0) * lgb), 0.0))
            w_scr[h, W_IN_F] = jnp.exp((ii + 1.0) * lgf)
            w_scr[h, W_IN_B] = jnp.exp((C - ii) * lgb)
            w_scr[h, W_END_F] = jnp.exp((C - 1.0 - ii) * lgf)
            w_scr[h, W_END_B] = jnp.exp(ii * lgb)

    def kv_body(n, carry):
        r = pl.multiple_of(n * C, C)
        for h in range(H):
            cols = slice(h * C, (h + 1) * C)
            k = k_ref[pl.ds(r, C), cols]
            vb = v_ref[pl.ds(r, C), cols].astype(bf16)
            kw = jnp.concatenate([k * w_scr[h, W_END_F], k * w_scr[h, W_END_B]], axis=1).astype(bf16)
            kv = lax.dot_general(kw, vb, tn_dims, preferred_element_type=f32)
            kv_scr[0, n * H + h] = kv[:C]
            kv_scr[1, n * H + h] = kv[C:]
        return carry

    lax.fori_loop(0, nc, kv_body, 0, unroll=min(nc, 4))

    for h in range(H):
        cd_f = jnp.exp(C * lg_ref[h, 0:1, :])
        cd_b = jnp.exp(C * lg_ref[h, 1:2, :])

        def fwd_body(n, s, h=h, cd_f=cd_f):
            kv = kv_scr[0, n * H + h]
            kv_scr[0, n * H + h] = s
            return cd_f * s + kv

        def bwd_body(i, s, h=h, cd_b=cd_b):
            n = nc - 1 - i
            kv = kv_scr[1, n * H + h]
            kv_scr[1, n * H + h] = s
            return cd_b * s + kv

        s_f = lax.fori_loop(0, nc, fwd_body, s0_ref[0, h] if has_s0 else jnp.zeros((C, C), f32))
        s_b = lax.fori_loop(0, nc, bwd_body, s0_ref[1, h] if has_s0 else jnp.zeros((C, C), f32))
        if emit_state:
            sout_ref[0, h] = s_f
            sout_ref[1, h] = s_b

    def out_body(n, carry):
        r = pl.multiple_of(n * C, C)
        for h in range(H):
            cols = slice(h * C, (h + 1) * C)
            q = q_ref[pl.ds(r, C), cols] * (RET_DK ** -0.5)
            kb = k_ref[pl.ds(r, C), cols].astype(bf16)
            vb = v_ref[pl.ds(r, C), cols].astype(bf16)
            sc = lax.dot_general(q.astype(bf16), kb, nt_dims, preferred_element_type=f32) * w_scr[h, DEC]
            lhs = jnp.concatenate([sc, q * w_scr[h, W_IN_F], q * w_scr[h, W_IN_B]], axis=1).astype(bf16)
            rhs = jnp.concatenate([vb, kv_scr[0, n * H + h].astype(bf16), kv_scr[1, n * H + h].astype(bf16)], axis=0)
            o = jnp.dot(lhs, rhs, preferred_element_type=f32)
            mu = jnp.mean(o, axis=-1, keepdims=True)
            d = o - mu
            var = jnp.mean(d * d, axis=-1, keepdims=True)
            on = d * lax.rsqrt(var + EPS) * gn_ref[:, cols]
            g = g_ref[pl.ds(r, C), cols]
            o_ref[pl.ds(r, C), cols] = (g * _sigmoid(g) * on).astype(bf16)
        return carry

    lax.fori_loop(0, nc, out_body, 0, unroll=True)


def _retention(z, lg, gn, s0, grp):
    seq = grp.seq
    W = RET_HEADS * 128
    in_specs = [
        pl.BlockSpec((seq, W), lambda b: (b, RQ // W)),
        pl.BlockSpec((seq, W), lambda b: (b, RK // W)),
        pl.BlockSpec((seq, W), lambda b: (b, RV // W)),
        pl.BlockSpec((seq, W), lambda b: (b, RG // W)),
        pl.BlockSpec((RET_HEADS, 2, 128), lambda b: (0, 0, 0)),
        pl.BlockSpec((1, W), lambda b: (0, 0)),
    ]
    args = [z, z, z, z, lg, gn]
    st_spec = pl.BlockSpec((None, 2, RET_HEADS, 128, 128), lambda b: (b, 0, 0, 0, 0))
    if s0 is not None:
        in_specs.append(st_spec)
        args.append(s0)
    out_specs = [pl.BlockSpec((seq, W), lambda b: (b, 0))]
    out_shape = [jax.ShapeDtypeStruct((grp.n, W), bf16)]
    if grp.ctx:
        out_specs.append(st_spec)
        out_shape.append(jax.ShapeDtypeStruct((grp.nb, 2, RET_HEADS, 128, 128), f32))
    nc = seq // CHUNK
    res = pl.pallas_call(
        functools.partial(_ret_kernel, nc=nc, has_s0=s0 is not None, emit_state=grp.ctx),
        grid=(grp.nb,),
        in_specs=in_specs,
        out_specs=out_specs,
        out_shape=out_shape,
        scratch_shapes=[pltpu.VMEM((RET_HEADS, 5, 128, 128), f32),
                        pltpu.VMEM((2, nc * RET_HEADS, 128, 128), f32)],
        compiler_params=_cp(1),
        name="retention",
    )(*args)
    return res if grp.ctx else (res[0], None)


def _gather_tm(src_ref, nt, dst_ref, row0, unroll=2):
    w = src_ref.shape[-1]

    def body(i, carry):
        v = src_ref[:, pl.ds(pl.multiple_of(i * SUB, SUB), SUB), :]
        rows = pl.ds(pl.multiple_of(row0 + i * SUB * SUB, SUB), SUB * SUB)
        dst_ref[rows, :] = jnp.swapaxes(v, 0, 1).reshape(SUB * SUB, w)
        return carry

    lax.fori_loop(0, nt // SUB, body, 0, unroll=unroll)


def _scatter_bm(src_ref, nt, dst_ref, unroll=2):
    w = src_ref.shape[-1]

    def body(i, carry):
        v = src_ref[pl.ds(pl.multiple_of(i * SUB * SUB, SUB * SUB), SUB * SUB), :].reshape(SUB, SUB, w)
        dst_ref[:, pl.ds(pl.multiple_of(i * SUB, SUB), SUB), :] = jnp.swapaxes(v, 0, 1)
        return carry

    lax.fori_loop(0, nt // SUB, body, 0, unroll=unroll)


def _lru_kernel(*refs, tc, nch, direction, has_s0):
    if direction == 0:
        xprev_ref, xcur_ref, xnext_ref, cw_ref, cb_ref = refs[:5]
        pos = 5
    else:
        xc_ref = refs[0]
        pos = 1
    wa_ref, ba_ref, wx_ref, bx_ref, sp_ref = refs[pos:pos + 5]
    pos += 5
    s0_ref = None
    if has_s0:
        s0_ref = refs[pos]
        pos += 1
    if direction == 0:
        out_ref, xc_out_ref, fin_ref = refs[pos:pos + 3]
        xe_scr, a_scr, b_scr, h_scr = refs[pos + 3:pos + 7]
    else:
        gate_ref, hf_ref, out_ref, fin_ref = refs[pos:pos + 4]
        a_scr, b_scr, h_scr, g_scr, y_scr = refs[pos + 4:pos + 9]

    R = tc * SUB
    W = LRU_WIDTH
    j = pl.program_id(1)
    c = j if direction == 0 else nch - 1 - j

    @pl.when(j == 0)
    def _():
        h_scr[...] = s0_ref[...] if has_s0 else jnp.zeros((SUB, W), f32)

    if direction == 0:
        for t in range(2):
            v = xprev_ref[:, SUB - 2 + t:SUB - 1 + t, :].reshape(SUB, W)
            xe_scr[t * SUB:(t + 1) * SUB, :] = jnp.where(c > 0, v, 0.0)
        _gather_tm(xcur_ref, tc, xe_scr, 2 * SUB)
        xe_scr[2 * SUB + R:3 * SUB + R, :] = jnp.where(c < nch - 1, xnext_ref[:, 0:1, :].reshape(SUB, W), 0.0)
        xc = cb_ref[...] + xe_scr[0:R, :] * cw_ref[0:1, :]
        for t in range(1, CONV_W):
            xc = xc + xe_scr[t * SUB:t * SUB + R, :] * cw_ref[t:t + 1, :]
        xc_out_ref[...] = xc
    else:
        xc = xc_ref[...]
        _gather_tm(gate_ref, tc, g_scr, 0)

    bd = W // LRU_BLOCKS
    for n in range(LRU_BLOCKS):
        sl = slice(n * bd, (n + 1) * bd)
        xs = xc[:, sl]
        xb = xs.astype(bf16)
        tr = jnp.tanh(jnp.dot(xb, wa_ref[n], preferred_element_type=f32) + ba_ref[:, sl])
        ti = jnp.tanh(jnp.dot(xb, wx_ref[n], preferred_element_type=f32) + bx_ref[:, sl])
        log_a = sp_ref[:, sl] * (tr + 1.0)
        a = jnp.exp(log_a)
        a_scr[:, sl] = a
        t = -jnp.tanh(log_a) * (a * a + 1.0)
        root = jnp.where(t > 0.0, t * lax.rsqrt(t), 0.0)
        b_scr[:, sl] = (0.5 * root) * ((ti + 1.0) * xs)

    def step(s, h):
        t = s if direction == 0 else tc - 1 - s
        r0 = pl.multiple_of(t * SUB, SUB)
        h = a_scr[pl.ds(r0, SUB), :] * h + b_scr[pl.ds(r0, SUB), :]
        b_scr[pl.ds(r0, SUB), :] = h
        return h

    h = lax.fori_loop(0, tc, step, h_scr[...], unroll=8)
    h_scr[...] = h
    fin_ref[...] = h
    if direction == 0:
        out_ref[...] = b_scr[...]
    else:
        b_scr[...] = jax.nn.gelu(g_scr[...]) * (hf_ref[...] + b_scr[...])
        _scatter_bm(b_scr, tc, y_scr)
        out_ref[...] = y_scr[...].astype(bf16)


def _lru_dir(z3, prev, s0, prm, grp, direction):
    nb, seq = grp.nb, grp.seq
    nbg = nb // SUB
    tc = min(LRU_TC, seq)
    nch = seq // tc
    R = tc * SUB
    W = LRU_WIDTH

    def cidx(j):
        return j if direction == 0 else nch - 1 - j

    tm_spec = pl.BlockSpec((None, R, W), lambda g, j: (g, cidx(j), 0))
    tm_sds = jax.ShapeDtypeStruct((nbg, seq * SUB, W), f32)
    st_spec = pl.BlockSpec((SUB, W), lambda g, j: (g, 0))
    st_sds = jax.ShapeDtypeStruct((nb, W), f32)
    row_spec = pl.BlockSpec((1, W), lambda g, j: (0, 0))
    gate_w_spec = pl.BlockSpec((LRU_BLOCKS, 128, 128), lambda g, j: (0, 0, 0))
    if direction == 0:
        in_specs = [
            pl.BlockSpec((SUB, SUB, W), lambda g, j: (g, jnp.maximum(cidx(j) * (tc // SUB) - 1, 0), LX // W)),
            pl.BlockSpec((SUB, tc, W), lambda g, j: (g, cidx(j), LX // W)),
            pl.BlockSpec((SUB, SUB, W),
                         lambda g, j: (g, jnp.minimum((cidx(j) + 1) * (tc // SUB), seq // SUB - 1), LX // W)),
            pl.BlockSpec((CONV_W, W), lambda g, j: (0, 0)),
            row_spec,
        ]
        args = [z3, z3, z3, prm["cw"], prm["cb"]]
    else:
        hf_tm, xc_tm = prev
        in_specs = [tm_spec]
        args = [xc_tm]
    in_specs += [gate_w_spec, row_spec, gate_w_spec, row_spec, row_spec]
    args += [prm["wa"][direction], prm["ba"][direction], prm["wx"][direction], prm["bx"][direction],
             prm["sp"][direction]]
    if s0 is not None:
        in_specs.append(st_spec)
        args.append(s0)
    chunk_scr = pltpu.VMEM((R, W), f32)
    if direction == 0:
        out_specs = [tm_spec, tm_spec, st_spec]
        out_shape = [tm_sds, tm_sds, st_sds]
        scratch = [pltpu.VMEM((R + 3 * SUB, W), f32), chunk_scr, chunk_scr, pltpu.VMEM((SUB, W), f32)]
    else:
        in_specs += [pl.BlockSpec((SUB, tc, W), lambda g, j: (g, cidx(j), LG // W)), tm_spec]
        args += [z3, hf_tm]
        out_specs = [pl.BlockSpec((SUB, tc, W), lambda g, j: (g, cidx(j), 0)), st_spec]
        out_shape = [jax.ShapeDtypeStruct((nb, seq, W), bf16), st_sds]
        scratch = [chunk_scr, chunk_scr, pltpu.VMEM((SUB, W), f32), chunk_scr, pltpu.VMEM((SUB, tc, W), f32)]
    return pl.pallas_call(
        functools.partial(_lru_kernel, tc=tc, nch=nch, direction=direction, has_s0=s0 is not None),
        grid=(nbg, nch),
        in_specs=in_specs,
        out_specs=out_specs,
        out_shape=out_shape,
        scratch_shapes=scratch,
        compiler_params=_cp(2),
        name="rglru_dir%d" % direction,
    )(*args)


def _s5_kernel(*refs, tc, nch, direction, has_s0):
    u_ref, bd_ref, a_ref, cd_ref = refs[:4]
    pos = 4
    s0_ref = None
    if has_s0:
        s0_ref = refs[pos]
        pos += 1
    if direction == 1:
        yf_ref, dv_ref, wg_ref, bg_ref = refs[pos:pos + 4]
        pos += 4
    out_ref, fin_ref = refs[pos], refs[pos + 1]
    u_scr, ub_scr, hs_scr, h_scr = refs[pos + 2: pos + 6]
    if direction == 1:
        y_scr = refs[pos + 6]

    j = pl.program_id(1)

    @pl.when(j == 0)
    def _():
        h_scr[...] = s0_ref[...] if has_s0 else jnp.zeros((SUB, 2 * SSM_N), f32)

    _gather_tm(u_ref, tc, u_scr, 0, unroll=True)
    ub_scr[...] = u_scr[...].astype(bf16)

    cbw = SSM_CB
    for cb in range(SSM_N // cbw):
        cre = slice(cb * cbw, (cb + 1) * cbw)
        cim = slice(SSM_N + cb * cbw, SSM_N + (cb + 1) * cbw)
        cin = slice(cb * SSM_CI, (cb + 1) * SSM_CI)
        hs_scr[:, cre] = jnp.dot(ub_scr[:, cin], bd_ref[0, cb], preferred_element_type=f32)
        hs_scr[:, cim] = jnp.dot(ub_scr[:, cin], bd_ref[1, cb], preferred_element_type=f32)
        a_re = jnp.broadcast_to(a_ref[0:1, cre], (SUB, cbw))
        a_im = jnp.broadcast_to(a_ref[1:2, cre], (SUB, cbw))

        def step(s, carry):
            hr, hi = carry
            t = s if direction == 0 else tc - 1 - s
            r0 = pl.multiple_of(t * SUB, SUB)
            nr = a_re * hr - a_im * hi + hs_scr[pl.ds(r0, SUB), cre]
            ni = a_re * hi + a_im * hr + hs_scr[pl.ds(r0, SUB), cim]
            hs_scr[pl.ds(r0, SUB), cre] = nr
            hs_scr[pl.ds(r0, SUB), cim] = ni
            return nr, ni

        hr, hi = lax.fori_loop(0, tc, step, (h_scr[:, cre], h_scr[:, cim]), unroll=True)
        h_scr[:, cre] = hr
        h_scr[:, cim] = hi

    fin_ref[...] = h_scr[...]
    y_parts = []
    for cb in range(SSM_N // cbw):
        cre = slice(cb * cbw, (cb + 1) * cbw)
        cim = slice(SSM_N + cb * cbw, SSM_N + (cb + 1) * cbw)
        y_parts.append(jnp.dot(hs_scr[:, cre].astype(bf16), cd_ref[0, cb], preferred_element_type=f32)
                       + jnp.dot(hs_scr[:, cim].astype(bf16), cd_ref[1, cb], preferred_element_type=f32))
    y = jnp.concatenate(y_parts, axis=1)
    if direction == 0:
        out_ref[...] = y
    else:
        yy = jax.nn.gelu(yf_ref[...] + y + dv_ref[...] * u_scr[...])
        gl = jnp.dot(yy.astype(bf16), wg_ref[...], preferred_element_type=f32) + bg_ref[...]
        u_scr[...] = yy * _sigmoid(gl)
        _scatter_bm(u_scr, tc, y_scr, unroll=True)
        out_ref[...] = y_scr[...].astype(bf16)


def _s5_dir(z3, yf_tm, s0, prm, grp, direction):
    nb, seq = grp.nb, grp.seq
    nbg = nb // SUB
    tc = min(SSM_TC, seq)
    nch = seq // tc
    R = tc * SUB
    W = SSM_WIDTH
    ncb = SSM_N // SSM_CB

    def cidx(j):
        return j if direction == 0 else nch - 1 - j

    tm_spec = pl.BlockSpec((None, R, W), lambda g, j: (g, cidx(j), 0))
    st_spec = pl.BlockSpec((SUB, 2 * SSM_N), lambda g, j: (g, 0))
    in_specs = [
        pl.BlockSpec((SUB, tc, W), lambda g, j: (g, cidx(j), SU // W)),
        pl.BlockSpec((2, ncb, SSM_CI, SSM_CB), lambda g, j: (0, 0, 0, 0)),
        pl.BlockSpec((2, SSM_N), lambda g, j: (0, 0)),
        pl.BlockSpec((2, ncb, SSM_CB, SSM_CI), lambda g, j: (0, 0, 0, 0)),
    ]
    args = [z3, prm["bd"][direction], prm["a"][direction], prm["cd"][direction]]
    if s0 is not None:
        in_specs.append(st_spec)
        args.append(s0)
    scratch = [pltpu.VMEM((R, W), f32), pltpu.VMEM((R, W), bf16), pltpu.VMEM((R, 2 * SSM_N), f32),
               pltpu.VMEM((SUB, 2 * SSM_N), f32)]
    if direction == 0:
        out_spec = tm_spec
        out_sds = jax.ShapeDtypeStruct((nbg, seq * SUB, W), f32)
    else:
        in_specs += [tm_spec, pl.BlockSpec((1, W), lambda g, j: (0, 0)),
                     pl.BlockSpec((W, W), lambda g, j: (0, 0)), pl.BlockSpec((1, W), lambda g, j: (0, 0))]
        args += [yf_tm, prm["d"], prm["wglu"], prm["bglu"]]
        out_spec = pl.BlockSpec((SUB, tc, W), lambda g, j: (g, cidx(j), 0))
        out_sds = jax.ShapeDtypeStruct((nb, seq, W), bf16)
        scratch.append(pltpu.VMEM((SUB, tc, W), f32))
    return pl.pallas_call(
        functools.partial(_s5_kernel, tc=tc, nch=nch, direction=direction, has_s0=s0 is not None),
        grid=(nbg, nch),
        in_specs=in_specs,
        out_specs=[out_spec, st_spec],
        out_shape=[out_sds, jax.ShapeDtypeStruct((nb, 2 * SSM_N), f32)],
        scratch_shapes=scratch,
        compiler_params=_cp(2),
        name="s5_dir%d" % direction,
    )(*args)


def _rope(x, cos, sin_signed):
    return x * cos + pltpu.roll(x, HEAD_DIM // 2, 1) * sin_signed


def _unit_rms(x, w):
    return x * lax.rsqrt(jnp.mean(x * x, axis=-1, keepdims=True) + EPS) * w


def _attn_kernel(*refs, seq, latent, sub_rows):
    q_ref, k_ref, v_ref, qw_ref, kw_ref, sink_ref = refs[:6]
    pos = 6
    if latent:
        cos_ref, sin_ref, ck_ref, cv_ref = refs[pos:pos + 4]
        pos += 4
    o_ref = refs[pos]
    pos += 1
    if not latent:
        kout_ref, vout_ref = refs[pos], refs[pos + 1]
        pos += 2
    kb_scr, vb_scr = refs[pos], refs[pos + 1]
    pos += 2
    if latent:
        ckb_scr, cvb_scr = refs[pos], refs[pos + 1]

    HD = HEAD_DIM
    qb = pl.program_id(1)

    @pl.when(qb == 0)
    def _():
        for hk in range(ATT_KV_HEADS):
            sl = slice(hk * HD, (hk + 1) * HD)
            kn = _unit_rms(k_ref[:, sl], kw_ref[...])
            if latent:
                kn = _rope(kn, cos_ref[...], sin_ref[...])
            else:
                kout_ref[:, sl] = kn
            kb_scr[:, sl] = kn.astype(bf16)
        v = v_ref[...]
        vb_scr[...] = v.astype(bf16)
        if latent:
            ckb_scr[...] = ck_ref[...].astype(bf16)
            cvb_scr[...] = cv_ref[...].astype(bf16)
        else:
            vout_ref[...] = v

    nt_dims = (((1,), (1,)), ((), ()))

    def lane_tiles(x):
        return [x[:, t * HD:(t + 1) * HD] for t in range(x.shape[1] // HD)]

    def fold(op, tiles):
        acc = tiles[0]
        for t in tiles[1:]:
            acc = op(acc, t)
        return acc

    for sub in range(q_ref.shape[0] // sub_rows):
        QB = sub_rows
        rows = slice(sub * QB, (sub + 1) * QB)
        q0 = pl.multiple_of(qb * q_ref.shape[0] + sub * QB, QB)
        if latent:
            win = QB + 2 * WINDOW
            start = pl.multiple_of(jnp.clip(q0 - WINDOW, 0, seq - win), WINDOW)
            qpos = q0 + (lax.broadcasted_iota(jnp.int32, (2 * QB, win), 0) & (QB - 1))
            kpos = start + lax.broadcasted_iota(jnp.int32, (2 * QB, win), 1)
            band = jnp.abs(qpos - kpos) <= WINDOW
            cos_q = cos_ref[pl.ds(q0, QB), :]
            sin_q = sin_ref[pl.ds(q0, QB), :]
        else:
            win = seq
            start = 0
        row = lax.broadcasted_iota(jnp.int32, (2 * QB, 1), 0)
        for hk in range(ATT_KV_HEADS):
            sl = slice(hk * HD, (hk + 1) * HD)
            qs = []
            for g in range(2):
                hq = hk * 2 + g
                qn = _unit_rms(q_ref[rows, hq * HD:(hq + 1) * HD], qw_ref[...])
                if latent:
                    qn = _rope(qn, cos_q, sin_q)
                qs.append((qn * (HD ** -0.5)).astype(bf16))
            qg = jnp.concatenate(qs, axis=0)
            snk = jnp.where(row < QB, sink_ref[hk * 2], sink_ref[hk * 2 + 1])
            s1 = lax.dot_general(qg, kb_scr[pl.ds(start, win), sl], nt_dims, preferred_element_type=f32)
            if latent:
                s1 = jnp.where(band, s1, NEG_INF)
                s2 = lax.dot_general(qg, ckb_scr[:, sl], nt_dims, preferred_element_type=f32)
            s_tiles = lane_tiles(s1) + (lane_tiles(s2) if latent else [])
            m = jnp.maximum(jnp.max(fold(jnp.maximum, s_tiles), axis=-1, keepdims=True), snk)
            p1 = jnp.exp(s1 - m)
            if latent:
                p2 = jnp.exp(s2 - m)
            p_tiles = lane_tiles(p1) + (lane_tiles(p2) if latent else [])
            den = jnp.sum(fold(jnp.add, p_tiles), axis=-1, keepdims=True) + jnp.exp(snk - m)
            o = jnp.dot((p1 / den).astype(bf16), vb_scr[pl.ds(start, win), sl], preferred_element_type=f32)
            if latent:
                o += jnp.dot((p2 / den).astype(bf16), cvb_scr[:, sl], preferred_element_type=f32)
            for g in range(2):
                hq = hk * 2 + g
                o_ref[rows, hq * HD:(hq + 1) * HD] = o[g * QB:(g + 1) * QB].astype(bf16)


def _attention(z, qw, kw, sink, rope, cache, grp):
    nb, seq = grp.nb, grp.seq
    latent = not grp.ctx
    QB = min(seq, ATT_QB_LAT if latent else ATT_QB_CTX)
    sub_rows = min(QB, ATT_SUB_LAT) if latent else QB
    nqb = seq // QB
    kvw = ATT_KV_HEADS * HEAD_DIM
    in_specs = [
        pl.BlockSpec((QB, 512), lambda b, i: (b * nqb + i, AQ // 512)),
        pl.BlockSpec((seq, kvw), lambda b, i: (b, AK // kvw)),
        pl.BlockSpec((seq, kvw), lambda b, i: (b, AV // kvw)),
        pl.BlockSpec((1, HEAD_DIM), lambda b, i: (0, 0)),
        pl.BlockSpec((1, HEAD_DIM), lambda b, i: (0, 0)),
        pl.BlockSpec(memory_space=pltpu.SMEM),
    ]
    args = [z, z, z, qw, kw, sink]
    scratch = [pltpu.VMEM((seq, kvw), bf16), pltpu.VMEM((seq, kvw), bf16)]
    out_specs = [pl.BlockSpec((QB, 512), lambda b, i: (b * nqb + i, 0))]
    out_shape = [jax.ShapeDtypeStruct((nb * seq, ATT_Q_HEADS * HEAD_DIM), bf16)]
    if latent:
        in_specs += [pl.BlockSpec((seq, HEAD_DIM), lambda b, i: (0, 0)),
                     pl.BlockSpec((seq, HEAD_DIM), lambda b, i: (0, 0)),
                     pl.BlockSpec((None, PAST_LEN, kvw), lambda b, i: (b, 0, 0)),
                     pl.BlockSpec((None, PAST_LEN, kvw), lambda b, i: (b, 0, 0))]
        args += [rope[0], rope[1], cache[0], cache[1]]
        scratch += [pltpu.VMEM((PAST_LEN, kvw), bf16), pltpu.VMEM((PAST_LEN, kvw), bf16)]
    else:
        kv_out = pl.BlockSpec((seq, kvw), lambda b, i: (b, 0))
        out_specs += [kv_out, kv_out]
        out_shape += [jax.ShapeDtypeStruct((nb * seq, kvw), f32)] * 2
    return pl.pallas_call(
        functools.partial(_attn_kernel, seq=seq, latent=latent, sub_rows=sub_rows),
        grid=(nb, nqb),
        in_specs=in_specs,
        out_specs=out_specs,
        out_shape=out_shape,
        scratch_shapes=scratch,
        compiler_params=_cp(2),
        name="attention_lat" if latent else "attention_ctx",
    )(*args)


def _merge_kernel(h_ref, o0, o1, o2, o3, g0, g1, g2, g3, wbr_ref, *rest):
    out_ref = rest[len(rest) // 2]
    h = h_ref[...]
    acc = None
    for b, (o_ref, g_ref) in enumerate(((o0, g0), (o1, g1), (o2, g2), (o3, g3))):
        gate = _sigmoid(jnp.dot(h, g_ref[...], preferred_element_type=f32))
        t = gate * jnp.dot(o_ref[...], wbr_ref[b], preferred_element_type=f32)
        acc = t if acc is None else acc + t
    out_ref[...] = acc.astype(bf16)
    n_cast = len(rest) // 2
    for src, dst in zip(rest[:n_cast], rest[n_cast + 1:]):
        dst[...] = src[...].astype(bf16)


def _merge(h, branches, w_in, w_br, grp, cast_next=None):
    tn = 512
    g0 = ZW // tn
    gb = D_MODEL // tn
    n_i, n_j = grp.n // TM, D_MODEL // tn
    o_spec = pl.BlockSpec((TM, BRANCH_WIDTH), lambda i, j: (i, 0))

    def gate_spec(b):
        return pl.BlockSpec((D_MODEL, tn), lambda i, j: (0, g0 + b * gb + j))

    in_specs = [
        pl.BlockSpec((TM, D_MODEL), lambda i, j: (i, 0)),
        o_spec, o_spec, o_spec, o_spec,
        gate_spec(0), gate_spec(1), gate_spec(2), gate_spec(3),
        pl.BlockSpec((N_BRANCH, BRANCH_WIDTH, tn), lambda i, j: (0, 0, j)),
    ]
    args = [h, *branches, w_in, w_in, w_in, w_in, w_br]
    out_specs = [pl.BlockSpec((TM, tn), lambda i, j: (i, j))]
    out_shape = [jax.ShapeDtypeStruct((grp.n, D_MODEL), bf16)]
    if cast_next is not None:
        nxt, stacks = cast_next
        for w in stacks:
            rows, cols = w.shape[1:]
            slab = rows // (n_i * n_j)
            in_specs.append(pl.BlockSpec((None, slab, cols), lambda i, j: (nxt, i * n_j + j, 0)))
            out_specs.append(pl.BlockSpec((slab, cols), lambda i, j: (i * n_j + j, 0)))
            out_shape.append(jax.ShapeDtypeStruct((rows, cols), bf16))
            args.append(w)
    res = pl.pallas_call(
        _merge_kernel,
        grid=(n_i, n_j),
        in_specs=in_specs,
        out_specs=out_specs,
        out_shape=out_shape,
        compiler_params=_cp(2, vmem=BIG_VMEM_LIMIT if cast_next is not None else VMEM_LIMIT),
        name="merge",
    )(*args)
    return res[0], res[1:]


def _outproj_kernel(x_ref, mod_ref, m_ref, w_ref, o_ref):
    o_ref[...] = x_ref[...] + mod_ref[2:3, :] * jnp.dot(m_ref[...], w_ref[...], preferred_element_type=f32)


def _out_proj(x, mod, merged, w_out, layer, grp):
    tm = OUT_TM
    return pl.pallas_call(
        _outproj_kernel,
        grid=(grp.n // tm,),
        in_specs=[
            pl.BlockSpec((tm, D_MODEL), lambda i: (i, 0)),
            pl.BlockSpec((None, None, 6, D_MODEL), lambda i: (layer, _mod_row(grp, i, tm), 0, 0)),
            pl.BlockSpec((tm, D_MODEL), lambda i: (i, 0)),
            pl.BlockSpec((D_MODEL, D_MODEL), lambda i: (0, 0)),
        ],
        out_specs=pl.BlockSpec((tm, D_MODEL), lambda i: (i, 0)),
        out_shape=jax.ShapeDtypeStruct((grp.n, D_MODEL), f32),
        compiler_params=_cp(1, vmem=BIG_VMEM_LIMIT),
        name="out_proj",
    )(x, mod, merged, w_out)


def _ffn_kernel(x_ref, mod_ref, nw_ref, w1_ref, w2_ref, o_ref, h_scr, r_scr, gs_scr):
    j = pl.program_id(1)

    @pl.when(j == 0)
    def _():
        _modnorm_rows(x_ref, nw_ref[...], mod_ref[3:4, :], mod_ref[4:5, :], h_scr, r_scr, gs_scr, copy_ref=o_ref)

    a = jnp.maximum(jnp.dot(h_scr[...], w1_ref[...], preferred_element_type=f32), 0.0)
    a2 = (a * a).astype(bf16)
    cw = 512
    for cidx in range(D_MODEL // cw):
        cs = slice(cidx * cw, (cidx + 1) * cw)
        o_ref[:, cs] += mod_ref[5:6, cs] * jnp.dot(a2, w2_ref[:, cs], preferred_element_type=f32)


def _ffn(x, mod, nw, w1, w2, layer, grp):
    tf = 1024
    nj = D_FF // tf
    return pl.pallas_call(
        _ffn_kernel,
        grid=(grp.n // TM, nj),
        in_specs=[
            pl.BlockSpec((TM, D_MODEL), lambda i, j: (i, 0)),
            pl.BlockSpec((None, None, 6, D_MODEL), lambda i, j: (layer, _mod_row(grp, i), 0, 0)),
            pl.BlockSpec((None, 1, D_MODEL), lambda i, j: (layer, 0, 0)),
            pl.BlockSpec((D_MODEL, tf), lambda i, j: (0, j)),
            pl.BlockSpec((tf, D_MODEL), lambda i, j: (j, 0)),
        ],
        out_specs=pl.BlockSpec((TM, D_MODEL), lambda i, j: (i, 0)),
        out_shape=jax.ShapeDtypeStruct((grp.n, D_MODEL), f32),
        scratch_shapes=[pltpu.VMEM((TM, D_MODEL), bf16), pltpu.VMEM((TM, 128), f32),
                        pltpu.VMEM((2, SUB, D_MODEL), f32)],
        compiler_params=_cp(2, vmem=BIG_VMEM_LIMIT),
        name="ffn",
    )(x, mod, nw, w1, w2)


def _rope_tables(seq):
    t = jnp.arange(seq)
    row = (t // GRID_W).astype(f32)
    col = (t % GRID_W).astype(f32)
    n_pairs = HEAD_DIM // 4
    freqs = ROPE_BASE ** (-jnp.arange(n_pairs, dtype=f32) / n_pairs)
    ang = jnp.concatenate([row[:, None] * freqs, col[:, None] * freqs], axis=-1)
    cos = jnp.cos(ang)
    sin = jnp.sin(ang)
    return jnp.concatenate([cos, cos], axis=-1), jnp.concatenate([-sin, sin], axis=-1)


def _ssm_params(a_re, a_im, log_dt, b_re, b_im, c_re, c_im):
    lead = a_re.shape[:-2]
    dt = jnp.exp(log_dt)[..., None]
    mag = jnp.exp(a_re * dt)
    ab_re = mag * jnp.cos(a_im * dt)
    ab_im = mag * jnp.sin(a_im * dt)
    den = a_re * a_re + a_im * a_im
    q_re = (((ab_re - 1.0) * a_re + ab_im * a_im) / den)[..., None]
    q_im = ((ab_im * a_re - (ab_re - 1.0) * a_im) / den)[..., None]
    bb_re = q_re * b_re - q_im * b_im
    bb_im = q_re * b_im + q_im * b_re
    a = jnp.stack([ab_re.reshape(lead + (SSM_N,)), ab_im.reshape(lead + (SSM_N,))], axis=-2)

    ncb = SSM_N // SSM_CB
    gpb = SSM_GROUPS // ncb
    eye = jnp.eye(gpb, dtype=f32)[:, None, :, None]

    def diag_blocks(w, rows, cols):
        w = jnp.swapaxes(w.reshape(lead + (ncb, gpb) + w.shape[-2:]), -1, -2)
        return (w[..., None, :] * eye).reshape(lead + (ncb, rows, cols))

    bd = jnp.stack([diag_blocks(w, SSM_CI, SSM_CB) for w in (bb_re, bb_im)], axis=-4).astype(bf16)
    cd = jnp.stack([diag_blocks(w, SSM_CB, SSM_CI) for w in (c_re, -c_im)], axis=-4).astype(bf16)
    return a, bd, cd


def _lru_params(conv_w, conv_b, wa, ba, wx, bx, lam):
    w = LRU_WIDTH
    return {"cw": conv_w, "cb": conv_b.reshape(1, w),
            "wa": (0.5 * wa).astype(bf16), "ba": (0.5 * ba).reshape(2, 1, w),
            "wx": (0.5 * wx).astype(bf16), "bx": (0.5 * bx).reshape(2, 1, w),
            "sp": (-0.5 * LRU_C * jax.nn.softplus(-lam)).reshape(2, 1, w)}


def _mixers(z, grp, layer, prm, states):
    nb, seq = grp.nb, grp.seq
    z3 = z.reshape(nb, seq, ZW)
    s_ret0, s_lru0, s_ssm0, cache = states

    o_ret, s_ret = _retention(z, prm["ret_lg"], prm["ret_gn"], s_ret0, grp)

    hf, xc, lru_f = _lru_dir(z3, None, None if s_lru0 is None else s_lru0[:, 0], prm["lru"], grp, 0)
    o_lru, lru_b = _lru_dir(z3, (hf, xc), None if s_lru0 is None else s_lru0[:, 1], prm["lru"], grp, 1)

    att = _attention(z, prm["att_qw"], prm["att_kw"], prm["att_sink"], prm["rope"], cache, grp)

    yf, ssm_f = _s5_dir(z3, None, None if s_ssm0 is None else s_ssm0[:, 0], prm["ssm"], grp, 0)
    o_ssm, ssm_b = _s5_dir(z3, yf, None if s_ssm0 is None else s_ssm0[:, 1], prm["ssm"], grp, 1)

    branches = [o_ret, o_lru.reshape(grp.n, LRU_WIDTH), att[0], o_ssm.reshape(grp.n, SSM_WIDTH)]
    new_state = None
    if grp.ctx:
        ssm_fin = jnp.stack([ssm_f, ssm_b], axis=1)
        new_state = (att[1].reshape(nb, seq, ATT_KV_HEADS, HEAD_DIM), att[2].reshape(nb, seq, ATT_KV_HEADS, HEAD_DIM),
                     s_ret, jnp.stack([lru_f, lru_b], axis=1),
                     ssm_fin[:, :, :SSM_N].reshape(nb, 2, SSM_GROUPS, SSM_STATE),
                     ssm_fin[:, :, SSM_N:].reshape(nb, 2, SSM_GROUPS, SSM_STATE))
    return branches, new_state


def kernel(x_prompt, x_sample, cache_attn_k, cache_attn_v, state_ret, state_lru, state_ssm_re, state_ssm_im, c, c_ctx, w_mod, b_mod, norm1, w_in, ret_decay_logit, ret_gn, lru_conv_w, lru_conv_b, lru_wa, lru_ba, lru_wx, lru_bx, lru_lambda, att_q_norm, att_k_norm, att_sink, ssm_a_re, ssm_a_im, ssm_log_dt, ssm_b_re, ssm_b_im, ssm_c_re, ssm_c_im, ssm_d, ssm_w_glu, ssm_b_glu, w_br, w_out, norm2, w_ff1, w_ff2):
    xs = {PROMPT: x_prompt.reshape(PROMPT.n, D_MODEL), SAMPLE: x_sample.reshape(SAMPLE.n, D_MODEL)}
    cond = jnp.concatenate([c, c_ctx[None, :], jnp.zeros((COND_ROWS - SAMPLE.nb - 1, D_MODEL), f32)], axis=0)
    mod = _modulation(cond, w_mod, b_mod).reshape(DEPTH, COND_ROWS, 6, D_MODEL)

    w_stacks = (w_in, w_br.reshape(DEPTH, N_BRANCH * BRANCH_WIDTH, D_MODEL), w_out, w_ff1, w_ff2)
    w_layer = [w_stacks[0][0].astype(bf16), w_stacks[1][0].astype(bf16), None, None, None]
    norm1_r = norm1.reshape(DEPTH, 1, D_MODEL)
    norm2_r = norm2.reshape(DEPTH, 1, D_MODEL)
    rope = _rope_tables(SAMPLE.seq)
    log_gamma = -jax.nn.softplus(-ret_decay_logit)
    kvw = ATT_KV_HEADS * HEAD_DIM

    ssm_a_all, ssm_bd_all, ssm_cd_all = _ssm_params(ssm_a_re, ssm_a_im, ssm_log_dt, ssm_b_re, ssm_b_im,
                                                    ssm_c_re, ssm_c_im)
    new_states = []
    for l in range(DEPTH):
        ssm_a, ssm_bd, ssm_cd = ssm_a_all[l], ssm_bd_all[l], ssm_cd_all[l]
        prm = {
            "ret_lg": jnp.broadcast_to(log_gamma[l].T[:, :, None], (RET_HEADS, 2, 128)),
            "ret_gn": ret_gn[l].reshape(1, RET_HEADS * 128),
            "lru": _lru_params(lru_conv_w[l], lru_conv_b[l], lru_wa[l], lru_ba[l], lru_wx[l], lru_bx[l],
                               lru_lambda[l]),
            "att_qw": att_q_norm[l].reshape(1, HEAD_DIM), "att_kw": att_k_norm[l].reshape(1, HEAD_DIM),
            "att_sink": att_sink[l], "rope": rope,
            "ssm": {"a": ssm_a, "bd": ssm_bd, "cd": ssm_cd, "d": ssm_d[l].reshape(1, SSM_WIDTH),
                    "wglu": ssm_w_glu[l].astype(bf16), "bglu": ssm_b_glu[l].reshape(1, SSM_WIDTH)},
        }
        states = {
            PROMPT: (None, None, None, None),
            SAMPLE: (state_ret[:, l], state_lru[:, l],
                     jnp.concatenate([state_ssm_re[:, l].reshape(SAMPLE.nb, 2, SSM_N),
                                      state_ssm_im[:, l].reshape(SAMPLE.nb, 2, SSM_N)], axis=-1),
                     (cache_attn_k[:, l].reshape(SAMPLE.nb, PAST_LEN, kvw),
                      cache_attn_v[:, l].reshape(SAMPLE.nb, PAST_LEN, kvw))),
        }
        w_in_l, w_br_l, w_out_l, w_ff1_l, w_ff2_l = w_layer
        w_br_l = w_br_l.reshape(N_BRANCH, BRANCH_WIDTH, D_MODEL)
        for grp in (PROMPT, SAMPLE):
            x = xs[grp]
            z, h = _proj_in(x, mod, norm1_r, w_in_l, l, grp)
            branches, st = _mixers(z, grp, l, prm, states[grp])
            if grp.ctx:
                new_states.append(st)
            cast = None
            if grp is SAMPLE and l + 1 < DEPTH:
                cast = (l + 1, w_stacks)
            elif grp is PROMPT and l == 0:
                cast = (0, w_stacks[2:])
            merged, w_cast = _merge(h, branches, w_in_l, w_br_l, grp, cast)
            if cast is not None and cast[0] == l:
                w_out_l, w_ff1_l, w_ff2_l = w_cast
            elif cast is not None:
                w_layer = list(w_cast)
            x = _out_proj(x, mod, merged, w_out_l, l, grp)
            xs[grp] = _ffn(x, mod, norm2_r, w_ff1_l, w_ff2_l, l, grp)

    y_prompt = xs[PROMPT].reshape(PROMPT.nb, PROMPT.seq, D_MODEL)
    y_sample = xs[SAMPLE].reshape(SAMPLE.nb, SAMPLE.seq, D_MODEL)
    return (y_prompt, y_sample) + tuple(jnp.stack([st[i] for st in new_states], axis=1) for i in range(6))
```

```python
import functools
from typing import NamedTuple

import jax
import jax.numpy as jnp
from jax import lax
from jax.experimental import pallas as pl
from jax.experimental.pallas import tpu as pltpu

f32 = jnp.float32
bf16 = jnp.bfloat16

D_MODEL = 2048
DEPTH = 4
PAST_LEN = 256
GRID_W = 64
CHUNK = 128
ATT_QB_LAT = 512
ATT_SUB_LAT = 128
ATT_QB_CTX = 256
EPS = 1e-6
NEG_INF = -1e30
RET_HEADS = 4
RET_DK = 128
LRU_WIDTH = 512
LRU_BLOCKS = 4
LRU_C = 8.0
CONV_W = 4
ATT_Q_HEADS = 4
ATT_KV_HEADS = 2
HEAD_DIM = 128
WINDOW = 128
ROPE_BASE = 10000.0
SSM_WIDTH = 512
SSM_GROUP = 16
SSM_GROUPS = 32
SSM_STATE = 64
SSM_N = SSM_GROUPS * SSM_STATE
SSM_CB = 512
SSM_CI = SSM_CB // SSM_STATE * SSM_GROUP
N_BRANCH = 4
BRANCH_WIDTH = 512
D_FF = 4 * D_MODEL
RQ, RK, RV, RG, LX, LG, AQ, AK, AV, SU, ZW = 0, 512, 1024, 1536, 2048, 2560, 3072, 3584, 3840, 4096, 4608
SUB = 8
COND_ROWS = 16
VMEM_LIMIT = 56 * 1024 * 1024
BIG_VMEM_LIMIT = 62 * 1024 * 1024

TM = 1024
OUT_TM = 512
LRU_TC = 128
SSM_TC = 128
ROW_BLK = 16


class Group(NamedTuple):
    nb: int
    seq: int
    ctx: bool

    @property
    def n(self):
        return self.nb * self.seq


PROMPT = Group(32, 256, True)
SAMPLE = Group(8, 2048, False)
CTX_ROW = SAMPLE.nb


def _cp(n_axes, vmem=VMEM_LIMIT):
    return pltpu.CompilerParams(dimension_semantics=("arbitrary",) * n_axes, vmem_limit_bytes=vmem)


def _sigmoid(x):
    return 0.5 * jnp.tanh(0.5 * x) + 0.5


def _mod_row(grp, i, tm=None):
    return CTX_ROW if grp.ctx else i // (grp.seq // (tm or TM))


def _modnorm_rows(x_ref, nw, shift, scale, h_ref, r_scr, gs_scr, copy_ref=None):
    tm, d = x_ref.shape
    lane_tiles = d // 128

    def ssq_body(r, carry):
        rows = pl.ds(pl.multiple_of(r * SUB, SUB), SUB)
        acc = None
        for t in range(lane_tiles):
            v = x_ref[rows, t * 128:(t + 1) * 128]
            acc = v * v if acc is None else acc + v * v
        r_scr[rows, :] = acc
        return carry

    lax.fori_loop(0, tm // SUB, ssq_body, 0, unroll=4)
    ssq = jnp.sum(r_scr[...], axis=-1, keepdims=True)
    r_scr[...] = jnp.broadcast_to(lax.rsqrt(ssq * (1.0 / d) + EPS), (tm, 128))
    gs_scr[0] = jnp.broadcast_to(nw * (1.0 + scale), (SUB, d))
    gs_scr[1] = jnp.broadcast_to(shift, (SUB, d))

    def out_body(r, carry):
        r0 = pl.multiple_of(r * ROW_BLK, ROW_BLK)
        halves = [pl.ds(pl.multiple_of(r0 + k * SUB, SUB), SUB) for k in range(ROW_BLK // SUB)]
        invs = [r_scr[rows, :] for rows in halves]
        for t in range(lane_tiles):
            cols = slice(t * 128, (t + 1) * 128)
            gain = gs_scr[0, :, cols]
            shf = gs_scr[1, :, cols]
            xs = [x_ref[rows, cols] for rows in halves]
            if copy_ref is not None:
                for rows, xv in zip(halves, xs):
                    copy_ref[rows, cols] = xv
            parts = [xv * inv * gain + shf for xv, inv in zip(xs, invs)]
            h_ref[pl.ds(r0, ROW_BLK), cols] = jnp.concatenate(parts, axis=0).astype(bf16)
        return carry

    lax.fori_loop(0, tm // ROW_BLK, out_body, 0, unroll=2)


def _mod_kernel(c_ref, w_ref, b_ref, o_ref):
    c = c_ref[...]
    s = (c * _sigmoid(c)).astype(bf16)
    o_ref[...] = jnp.dot(s, w_ref[...].astype(bf16), preferred_element_type=f32) + b_ref[...]


def _modulation(cond, w_mod, b_mod):
    tn = 1024
    n6 = 6 * D_MODEL
    return pl.pallas_call(
        _mod_kernel,
        grid=(DEPTH, n6 // tn),
        in_specs=[
            pl.BlockSpec((COND_ROWS, D_MODEL), lambda l, j: (0, 0)),
            pl.BlockSpec((None, D_MODEL, tn), lambda l, j: (l, 0, j)),
            pl.BlockSpec((None, 1, tn), lambda l, j: (l, 0, j)),
        ],
        out_specs=pl.BlockSpec((None, COND_ROWS, tn), lambda l, j: (l, 0, j)),
        out_shape=jax.ShapeDtypeStruct((DEPTH, COND_ROWS, n6), f32),
        compiler_params=_cp(2),
        name="modulation",
    )(cond, w_mod, b_mod.reshape(DEPTH, 1, n6))


def _proj_kernel(x_ref, mod_ref, nw_ref, w_ref, o_ref, h_ref, r_scr, gs_scr):
    @pl.when(pl.program_id(1) == 0)
    def _():
        _modnorm_rows(x_ref, nw_ref[...], mod_ref[0:1, :], mod_ref[1:2, :], h_ref, r_scr, gs_scr)

    o_ref[...] = jnp.dot(h_ref[...], w_ref[...], preferred_element_type=f32)


def _proj_in(x, mod, nw, w_in, layer, grp):
    tn = 1536
    return pl.pallas_call(
        _proj_kernel,
        grid=(grp.n // TM, ZW // tn),
        in_specs=[
            pl.BlockSpec((TM, D_MODEL), lambda i, j: (i, 0)),
            pl.BlockSpec((None, None, 6, D_MODEL), lambda i, j: (layer, _mod_row(grp, i), 0, 0)),
            pl.BlockSpec((None, 1, D_MODEL), lambda i, j: (layer, 0, 0)),
            pl.BlockSpec((D_MODEL, tn), lambda i, j: (0, j)),
        ],
        out_specs=[pl.BlockSpec((TM, tn), lambda i, j: (i, j)),
                   pl.BlockSpec((TM, D_MODEL), lambda i, j: (i, 0))],
        out_shape=[jax.ShapeDtypeStruct((grp.n, ZW), f32), jax.ShapeDtypeStruct((grp.n, D_MODEL), bf16)],
        scratch_shapes=[pltpu.VMEM((TM, 128), f32), pltpu.VMEM((2, SUB, D_MODEL), f32)],
        compiler_params=_cp(2),
        name="proj_in",
    )(x, mod, nw, w_in)


def _ret_kernel(*refs, nc, has_s0, emit_state):
    q_ref, k_ref, v_ref, g_ref, lg_ref, gn_ref = refs[:6]
    pos = 6
    s0_ref = None
    if has_s0:
        s0_ref = refs[pos]
        pos += 1
    o_ref = refs[pos]
    pos += 1
    sout_ref = None
    if emit_state:
        sout_ref = refs[pos]
        pos += 1
    w_scr, kv_scr = refs[pos], refs[pos + 1]

    C = CHUNK
    H = RET_HEADS
    DEC, W_IN_F, W_IN_B, W_END_F, W_END_B = range(5)
    tn_dims = (((0,), (0,)), ((), ()))
    nt_dims = (((1,), (1,)), ((), ()))

    @pl.when(pl.program_id(0) == 0)
    def _():
        ii = lax.broadcasted_iota(jnp.int32, (C, C), 0).astype(f32)
        jj = lax.broadcasted_iota(jnp.int32, (C, C), 1).astype(f32)
        rel = ii - jj
        for h in range(H):
            lgf = lg_ref[h, 0:1, :]
            lgb = lg_ref[h, 1:2, :]
            w_scr[h, DEC] = (jnp.where(rel >= 0, jnp.exp(jnp.maximum(rel, 0.0) * lgf), 0.0)
                             + jnp.where(rel <= 0, jnp.exp(jnp.maximum(-rel, 0.0) * lgb), 0.0))
            w_scr[h, W_IN_F] = jnp.exp((ii + 1.0) * lgf)
            w_scr[h, W_IN_B] = jnp.exp((C - ii) * lgb)
            w_scr[h, W_END_F] = jnp.exp((C - 1.0 - ii) * lgf)
            w_scr[h, W_END_B] = jnp.exp(ii * lgb)

    def kv_body(n, carry):
        r = pl.multiple_of(n * C, C)
        for h in range(H):
            cols = slice(h * C, (h + 1) * C)
            k = k_ref[pl.ds(r, C), cols]
            vb = v_ref[pl.ds(r, C), cols].astype(bf16)
            kw = jnp.concatenate([k * w_scr[h, W_END_F], k * w_scr[h, W_END_B]], axis=1).astype(bf16)
            kv = lax.dot_general(kw, vb, tn_dims, preferred_element_type=f32)
            kv_scr[0, n * H + h] = kv[:C]
            kv_scr[1, n * H + h] = kv[C:]
        return carry

    lax.fori_loop(0, nc, kv_body, 0, unroll=min(nc, 4))

    for h in range(H):
        cd_f = jnp.exp(C * lg_ref[h, 0:1, :])
        cd_b = jnp.exp(C * lg_ref[h, 1:2, :])

        def fwd_body(n, s, h=h, cd_f=cd_f):
            kv = kv_scr[0, n * H + h]
            kv_scr[0, n * H + h] = s
            return cd_f * s + kv

        def bwd_body(i, s, h=h, cd_b=cd_b):
            n = nc - 1 - i
            kv = kv_scr[1, n * H + h]
            kv_scr[1, n * H + h] = s
            return cd_b * s + kv

        s_f = lax.fori_loop(0, nc, fwd_body, s0_ref[0, h] if has_s0 else jnp.zeros((C, C), f32))
        s_b = lax.fori_loop(0, nc, bwd_body, s0_ref[1, h] if has_s0 else jnp.zeros((C, C), f32))
        if emit_state:
            sout_ref[0, h] = s_f
            sout_ref[1, h] = s_b

    def out_body(n, carry):
        r = pl.multiple_of(n * C, C)
        for h in range(H):
            cols = slice(h * C, (h + 1) * C)
            q = q_ref[pl.ds(r, C), cols] * (RET_DK ** -0.5)
            kb = k_ref[pl.ds(r, C), cols].astype(bf16)
            vb = v_ref[pl.ds(r, C), cols].astype(bf16)
            sc = lax.dot_general(q.astype(bf16), kb, nt_dims, preferred_element_type=f32) * w_scr[h, DEC]
            lhs = jnp.concatenate([sc, q * w_scr[h, W_IN_F], q * w_scr[h, W_IN_B]], axis=1).astype(bf16)
            rhs = jnp.concatenate([vb, kv_scr[0, n * H + h].astype(bf16), kv_scr[1, n * H + h].astype(bf16)], axis=0)
            o = jnp.dot(lhs, rhs, preferred_element_type=f32)
            mu = jnp.mean(o, axis=-1, keepdims=True)
            d = o - mu
            var = jnp.mean(d * d, axis=-1, keepdims=True)
            on = d * lax.rsqrt(var + EPS) * gn_ref[:, cols]
            g = g_ref[pl.ds(r, C), cols]
            o_ref[pl.ds(r, C), cols] = (g * _sigmoid(g) * on).astype(bf16)
        return carry

    lax.fori_loop(0, nc, out_body, 0, unroll=True)


def _retention(z, lg, gn, s0, grp):
    seq = grp.seq
    W = RET_HEADS * 128
    in_specs = [
        pl.BlockSpec((seq, W), lambda b: (b, RQ // W)),
        pl.BlockSpec((seq, W), lambda b: (b, RK // W)),
        pl.BlockSpec((seq, W), lambda b: (b, RV // W)),
        pl.BlockSpec((seq, W), lambda b: (b, RG // W)),
        pl.BlockSpec((RET_HEADS, 2, 128), lambda b: (0, 0, 0)),
        pl.BlockSpec((1, W), lambda b: (0, 0)),
    ]
    args = [z, z, z, z, lg, gn]
    st_spec = pl.BlockSpec((None, 2, RET_HEADS, 128, 128), lambda b: (b, 0, 0, 0, 0))
    if s0 is not None:
        in_specs.append(st_spec)
        args.append(s0)
    out_specs = [pl.BlockSpec((seq, W), lambda b: (b, 0))]
    out_shape = [jax.ShapeDtypeStruct((grp.n, W), bf16)]
    if grp.ctx:
        out_specs.append(st_spec)
        out_shape.append(jax.ShapeDtypeStruct((grp.nb, 2, RET_HEADS, 128, 128), f32))
    nc = seq // CHUNK
    res = pl.pallas_call(
        functools.partial(_ret_kernel, nc=nc, has_s0=s0 is not None, emit_state=grp.ctx),
        grid=(grp.nb,),
        in_specs=in_specs,
        out_specs=out_specs,
        out_shape=out_shape,
        scratch_shapes=[pltpu.VMEM((RET_HEADS, 5, 128, 128), f32),
                        pltpu.VMEM((2, nc * RET_HEADS, 128, 128), f32)],
        compiler_params=_cp(1),
        name="retention",
    )(*args)
    return res if grp.ctx else (res[0], None)


def _gather_tm(src_ref, nt, dst_ref, row0, unroll=2):
    w = src_ref.shape[-1]

    def body(i, carry):
        v = src_ref[:, pl.ds(pl.multiple_of(i * SUB, SUB), SUB), :]
        rows = pl.ds(pl.multiple_of(row0 + i * SUB * SUB, SUB), SUB * SUB)
        dst_ref[rows, :] = jnp.swapaxes(v, 0, 1).reshape(SUB * SUB, w)
        return carry

    lax.fori_loop(0, nt // SUB, body, 0, unroll=unroll)


def _scatter_bm(src_ref, nt, dst_ref, unroll=2):
    w = src_ref.shape[-1]

    def body(i, carry):
        v = src_ref[pl.ds(pl.multiple_of(i * SUB * SUB, SUB * SUB), SUB * SUB), :].reshape(SUB, SUB, w)
        dst_ref[:, pl.ds(pl.multiple_of(i * SUB, SUB), SUB), :] = jnp.swapaxes(v, 0, 1)
        return carry

    lax.fori_loop(0, nt // SUB, body, 0, unroll=unroll)


def _lru_kernel(*refs, tc, nch, direction, has_s0):
    if direction == 0:
        xprev_ref, xcur_ref, xnext_ref, cw_ref, cb_ref = refs[:5]
        pos = 5
    else:
        xc_ref = refs[0]
        pos = 1
    wa_ref, ba_ref, wx_ref, bx_ref, sp_ref = refs[pos:pos + 5]
    pos += 5
    s0_ref = None
    if has_s0:
        s0_ref = refs[pos]
        pos += 1
    if direction == 0:
        out_ref, xc_out_ref, fin_ref = refs[pos:pos + 3]
        xe_scr, a_scr, b_scr, h_scr = refs[pos + 3:pos + 7]
    else:
        gate_ref, hf_ref, out_ref, fin_ref = refs[pos:pos + 4]
        a_scr, b_scr, h_scr, g_scr, y_scr = refs[pos + 4:pos + 9]

    R = tc * SUB
    W = LRU_WIDTH
    j = pl.program_id(1)
    c = j if direction == 0 else nch - 1 - j

    @pl.when(j == 0)
    def _():
        h_scr[...] = s0_ref[...] if has_s0 else jnp.zeros((SUB, W), f32)

    if direction == 0:
        for t in range(2):
            v = xprev_ref[:, SUB - 2 + t:SUB - 1 + t, :].reshape(SUB, W)
            xe_scr[t * SUB:(t + 1) * SUB, :] = jnp.where(c > 0, v, 0.0)
        _gather_tm(xcur_ref, tc, xe_scr, 2 * SUB)
        xe_scr[2 * SUB + R:3 * SUB + R, :] = jnp.where(c < nch - 1, xnext_ref[:, 0:1, :].reshape(SUB, W), 0.0)
        xc = cb_ref[...] + xe_scr[0:R, :] * cw_ref[0:1, :]
        for t in range(1, CONV_W):
            xc = xc + xe_scr[t * SUB:t * SUB + R, :] * cw_ref[t:t + 1, :]
        xc_out_ref[...] = xc
    else:
        xc = xc_ref[...]
        _gather_tm(gate_ref, tc, g_scr, 0)

    bd = W // LRU_BLOCKS
    for n in range(LRU_BLOCKS):
        sl = slice(n * bd, (n + 1) * bd)
        xs = xc[:, sl]
        xb = xs.astype(bf16)
        tr = jnp.tanh(jnp.dot(xb, wa_ref[n], preferred_element_type=f32) + ba_ref[:, sl])
        ti = jnp.tanh(jnp.dot(xb, wx_ref[n], preferred_element_type=f32) + bx_ref[:, sl])
        log_a = sp_ref[:, sl] * (tr + 1.0)
        a = jnp.exp(log_a)
        a_scr[:, sl] = a
        t = -jnp.tanh(log_a) * (a * a + 1.0)
        root = jnp.where(t > 0.0, t * lax.rsqrt(t), 0.0)
        b_scr[:, sl] = (0.5 * root) * ((ti + 1.0) * xs)

    def step(s, h):
        t = s if direction == 0 else tc - 1 - s
        r0 = pl.multiple_of(t * SUB, SUB)
        h = a_scr[pl.ds(r0, SUB), :] * h + b_scr[pl.ds(r0, SUB), :]
        b_scr[pl.ds(r0, SUB), :] = h
        return h

    h = lax.fori_loop(0, tc, step, h_scr[...], unroll=8)
    h_scr[...] = h
    fin_ref[...] = h
    if direction == 0:
        out_ref[...] = b_scr[...]
    else:
        b_scr[...] = jax.nn.gelu(g_scr[...]) * (hf_ref[...] + b_scr[...])
        _scatter_bm(b_scr, tc, y_scr)
        out_ref[...] = y_scr[...].astype(bf16)


def _lru_dir(z3, prev, s0, prm, grp, direction):
    nb, seq = grp.nb, grp.seq
    nbg = nb // SUB
    tc = min(LRU_TC, seq)
    nch = seq // tc
    R = tc * SUB
    W = LRU_WIDTH

    def cidx(j):
        return j if direction == 0 else nch - 1 - j

    tm_spec = pl.BlockSpec((None, R, W), lambda g, j: (g, cidx(j), 0))
    tm_sds = jax.ShapeDtypeStruct((nbg, seq * SUB, W), f32)
    st_spec = pl.BlockSpec((SUB, W), lambda g, j: (g, 0))
    st_sds = jax.ShapeDtypeStruct((nb, W), f32)
    row_spec = pl.BlockSpec((1, W), lambda g, j: (0, 0))
    gate_w_spec = pl.BlockSpec((LRU_BLOCKS, 128, 128), lambda g, j: (0, 0, 0))
    if direction == 0:
        in_specs = [
            pl.BlockSpec((SUB, SUB, W), lambda g, j: (g, jnp.maximum(cidx(j) * (tc // SUB) - 1, 0), LX // W)),
            pl.BlockSpec((SUB, tc, W), lambda g, j: (g, cidx(j), LX // W)),
            pl.BlockSpec((SUB, SUB, W),
                         lambda g, j: (g, jnp.minimum((cidx(j) + 1) * (tc // SUB), seq // SUB - 1), LX // W)),
            pl.BlockSpec((CONV_W, W), lambda g, j: (0, 0)),
            row_spec,
        ]
        args = [z3, z3, z3, prm["cw"], prm["cb"]]
    else:
        hf_tm, xc_tm = prev
        in_specs = [tm_spec]
        args = [xc_tm]
    in_specs += [gate_w_spec, row_spec, gate_w_spec, row_spec, row_spec]
    args += [prm["wa"][direction], prm["ba"][direction], prm["wx"][direction], prm["bx"][direction],
             prm["sp"][direction]]
    if s0 is not None:
        in_specs.append(st_spec)
        args.append(s0)
    chunk_scr = pltpu.VMEM((R, W), f32)
    if direction == 0:
        out_specs = [tm_spec, tm_spec, st_spec]
        out_shape = [tm_sds, tm_sds, st_sds]
        scratch = [pltpu.VMEM((R + 3 * SUB, W), f32), chunk_scr, chunk_scr, pltpu.VMEM((SUB, W), f32)]
    else:
        in_specs += [pl.BlockSpec((SUB, tc, W), lambda g, j: (g, cidx(j), LG // W)), tm_spec]
        args += [z3, hf_tm]
        out_specs = [pl.BlockSpec((SUB, tc, W), lambda g, j: (g, cidx(j), 0)), st_spec]
        out_shape = [jax.ShapeDtypeStruct((nb, seq, W), bf16), st_sds]
        scratch = [chunk_scr, chunk_scr, pltpu.VMEM((SUB, W), f32), chunk_scr, pltpu.VMEM((SUB, tc, W), f32)]
    return pl.pallas_call(
        functools.partial(_lru_kernel, tc=tc, nch=nch, direction=direction, has_s0=s0 is not None),
        grid=(nbg, nch),
        in_specs=in_specs,
        out_specs=out_specs,
        out_shape=out_shape,
        scratch_shapes=scratch,
        compiler_params=_cp(2),
        name="rglru_dir%d" % direction,
    )(*args)


def _s5_kernel(*refs, tc, nch, direction, has_s0):
    u_ref, bd_ref, a_ref, cd_ref = refs[:4]
    pos = 4
    s0_ref = None
    if has_s0:
        s0_ref = refs[pos]
        pos += 1
    if direction == 1:
        yf_ref, dv_ref, wg_ref, bg_ref = refs[pos:pos + 4]
        pos += 4
    out_ref, fin_ref = refs[pos], refs[pos + 1]
    u_scr, ub_scr, hs_scr, h_scr = refs[pos + 2: pos + 6]
    if direction == 1:
        y_scr = refs[pos + 6]

    j = pl.program_id(1)

    @pl.when(j == 0)
    def _():
        h_scr[...] = s0_ref[...] if has_s0 else jnp.zeros((SUB, 2 * SSM_N), f32)

    _gather_tm(u_ref, tc, u_scr, 0, unroll=True)
    ub_scr[...] = u_scr[...].astype(bf16)

    cbw = SSM_CB
    for cb in range(SSM_N // cbw):
        cre = slice(cb * cbw, (cb + 1) * cbw)
        cim = slice(SSM_N + cb * cbw, SSM_N + (cb + 1) * cbw)
        cin = slice(cb * SSM_CI, (cb + 1) * SSM_CI)
        hs_scr[:, cre] = jnp.dot(ub_scr[:, cin], bd_ref[0, cb], preferred_element_type=f32)
        hs_scr[:, cim] = jnp.dot(ub_scr[:, cin], bd_ref[1, cb], preferred_element_type=f32)
        a_re = jnp.broadcast_to(a_ref[0:1, cre], (SUB, cbw))
        a_im = jnp.broadcast_to(a_ref[1:2, cre], (SUB, cbw))

        def step(s, carry):
            hr, hi = carry
            t = s if direction == 0 else tc - 1 - s
            r0 = pl.multiple_of(t * SUB, SUB)
            nr = a_re * hr - a_im * hi + hs_scr[pl.ds(r0, SUB), cre]
            ni = a_re * hi + a_im * hr + hs_scr[pl.ds(r0, SUB), cim]
            hs_scr[pl.ds(r0, SUB), cre] = nr
            hs_scr[pl.ds(r0, SUB), cim] = ni
            return nr, ni

        hr, hi = lax.fori_loop(0, tc, step, (h_scr[:, cre], h_scr[:, cim]), unroll=True)
        h_scr[:, cre] = hr
        h_scr[:, cim] = hi

    fin_ref[...] = h_scr[...]
    y_parts = []
    for cb in range(SSM_N // cbw):
        cre = slice(cb * cbw, (cb + 1) * cbw)
        cim = slice(SSM_N + cb * cbw, SSM_N + (cb + 1) * cbw)
        y_parts.append(jnp.dot(hs_scr[:, cre].astype(bf16), cd_ref[0, cb], preferred_element_type=f32)
                       + jnp.dot(hs_scr[:, cim].astype(bf16), cd_ref[1, cb], preferred_element_type=f32))
    y = jnp.concatenate(y_parts, axis=1)
    if direction == 0:
        out_ref[...] = y
    else:
        yy = jax.nn.gelu(yf_ref[...] + y + dv_ref[...] * u_scr[...])
        gl = jnp.dot(yy.astype(bf16), wg_ref[...], preferred_element_type=f32) + bg_ref[...]
        u_scr[...] = yy * _sigmoid(gl)
        _scatter_bm(u_scr, tc, y_scr, unroll=True)
        out_ref[...] = y_scr[...].astype(bf16)


def _s5_dir(z3, yf_tm, s0, prm, grp, direction):
    nb, seq = grp.nb, grp.seq
    nbg = nb // SUB
    tc = min(SSM_TC, seq)
    nch = seq // tc
    R = tc * SUB
    W = SSM_WIDTH
    ncb = SSM_N // SSM_CB

    def cidx(j):
        return j if direction == 0 else nch - 1 - j

    tm_spec = pl.BlockSpec((None, R, W), lambda g, j: (g, cidx(j), 0))
    st_spec = pl.BlockSpec((SUB, 2 * SSM_N), lambda g, j: (g, 0))
    in_specs = [
        pl.BlockSpec((SUB, tc, W), lambda g, j: (g, cidx(j), SU // W)),
        pl.BlockSpec((2, ncb, SSM_CI, SSM_CB), lambda g, j: (0, 0, 0, 0)),
        pl.BlockSpec((2, SSM_N), lambda g, j: (0, 0)),
        pl.BlockSpec((2, ncb, SSM_CB, SSM_CI), lambda g, j: (0, 0, 0, 0)),
    ]
    args = [z3, prm["bd"][direction], prm["a"][direction], prm["cd"][direction]]
    if s0 is not None:
        in_specs.append(st_spec)
        args.append(s0)
    scratch = [pltpu.VMEM((R, W), f32), pltpu.VMEM((R, W), bf16), pltpu.VMEM((R, 2 * SSM_N), f32),
               pltpu.VMEM((SUB, 2 * SSM_N), f32)]
    if direction == 0:
        out_spec = tm_spec
        out_sds = jax.ShapeDtypeStruct((nbg, seq * SUB, W), f32)
    else:
        in_specs += [tm_spec, pl.BlockSpec((1, W), lambda g, j: (0, 0)),
                     pl.BlockSpec((W, W), lambda g, j: (0, 0)), pl.BlockSpec((1, W), lambda g, j: (0, 0))]
        args += [yf_tm, prm["d"], prm["wglu"], prm["bglu"]]
        out_spec = pl.BlockSpec((SUB, tc, W), lambda g, j: (g, cidx(j), 0))
        out_sds = jax.ShapeDtypeStruct((nb, seq, W), bf16)
        scratch.append(pltpu.VMEM((SUB, tc, W), f32))
    return pl.pallas_call(
        functools.partial(_s5_kernel, tc=tc, nch=nch, direction=direction, has_s0=s0 is not None),
        grid=(nbg, nch),
        in_specs=in_specs,
        out_specs=[out_spec, st_spec],
        out_shape=[out_sds, jax.ShapeDtypeStruct((nb, 2 * SSM_N), f32)],
        scratch_shapes=scratch,
        compiler_params=_cp(2),
        name="s5_dir%d" % direction,
    )(*args)


def _rope(x, cos, sin_signed):
    return x * cos + pltpu.roll(x, HEAD_DIM // 2, 1) * sin_signed


def _unit_rms(x, w):
    return x * lax.rsqrt(jnp.mean(x * x, axis=-1, keepdims=True) + EPS) * w


def _attn_kernel(*refs, seq, latent, sub_rows):
    q_ref, k_ref, v_ref, qw_ref, kw_ref, sink_ref = refs[:6]
    pos = 6
    if latent:
        cos_ref, sin_ref, ck_ref, cv_ref = refs[pos:pos + 4]
        pos += 4
    o_ref = refs[pos]
    pos += 1
    if not latent:
        kout_ref, vout_ref = refs[pos], refs[pos + 1]
        pos += 2
    kb_scr, vb_scr = refs[pos], refs[pos + 1]
    pos += 2
    if latent:
        ckb_scr, cvb_scr = refs[pos], refs[pos + 1]

    HD = HEAD_DIM
    qb = pl.program_id(1)

    @pl.when(qb == 0)
    def _():
        for hk in range(ATT_KV_HEADS):
            sl = slice(hk * HD, (hk + 1) * HD)
            kn = _unit_rms(k_ref[:, sl], kw_ref[...])
            if latent:
                kn = _rope(kn, cos_ref[...], sin_ref[...])
            else:
                kout_ref[:, sl] = kn
            kb_scr[:, sl] = kn.astype(bf16)
        v = v_ref[...]
        vb_scr[...] = v.astype(bf16)
        if latent:
            ckb_scr[...] = ck_ref[...].astype(bf16)
            cvb_scr[...] = cv_ref[...].astype(bf16)
        else:
            vout_ref[...] = v

    nt_dims = (((1,), (1,)), ((), ()))

    def lane_tiles(x):
        return [x[:, t * HD:(t + 1) * HD] for t in range(x.shape[1] // HD)]

    def fold(op, tiles):
        acc = tiles[0]
        for t in tiles[1:]:
            acc = op(acc, t)
        return acc

    for sub in range(q_ref.shape[0] // sub_rows):
        QB = sub_rows
        rows = slice(sub * QB, (sub + 1) * QB)
        q0 = pl.multiple_of(qb * q_ref.shape[0] + sub * QB, QB)
        if latent:
            win = QB + 2 * WINDOW
            start = pl.multiple_of(jnp.clip(q0 - WINDOW, 0, seq - win), WINDOW)
            qpos = q0 + (lax.broadcasted_iota(jnp.int32, (2 * QB, win), 0) & (QB - 1))
            kpos = start + lax.broadcasted_iota(jnp.int32, (2 * QB, win), 1)
            band = jnp.abs(qpos - kpos) <= WINDOW
            cos_q = cos_ref[pl.ds(q0, QB), :]
            sin_q = sin_ref[pl.ds(q0, QB), :]
        else:
            win = seq
            start = 0
        row = lax.broadcasted_iota(jnp.int32, (2 * QB, 1), 0)
        for hk in range(ATT_KV_HEADS):
            sl = slice(hk * HD, (hk + 1) * HD)
            qs = []
            for g in range(2):
                hq = hk * 2 + g
                qn = _unit_rms(q_ref[rows, hq * HD:(hq + 1) * HD], qw_ref[...])
                if latent:
                    qn = _rope(qn, cos_q, sin_q)
                qs.append((qn * (HD ** -0.5)).astype(bf16))
            qg = jnp.concatenate(qs, axis=0)
            snk = jnp.where(row < QB, sink_ref[hk * 2], sink_ref[hk * 2 + 1])
            s1 = lax.dot_general(qg, kb_scr[pl.ds(start, win), sl], nt_dims, preferred_element_type=f32)
            if latent:
                s1 = jnp.where(band, s1, NEG_INF)
                s2 = lax.dot_general(qg, ckb_scr[:, sl], nt_dims, preferred_element_type=f32)
            s_tiles = lane_tiles(s1) + (lane_tiles(s2) if latent else [])
            m = jnp.maximum(jnp.max(fold(jnp.maximum, s_tiles), axis=-1, keepdims=True), snk)
            p1 = jnp.exp(s1 - m)
            if latent:
                p2 = jnp.exp(s2 - m)
            p_tiles = lane_tiles(p1) + (lane_tiles(p2) if latent else [])
            den = jnp.sum(fold(jnp.add, p_tiles), axis=-1, keepdims=True) + jnp.exp(snk - m)
            o = jnp.dot((p1 / den).astype(bf16), vb_scr[pl.ds(start, win), sl], preferred_element_type=f32)
            if latent:
                o += jnp.dot((p2 / den).astype(bf16), cvb_scr[:, sl], preferred_element_type=f32)
            for g in range(2):
                hq = hk * 2 + g
                o_ref[rows, hq * HD:(hq + 1) * HD] = o[g * QB:(g + 1) * QB].astype(bf16)


def _attention(z, qw, kw, sink, rope, cache, grp):
    nb, seq = grp.nb, grp.seq
    latent = not grp.ctx
    QB = min(seq, ATT_QB_LAT if latent else ATT_QB_CTX)
    sub_rows = min(QB, ATT_SUB_LAT) if latent else QB
    nqb = seq // QB
    kvw = ATT_KV_HEADS * HEAD_DIM
    in_specs = [
        pl.BlockSpec((QB, 512), lambda b, i: (b * nqb + i, AQ // 512)),
        pl.BlockSpec((seq, kvw), lambda b, i: (b, AK // kvw)),
        pl.BlockSpec((seq, kvw), lambda b, i: (b, AV // kvw)),
        pl.BlockSpec((1, HEAD_DIM), lambda b, i: (0, 0)),
        pl.BlockSpec((1, HEAD_DIM), lambda b, i: (0, 0)),
        pl.BlockSpec(memory_space=pltpu.SMEM),
    ]
    args = [z, z, z, qw, kw, sink]
    scratch = [pltpu.VMEM((seq, kvw), bf16), pltpu.VMEM((seq, kvw), bf16)]
    out_specs = [pl.BlockSpec((QB, 512), lambda b, i: (b * nqb + i, 0))]
    out_shape = [jax.ShapeDtypeStruct((nb * seq, ATT_Q_HEADS * HEAD_DIM), bf16)]
    if latent:
        in_specs += [pl.BlockSpec((seq, HEAD_DIM), lambda b, i: (0, 0)),
                     pl.BlockSpec((seq, HEAD_DIM), lambda b, i: (0, 0)),
                     pl.BlockSpec((None, PAST_LEN, kvw), lambda b, i: (b, 0, 0)),
                     pl.BlockSpec((None, PAST_LEN, kvw), lambda b, i: (b, 0, 0))]
        args += [rope[0], rope[1], cache[0], cache[1]]
        scratch += [pltpu.VMEM((PAST_LEN, kvw), bf16), pltpu.VMEM((PAST_LEN, kvw), bf16)]
    else:
        kv_out = pl.BlockSpec((seq, kvw), lambda b, i: (b, 0))
        out_specs += [kv_out, kv_out]
        out_shape += [jax.ShapeDtypeStruct((nb * seq, kvw), f32)] * 2
    return pl.pallas_call(
        functools.partial(_attn_kernel, seq=seq, latent=latent, sub_rows=sub_rows),
        grid=(nb, nqb),
        in_specs=in_specs,
        out_specs=out_specs,
        out_shape=out_shape,
        scratch_shapes=scratch,
        compiler_params=_cp(2),
        name="attention_lat" if latent else "attention_ctx",
    )(*args)


def _merge_kernel(h_ref, o0, o1, o2, o3, g0, g1, g2, g3, wbr_ref, *rest):
    out_ref = rest[len(rest) // 2]
    h = h_ref[...]
    acc = None
    for b, (o_ref, g_ref) in enumerate(((o0, g0), (o1, g1), (o2, g2), (o3, g3))):
        gate = _sigmoid(jnp.dot(h, g_ref[...], preferred_element_type=f32))
        t = gate * jnp.dot(o_ref[...], wbr_ref[b], preferred_element_type=f32)
        acc = t if acc is None else acc + t
    out_ref[...] = acc.astype(bf16)
    n_cast = len(rest) // 2
    for src, dst in zip(rest[:n_cast], rest[n_cast + 1:]):
        dst[...] = src[...].astype(bf16)


def _merge(h, branches, w_in, w_br, grp, cast_next=None):
    tn = 512
    g0 = ZW // tn
    gb = D_MODEL // tn
    n_i, n_j = grp.n // TM, D_MODEL // tn
    o_spec = pl.BlockSpec((TM, BRANCH_WIDTH), lambda i, j: (i, 0))

    def gate_spec(b):
        return pl.BlockSpec((D_MODEL, tn), lambda i, j: (0, g0 + b * gb + j))

    in_specs = [
        pl.BlockSpec((TM, D_MODEL), lambda i, j: (i, 0)),
        o_spec, o_spec, o_spec, o_spec,
        gate_spec(0), gate_spec(1), gate_spec(2), gate_spec(3),
        pl.BlockSpec((N_BRANCH, BRANCH_WIDTH, tn), lambda i, j: (0, 0, j)),
    ]
    args = [h, *branches, w_in, w_in, w_in, w_in, w_br]
    out_specs = [pl.BlockSpec((TM, tn), lambda i, j: (i, j))]
    out_shape = [jax.ShapeDtypeStruct((grp.n, D_MODEL), bf16)]
    if cast_next is not None:
        nxt, stacks = cast_next
        for w in stacks:
            rows, cols = w.shape[1:]
            slab = rows // (n_i * n_j)
            in_specs.append(pl.BlockSpec((None, slab, cols), lambda i, j: (nxt, i * n_j + j, 0)))
            out_specs.append(pl.BlockSpec((slab, cols), lambda i, j: (i * n_j + j, 0)))
            out_shape.append(jax.ShapeDtypeStruct((rows, cols), bf16))
            args.append(w)
    res = pl.pallas_call(
        _merge_kernel,
        grid=(n_i, n_j),
        in_specs=in_specs,
        out_specs=out_specs,
        out_shape=out_shape,
        compiler_params=_cp(2, vmem=BIG_VMEM_LIMIT if cast_next is not None else VMEM_LIMIT),
        name="merge",
    )(*args)
    return res[0], res[1:]


def _outproj_kernel(x_ref, mod_ref, m_ref, w_ref, o_ref):
    o_ref[...] = x_ref[...] + mod_ref[2:3, :] * jnp.dot(m_ref[...], w_ref[...], preferred_element_type=f32)


def _out_proj(x, mod, merged, w_out, layer, grp):
    tm = OUT_TM
    return pl.pallas_call(
        _outproj_kernel,
        grid=(grp.n // tm,),
        in_specs=[
            pl.BlockSpec((tm, D_MODEL), lambda i: (i, 0)),
            pl.BlockSpec((None, None, 6, D_MODEL), lambda i: (layer, _mod_row(grp, i, tm), 0, 0)),
            pl.BlockSpec((tm, D_MODEL), lambda i: (i, 0)),
            pl.BlockSpec((D_MODEL, D_MODEL), lambda i: (0, 0)),
        ],
        out_specs=pl.BlockSpec((tm, D_MODEL), lambda i: (i, 0)),
        out_shape=jax.ShapeDtypeStruct((grp.n, D_MODEL), f32),
        compiler_params=_cp(1, vmem=BIG_VMEM_LIMIT),
        name="out_proj",
    )(x, mod, merged, w_out)


def _ffn_kernel(x_ref, mod_ref, nw_ref, w1_ref, w2_ref, o_ref, h_scr, r_scr, gs_scr):
    j = pl.program_id(1)

    @pl.when(j == 0)
    def _():
        _modnorm_rows(x_ref, nw_ref[...], mod_ref[3:4, :], mod_ref[4:5, :], h_scr, r_scr, gs_scr, copy_ref=o_ref)

    a = jnp.maximum(jnp.dot(h_scr[...], w1_ref[...], preferred_element_type=f32), 0.0)
    a2 = (a * a).astype(bf16)
    cw = 512
    for cidx in range(D_MODEL // cw):
        cs = slice(cidx * cw, (cidx + 1) * cw)
        o_ref[:, cs] += mod_ref[5:6, cs] * jnp.dot(a2, w2_ref[:, cs], preferred_element_type=f32)


def _ffn(x, mod, nw, w1, w2, layer, grp):
    tf = 1024
    nj = D_FF // tf
    return pl.pallas_call(
        _ffn_kernel,
        grid=(grp.n // TM, nj),
        in_specs=[
            pl.BlockSpec((TM, D_MODEL), lambda i, j: (i, 0)),
            pl.BlockSpec((None, None, 6, D_MODEL), lambda i, j: (layer, _mod_row(grp, i), 0, 0)),
            pl.BlockSpec((None, 1, D_MODEL), lambda i, j: (layer, 0, 0)),
            pl.BlockSpec((D_MODEL, tf), lambda i, j: (0, j)),
            pl.BlockSpec((tf, D_MODEL), lambda i, j: (j, 0)),
        ],
        out_specs=pl.BlockSpec((TM, D_MODEL), lambda i, j: (i, 0)),
        out_shape=jax.ShapeDtypeStruct((grp.n, D_MODEL), f32),
        scratch_shapes=[pltpu.VMEM((TM, D_MODEL), bf16), pltpu.VMEM((TM, 128), f32),
                        pltpu.VMEM((2, SUB, D_MODEL), f32)],
        compiler_params=_cp(2, vmem=BIG_VMEM_LIMIT),
        name="ffn",
    )(x, mod, nw, w1, w2)


def _rope_tables(seq):
    t = jnp.arange(seq)
    row = (t // GRID_W).astype(f32)
    col = (t % GRID_W).astype(f32)
    n_pairs = HEAD_DIM // 4
    freqs = ROPE_BASE ** (-jnp.arange(n_pairs, dtype=f32) / n_pairs)
    ang = jnp.concatenate([row[:, None] * freqs, col[:, None] * freqs], axis=-1)
    cos = jnp.cos(ang)
    sin = jnp.sin(ang)
    return jnp.concatenate([cos, cos], axis=-1), jnp.concatenate([-sin, sin], axis=-1)


def _ssm_params(a_re, a_im, log_dt, b_re, b_im, c_re, c_im):
    lead = a_re.shape[:-2]
    dt = jnp.exp(log_dt)[..., None]
    mag = jnp.exp(a_re * dt)
    ab_re = mag * jnp.cos(a_im * dt)
    ab_im = mag * jnp.sin(a_im * dt)
    den = a_re * a_re + a_im * a_im
    q_re = (((ab_re - 1.0) * a_re + ab_im * a_im) / den)[..., None]
    q_im = ((ab_im * a_re - (ab_re - 1.0) * a_im) / den)[..., None]
    bb_re = q_re * b_re - q_im * b_im
    bb_im = q_re * b_im + q_im * b_re
    a = jnp.stack([ab_re.reshape(lead + (SSM_N,)), ab_im.reshape(lead + (SSM_N,))], axis=-2)

    ncb = SSM_N // SSM_CB
    gpb = SSM_GROUPS // ncb
    eye = jnp.eye(gpb, dtype=f32)[:, None, :, None]

    def diag_blocks(w, rows, cols):
        w = jnp.swapaxes(w.reshape(lead + (ncb, gpb) + w.shape[-2:]), -1, -2)
        return (w[..., None, :] * eye).reshape(lead + (ncb, rows, cols))

    bd = jnp.stack([diag_blocks(w, SSM_CI, SSM_CB) for w in (bb_re, bb_im)], axis=-4).astype(bf16)
    cd = jnp.stack([diag_blocks(w, SSM_CB, SSM_CI) for w in (c_re, -c_im)], axis=-4).astype(bf16)
    return a, bd, cd


def _lru_params(conv_w, conv_b, wa, ba, wx, bx, lam):
    w = LRU_WIDTH
    return {"cw": conv_w, "cb": conv_b.reshape(1, w),
            "wa": (0.5 * wa).astype(bf16), "ba": (0.5 * ba).reshape(2, 1, w),
            "wx": (0.5 * wx).astype(bf16), "bx": (0.5 * bx).reshape(2, 1, w),
            "sp": (-0.5 * LRU_C * jax.nn.softplus(-lam)).reshape(2, 1, w)}


def _mixers(z, grp, layer, prm, states):
    nb, seq = grp.nb, grp.seq
    z3 = z.reshape(nb, seq, ZW)
    s_ret0, s_lru0, s_ssm0, cache = states

    o_ret, s_ret = _retention(z, prm["ret_lg"], prm["ret_gn"], s_ret0, grp)

    hf, xc, lru_f = _lru_dir(z3, None, None if s_lru0 is None else s_lru0[:, 0], prm["lru"], grp, 0)
    o_lru, lru_b = _lru_dir(z3, (hf, xc), None if s_lru0 is None else s_lru0[:, 1], prm["lru"], grp, 1)

    att = _attention(z, prm["att_qw"], prm["att_kw"], prm["att_sink"], prm["rope"], cache, grp)

    yf, ssm_f = _s5_dir(z3, None, None if s_ssm0 is None else s_ssm0[:, 0], prm["ssm"], grp, 0)
    o_ssm, ssm_b = _s5_dir(z3, yf, None if s_ssm0 is None else s_ssm0[:, 1], prm["ssm"], grp, 1)

    branches = [o_ret, o_lru.reshape(grp.n, LRU_WIDTH), att[0], o_ssm.reshape(grp.n, SSM_WIDTH)]
    new_state = None
    if grp.ctx:
        ssm_fin = jnp.stack([ssm_f, ssm_b], axis=1)
        new_state = (att[1].reshape(nb, seq, ATT_KV_HEADS, HEAD_DIM), att[2].reshape(nb, seq, ATT_KV_HEADS, HEAD_DIM),
                     s_ret, jnp.stack([lru_f, lru_b], axis=1),
                     ssm_fin[:, :, :SSM_N].reshape(nb, 2, SSM_GROUPS, SSM_STATE),
                     ssm_fin[:, :, SSM_N:].reshape(nb, 2, SSM_GROUPS, SSM_STATE))
    return branches, new_state


def kernel(x_prompt, x_sample, cache_attn_k, cache_attn_v, state_ret, state_lru, state_ssm_re, state_ssm_im, c, c_ctx, w_mod, b_mod, norm1, w_in, ret_decay_logit, ret_gn, lru_conv_w, lru_conv_b, lru_wa, lru_ba, lru_wx, lru_bx, lru_lambda, att_q_norm, att_k_norm, att_sink, ssm_a_re, ssm_a_im, ssm_log_dt, ssm_b_re, ssm_b_im, ssm_c_re, ssm_c_im, ssm_d, ssm_w_glu, ssm_b_glu, w_br, w_out, norm2, w_ff1, w_ff2):
    xs = {PROMPT: x_prompt.reshape(PROMPT.n, D_MODEL), SAMPLE: x_sample.reshape(SAMPLE.n, D_MODEL)}
    cond = jnp.concatenate([c, c_ctx[None, :], jnp.zeros((COND_ROWS - SAMPLE.nb - 1, D_MODEL), f32)], axis=0)
    mod = _modulation(cond, w_mod, b_mod).reshape(DEPTH, COND_ROWS, 6, D_MODEL)

    w_stacks = (w_in, w_br.reshape(DEPTH, N_BRANCH * BRANCH_WIDTH, D_MODEL), w_out, w_ff1, w_ff2)
    w_layer = [w_stacks[0][0].astype(bf16), w_stacks[1][0].astype(bf16), None, None, None]
    norm1_r = norm1.reshape(DEPTH, 1, D_MODEL)
    norm2_r = norm2.reshape(DEPTH, 1, D_MODEL)
    rope = _rope_tables(SAMPLE.seq)
    log_gamma = -jax.nn.softplus(-ret_decay_logit)
    kvw = ATT_KV_HEADS * HEAD_DIM

    ssm_a_all, ssm_bd_all, ssm_cd_all = _ssm_params(ssm_a_re, ssm_a_im, ssm_log_dt, ssm_b_re, ssm_b_im,
                                                    ssm_c_re, ssm_c_im)
    new_states = []
    for l in range(DEPTH):
        ssm_a, ssm_bd, ssm_cd = ssm_a_all[l], ssm_bd_all[l], ssm_cd_all[l]
        prm = {
            "ret_lg": jnp.broadcast_to(log_gamma[l].T[:, :, None], (RET_HEADS, 2, 128)),
            "ret_gn": ret_gn[l].reshape(1, RET_HEADS * 128),
            "lru": _lru_params(lru_conv_w[l], lru_conv_b[l], lru_wa[l], lru_ba[l], lru_wx[l], lru_bx[l],
                               lru_lambda[l]),
            "att_qw": att_q_norm[l].reshape(1, HEAD_DIM), "att_kw": att_k_norm[l].reshape(1, HEAD_DIM),
            "att_sink": att_sink[l], "rope": rope,
            "ssm": {"a": ssm_a, "bd": ssm_bd, "cd": ssm_cd, "d": ssm_d[l].reshape(1, SSM_WIDTH),
                    "wglu": ssm_w_glu[l].astype(bf16), "bglu": ssm_b_glu[l].reshape(1, SSM_WIDTH)},
        }
        states = {
            PROMPT: (None, None, None, None),
            SAMPLE: (state_ret[:, l], state_lru[:, l],
                     jnp.concatenate([state_ssm_re[:, l].reshape(SAMPLE.nb, 2, SSM_N),
                                      state_ssm_im[:, l].reshape(SAMPLE.nb, 2, SSM_N)], axis=-1),
                     (cache_attn_k[:, l].reshape(SAMPLE.nb, PAST_LEN, kvw),
                      cache_attn_v[:, l].reshape(SAMPLE.nb, PAST_LEN, kvw))),
        }
        w_in_l, w_br_l, w_out_l, w_ff1_l, w_ff2_l = w_layer
        w_br_l = w_br_l.reshape(N_BRANCH, BRANCH_WIDTH, D_MODEL)
        for grp in (PROMPT, SAMPLE):
            x = xs[grp]
            z, h = _proj_in(x, mod, norm1_r, w_in_l, l, grp)
            branches, st = _mixers(z, grp, l, prm, states[grp])
            if grp.ctx:
                new_states.append(st)
            cast = None
            if grp is SAMPLE and l + 1 < DEPTH:
                cast = (l + 1, w_stacks)
            elif grp is PROMPT and l == 0:
                cast = (0, w_stacks[2:])
            merged, w_cast = _merge(h, branches, w_in_l, w_br_l, grp, cast)
            if cast is not None and cast[0] == l:
                w_out_l, w_ff1_l, w_ff2_l = w_cast
            elif cast is not None:
                w_layer = list(w_cast)
            x = _out_proj(x, mod, merged, w_out_l, l, grp)
            xs[grp] = _ffn(x, mod, norm2_r, w_ff1_l, w_ff2_l, l, grp)

    y_prompt = xs[PROMPT].reshape(PROMPT.nb, PROMPT.seq, D_MODEL)
    y_sample = xs[SAMPLE].reshape(SAMPLE.nb, SAMPLE.seq, D_MODEL)
    return (y_prompt, y_sample) + tuple(jnp.stack([st[i] for st in new_states], axis=1) for i in range(6))
```
